```python
import jax, jax.numpy as jnp
from jax import lax
import numpy as np

D_MODEL = 1024
BATCH = 8
SEQ = 2048
DEPTH = 1
DEC_BATCH = 8
DEC_SEQ = 8192
PAST_LEN = 128

GRID_W = 64
HEAD_DIM = 64
N_Q_HEADS = 8
N_KV_HEADS = 2
ATTN_WIDTH = N_Q_HEADS * HEAD_DIM
KV_WIDTH = N_KV_HEADS * HEAD_DIM
Q_BLOCK = 128
ROPE_THETA = 10000.0
GLA_HEADS = 4
GLA_DK = 64
GLA_DV = 128
GLA_KW = GLA_HEADS * GLA_DK
GLA_VW = GLA_HEADS * GLA_DV
GLA_LOWRANK = 16
GLA_GATE_NORM = 16.0
GLA_CHUNK = 64
N_EXPERTS = 16
EXPERT_FF = 1024
EC_CAPACITY = 2
PLE_DIM = 256
NORM_EPS = 1e-6
IN_WIDTH = (ATTN_WIDTH + 2 * KV_WIDTH + 2 * GLA_KW + 2 * GLA_VW
            + 2 * GLA_LOWRANK + 2 * D_MODEL)

kernel_name = "hybrid_gqa_gla_ec_encoder"


def rms_norm(x, g):
    xf = x.astype(jnp.float32)
    y = xf * lax.rsqrt(jnp.mean(xf * xf, axis=-1, keepdims=True) + NORM_EPS)
    return (y * g.astype(jnp.float32)).astype(x.dtype)


def axial_angles(T):
    rows = T // GRID_W
    r, c = jnp.meshgrid(jnp.arange(rows), jnp.arange(GRID_W), indexing="ij")
    r = r.reshape(-1).astype(jnp.float32)
    c = c.reshape(-1).astype(jnp.float32)
    half = HEAD_DIM // 2
    inv = ROPE_THETA ** (-jnp.arange(0, half, 2, dtype=jnp.float32) / half)
    return r[:, None] * inv, c[:, None] * inv


def rotate_half(x, ang):
    cos = jnp.cos(ang)[:, None, :]
    sin = jnp.sin(ang)[:, None, :]
    x1, x2 = jnp.split(x, 2, axis=-1)
    return jnp.concatenate([x1 * cos - x2 * sin, x2 * cos + x1 * sin], axis=-1)


def apply_axial_rope(x, ang_r, ang_c):
    xr, xc = jnp.split(x, 2, axis=-1)
    return jnp.concatenate([rotate_half(xr, ang_r), rotate_half(xc, ang_c)], axis=-1)


def gqa_attention(q, k, v, q_norm, k_norm):
    B, T, _ = q.shape
    dt = q.dtype
    G = N_Q_HEADS // N_KV_HEADS
    q = q.reshape(B, T, N_Q_HEADS, HEAD_DIM)
    k = k.reshape(B, T, N_KV_HEADS, HEAD_DIM)
    v = v.reshape(B, T, N_KV_HEADS, HEAD_DIM)
    ang_r, ang_c = axial_angles(T)
    q = apply_axial_rope(rms_norm(q, q_norm).astype(jnp.float32), ang_r, ang_c).astype(dt)
    k = apply_axial_rope(rms_norm(k, k_norm).astype(jnp.float32), ang_r, ang_c).astype(dt)
    nb = T // Q_BLOCK
    qb = q.reshape(B, nb, Q_BLOCK, N_KV_HEADS, G, HEAD_DIM).transpose(1, 0, 2, 3, 4, 5)
    scale = HEAD_DIM ** -0.5

    def block(qblk):
        s = jnp.einsum("bqkgd,bskd->bkgqs", qblk, k).astype(jnp.float32) * scale
        p = jax.nn.softmax(s, axis=-1).astype(v.dtype)
        return jnp.einsum("bkgqs,bskd->bqkgd", p, v)

    o = lax.map(block, qb)
    return o.transpose(1, 0, 2, 3, 4, 5).reshape(B, T, ATTN_WIDTH)


def gla_scan(q, k, v, g, strict):
    B, T, H, dk = q.shape
    dv = v.shape[-1]
    L = GLA_CHUNK
    n = T // L

    def chunks(a):
        return a.reshape(B, n, L, H, a.shape[-1]).transpose(1, 0, 3, 2, 4)

    mask = jnp.tril(jnp.ones((L, L), dtype=bool), k=-1 if strict else 0)

    def step(S, inp):
        qc, kc, vc, gc = inp
        b = jnp.cumsum(gc, axis=2)
        diff = b[:, :, :, None, :] - b[:, :, None, :, :]
        decay = jnp.exp(jnp.where(mask[:, :, None], diff, -jnp.inf))
        A = jnp.einsum("bhtd,bhsd,bhtsd->bhts", qc, kc, decay)
        o = jnp.einsum("bhts,bhse->bhte", A, vc) + jnp.einsum("bhtd,bhde->bhte", qc * jnp.exp(b), S)
        b_last = b[:, :, -1:, :]
        S = (jnp.exp(b_last[:, :, 0, :])[..., None] * S
             + jnp.einsum("bhsd,bhse->bhde", kc * jnp.exp(b_last - b), vc))
        return S, o

    S0 = jnp.zeros((B, H, dk, dv), jnp.float32)
    _, o = lax.scan(step, S0, (chunks(q), chunks(k), chunks(v), chunks(g)))
    return o.transpose(1, 0, 3, 2, 4).reshape(B, T, H, dv)


def gla_branch(q, k, v, r, a_f, a_b, w_af, b_af, w_ab, b_ab, gla_norm):
    B, T, _ = q.shape
    f32 = jnp.float32
    qf = q.astype(f32).reshape(B, T, GLA_HEADS, GLA_DK) * (GLA_DK ** -0.5)
    kf = k.astype(f32).reshape(B, T, GLA_HEADS, GLA_DK)
    vf = v.astype(f32).reshape(B, T, GLA_HEADS, GLA_DV)

    def log_gate(a, w, b):
        z = a.astype(f32) @ w.astype(f32) + b.astype(f32)
        return (jax.nn.log_sigmoid(z) / GLA_GATE_NORM).reshape(B, T, GLA_HEADS, GLA_DK)

    g_f = log_gate(a_f, w_af, b_af)
    g_b = log_gate(a_b, w_ab, b_ab)
    o_f = gla_scan(qf, kf, vf, g_f, False)
    flip = lambda t: jnp.flip(t, axis=1)
    o_b = flip(gla_scan(flip(qf), flip(kf), flip(vf), flip(g_b), True))
    o = rms_norm(o_f + o_b, gla_norm).reshape(B, T, GLA_VW)
    return (o * jax.nn.silu(r.astype(f32))).astype(r.dtype)


def expert_choice_ffn(h, w_router, w_gate, w_up, w_down):
    B, T, D = h.shape
    N = B * T
    tok = h.reshape(N, D)
    aff = jax.nn.softmax((tok @ w_router).astype(jnp.float32), axis=-1)
    C = EC_CAPACITY * N // N_EXPERTS
    gate, idx = lax.top_k(aff.T, C)
    xe = tok[idx]
    hid = jax.nn.silu(jnp.einsum("ecd,edf->ecf", xe, w_gate)) * jnp.einsum("ecd,edf->ecf", xe, w_up)
    ye = jnp.einsum("ecf,efd->ecd", hid, w_down) * gate[..., None].astype(tok.dtype)
    out = jnp.zeros_like(tok).at[idx.reshape(-1)].add(ye.reshape(-1, D))
    return out.reshape(B, T, D)


def hybrid_layer(x, p, g_mix, w_in, q_norm, k_norm, w_af, b_af, w_ab, b_ab, gla_norm,
                 w_attn_up, w_gla_up, w_out, g_moe, w_router, w_gate_e, w_up_e, w_down_e,
                 g_ple, w_ple_gate, w_ple_proj):
    h = rms_norm(x, g_mix)
    proj = h @ w_in
    sizes = (ATTN_WIDTH, KV_WIDTH, KV_WIDTH, GLA_KW, GLA_KW, GLA_VW, GLA_VW,
             GLA_LOWRANK, GLA_LOWRANK, D_MODEL, D_MODEL)
    offs = [int(o) for o in np.cumsum(sizes)[:-1]]
    aq, ak, av, lq, lk, lv, lr, laf, lab, gate_a, gate_l = jnp.split(proj, offs, axis=-1)
    o_attn = gqa_attention(aq, ak, av, q_norm, k_norm) @ w_attn_up
    o_gla = gla_branch(lq, lk, lv, lr, laf, lab, w_af, b_af, w_ab, b_ab, gla_norm) @ w_gla_up
    merged = jax.nn.sigmoid(gate_a) * o_attn + jax.nn.sigmoid(gate_l) * o_gla
    x = x + merged @ w_out
    x = x + expert_choice_ffn(rms_norm(x, g_moe), w_router, w_gate_e, w_up_e, w_down_e)
    x = x + jax.nn.sigmoid(rms_norm(x, g_ple) @ w_ple_gate) * (p @ w_ple_proj)
    return x


def trunk(x, p, g_mix, w_in, q_norm, k_norm, w_af, b_af, w_ab, b_ab, gla_norm,
          w_attn_up, w_gla_up, w_out, g_moe, w_router, w_gate_e, w_up_e, w_down_e,
          g_ple, w_ple_gate, w_ple_proj, g_final):
    for i in range(DEPTH):
        x = hybrid_layer(x, p[i], g_mix[i], w_in[i], q_norm[i], k_norm[i], w_af[i], b_af[i],
                         w_ab[i], b_ab[i], gla_norm[i], w_attn_up[i], w_gla_up[i], w_out[i],
                         g_moe[i], w_router[i], w_gate_e[i], w_up_e[i], w_down_e[i],
                         g_ple[i], w_ple_gate[i], w_ple_proj[i])
    return rms_norm(x, g_final)


def setup_inputs(seed: int = 0) -> dict:
    key = jax.random.key(seed)
    ks = jax.random.split(key, 32)
    f32 = jnp.float32
    nrm = lambda k, s, sc: jax.random.normal(k, s, f32) * sc
    gain = lambda k, s: 1.0 + 0.02 * jax.random.normal(k, s, f32)
    D = D_MODEL
    return {
        "x_prompt": nrm(ks[0], (BATCH, SEQ, D), 1.0),
        "x_sample": nrm(ks[1], (DEC_BATCH, DEC_SEQ, D), 1.0),
        "p_prompt": nrm(ks[2], (DEPTH, BATCH, SEQ, PLE_DIM), 1.0),
        "p_sample": nrm(ks[3], (DEPTH, DEC_BATCH, DEC_SEQ, PLE_DIM), 1.0),
        "g_mix": gain(ks[4], (DEPTH, D)),
        "w_in": nrm(ks[5], (DEPTH, D, IN_WIDTH), D ** -0.5),
        "q_norm": gain(ks[6], (DEPTH, HEAD_DIM)),
        "k_norm": gain(ks[7], (DEPTH, HEAD_DIM)),
        "w_af": nrm(ks[8], (DEPTH, GLA_LOWRANK, GLA_KW), GLA_LOWRANK ** -0.5),
        "b_af": nrm(ks[9], (DEPTH, GLA_KW), 0.1),
        "w_ab": nrm(ks[10], (DEPTH, GLA_LOWRANK, GLA_KW), GLA_LOWRANK ** -0.5),
        "b_ab": nrm(ks[11], (DEPTH, GLA_KW), 0.1),
        "gla_norm": gain(ks[12], (DEPTH, GLA_DV)),
        "w_attn_up": nrm(ks[13], (DEPTH, ATTN_WIDTH, D), ATTN_WIDTH ** -0.5),
        "w_gla_up": nrm(ks[14], (DEPTH, GLA_VW, D), GLA_VW ** -0.5),
        "w_out": nrm(ks[15], (DEPTH, D, D), D ** -0.5),
        "g_moe": gain(ks[16], (DEPTH, D)),
        "w_router": nrm(ks[17], (DEPTH, D, N_EXPERTS), D ** -0.5),
        "w_gate_e": nrm(ks[18], (DEPTH, N_EXPERTS, D, EXPERT_FF), D ** -0.5),
        "w_up_e": nrm(ks[19], (DEPTH, N_EXPERTS, D, EXPERT_FF), D ** -0.5),
        "w_down_e": nrm(ks[20], (DEPTH, N_EXPERTS, EXPERT_FF, D), EXPERT_FF ** -0.5),
        "g_ple": gain(ks[21], (DEPTH, D)),
        "w_ple_gate": nrm(ks[22], (DEPTH, D, D), D ** -0.5),
        "w_ple_proj": nrm(ks[23], (DEPTH, PLE_DIM, D), PLE_DIM ** -0.5),
        "g_final": gain(ks[24], (D,)),
    }


def reference(x_prompt, x_sample, p_prompt, p_sample, g_mix, w_in, q_norm, k_norm, w_af, b_af,
              w_ab, b_ab, gla_norm, w_attn_up, w_gla_up, w_out, g_moe, w_router, w_gate_e,
              w_up_e, w_down_e, g_ple, w_ple_gate, w_ple_proj, g_final):
    y_prompt = trunk(x_prompt, p_prompt, g_mix, w_in, q_norm, k_norm, w_af, b_af, w_ab, b_ab,
                     gla_norm, w_attn_up, w_gla_up, w_out, g_moe, w_router, w_gate_e, w_up_e,
                     w_down_e, g_ple, w_ple_gate, w_ple_proj, g_final)
    y_sample = trunk(x_sample, p_sample, g_mix, w_in, q_norm, k_norm, w_af, b_af, w_ab, b_ab,
                     gla_norm, w_attn_up, w_gla_up, w_out, g_moe, w_router, w_gate_e, w_up_e,
                     w_down_e, g_ple, w_ple_gate, w_ple_proj, g_final)
    return (y_prompt, y_sample)
```

```python
import functools

import numpy as np
import jax
import jax.numpy as jnp
from jax import lax
from jax.experimental import pallas as pl
from jax.experimental.pallas import tpu as pltpu

F32 = jnp.float32
BF16 = jnp.bfloat16
I32 = jnp.int32

HEAD_DIM = 64
N_Q_HEADS = 8
N_KV_HEADS = 2
GRID_W = 64
ROPE_THETA = 10000.0
GLA_HEADS = 4
GLA_DK = 64
GLA_DV = 128
GLA_LOWRANK = 16
GLA_GATE_NORM = 16.0
N_EXPERTS = 16
EC_CAPACITY = 2
NORM_EPS = 1e-6

ATTN_W = N_Q_HEADS * HEAD_DIM
KV_W = N_KV_HEADS * HEAD_DIM
GLA_KW = GLA_HEADS * GLA_DK
GLA_VW = GLA_HEADS * GLA_DV

LANES = 128
SUBLANES = 8
VMEM_LIMIT = 56 * 1024 * 1024

GLA_CHUNK = 128
GLA_LEVELS = 7


def _cparams(sem):
    return pltpu.CompilerParams(dimension_semantics=sem, vmem_limit_bytes=VMEM_LIMIT)


def _dot(a, b):
    return jnp.dot(a, b, preferred_element_type=F32)


def _dot_nt(a, b):
    return lax.dot_general(a, b, (((1,), (1,)), ((), ())), preferred_element_type=F32)


def _dot_tn(a, b):
    return lax.dot_general(a, b, (((0,), (0,)), ((), ())), preferred_element_type=F32)


def _split2(x):
    hi = x.astype(BF16)
    lo = (x - hi.astype(F32)).astype(BF16)
    return hi, lo


def _split3(x):
    hi = x.astype(BF16)
    r = x - hi.astype(F32)
    mid = r.astype(BF16)
    lo = (r - mid.astype(F32)).astype(BF16)
    return hi, mid, lo


def _rms(x, gain):
    ms = jnp.mean(x * x, axis=-1, keepdims=True)
    return x * lax.rsqrt(ms + NORM_EPS) * gain


def _sigmoid(x):
    return 1.0 / (1.0 + jnp.exp(-x))


def _full_spec(shape):
    nd = len(shape)
    return pl.BlockSpec(shape, lambda *_: (0,) * nd)


def _head_ms(x, bd):
    hi, lo = _split2(x * x)
    return (_dot(hi, bd) + _dot(lo, bd)) * (1.0 / HEAD_DIM)


def _rope(x, cos, sin):
    w = x.shape[-1]
    lane = lax.broadcasted_iota(I32, x.shape, 1)
    fwd = pltpu.roll(x, w - 16, axis=1)
    bwd = pltpu.roll(x, 16, axis=1)
    partner = jnp.where((lane % 32) < 16, fwd, bwd)
    return x * cos + partner * sin


def _in_proj_kernel(x_ref, cos_ref, sin_ref, gmix_ref, wqkv_ref, wgla_ref, wlr_ref, wga_ref, wgl_ref,
                    wgate_ref, bgate_ref, qn_ref, kn_ref, bdq_ref, bdk_ref,
                    q_ref, k_ref, v_ref, lq_ref, lk_ref, lv_ref, lr_ref, gf_ref, gb_ref, sga_ref, sgl_ref):
    x = x_ref[...]
    h = _rms(x, gmix_ref[...]).astype(BF16)

    qkv = _dot(h, wqkv_ref[...])
    cos = cos_ref[...]
    sin = sin_ref[...]
    q = qkv[:, :ATTN_W]
    q = q * lax.rsqrt(_head_ms(q, bdq_ref[...]) + NORM_EPS) * qn_ref[...]
    q = _rope(q, cos, sin) * (HEAD_DIM ** -0.5)
    q_ref[...] = q.astype(BF16)
    k = qkv[:, ATTN_W:ATTN_W + KV_W]
    k = k * lax.rsqrt(_head_ms(k, bdk_ref[...]) + NORM_EPS) * kn_ref[...]
    k = _rope(k, cos[:, :KV_W], sin[:, :KV_W])
    k_ref[...] = k.astype(BF16)
    v_ref[...] = qkv[:, ATTN_W + KV_W:].astype(BF16)

    gla = _dot(h, wgla_ref[...])
    lq_ref[...] = gla[:, :GLA_KW] * (GLA_DK ** -0.5)
    lk_ref[...] = gla[:, GLA_KW:2 * GLA_KW]
    lv_ref[...] = gla[:, 2 * GLA_KW:2 * GLA_KW + GLA_VW]
    lr_ref[...] = gla[:, 2 * GLA_KW + GLA_VW:]

    a = _dot(h, wlr_ref[...]).astype(BF16)
    z = _dot(a, wgate_ref[...]) + bgate_ref[...]
    g = (jnp.minimum(z, 0.0) - jnp.log(1.0 + jnp.exp(-jnp.abs(z)))) * (1.0 / GLA_GATE_NORM)
    gf_ref[...] = g[:, :GLA_KW]
    gb_ref[...] = g[:, GLA_KW:]

    sga_ref[...] = _sigmoid(_dot(h, wga_ref[...]))
    sgl_ref[...] = _sigmoid(_dot(h, wgl_ref[...]))


def _in_proj(x2d, T, tabs, w, tm=256):
    n, d = x2d.shape
    nt = T // tm
    row = lambda wd: pl.BlockSpec((tm, wd), lambda i: (i, 0))
    tab = pl.BlockSpec((tm, ATTN_W), lambda i: (i % nt, 0))
    consts = [w["gmix"], w["wqkv"], w["wgla"], w["wlr"], w["wga"], w["wgl"], w["wgate"], w["bgate"],
              w["qn"], w["kn"], w["bdq"], w["bdk"]]
    outs = [(ATTN_W, BF16), (KV_W, BF16), (KV_W, BF16), (GLA_KW, F32), (GLA_KW, F32), (GLA_VW, F32),
            (GLA_VW, F32), (GLA_KW, F32), (GLA_KW, F32), (d, F32), (d, F32)]
    return pl.pallas_call(
        _in_proj_kernel,
        grid=(n // tm,),
        in_specs=[row(d), tab, tab] + [_full_spec(c.shape) for c in consts],
        out_specs=[row(wd) for wd, _ in outs],
        out_shape=[jax.ShapeDtypeStruct((n, wd), dt) for wd, dt in outs],
        compiler_params=_cparams(("parallel",)),
        name="in_proj",
    )(x2d, tabs[0], tabs[1], *consts)


def _attn_kernel(q_ref, k_ref, v_ref, o_ref, m_ref, l_ref, acc_ref):
    j = pl.program_id(2)

    @pl.when(j == 0)
    def _():
        m_ref[...] = jnp.full(m_ref.shape, -jnp.inf, F32)
        l_ref[...] = jnp.zeros(l_ref.shape, F32)
        acc_ref[...] = jnp.zeros(acc_ref.shape, F32)

    k = k_ref[0]
    v = v_ref[0]
    group = N_Q_HEADS // N_KV_HEADS
    for hq in range(N_Q_HEADS):
        hk = hq // group
        qh = q_ref[0, :, hq * HEAD_DIM:(hq + 1) * HEAD_DIM]
        kh = k[:, hk * HEAD_DIM:(hk + 1) * HEAD_DIM]
        s = _dot_nt(qh, kh)
        m_old = m_ref[hq]
        m_new = jnp.maximum(m_old, jnp.max(s, axis=-1, keepdims=True))
        alpha = jnp.exp(m_old - m_new)
        p = jnp.exp(s - m_new)
        l_ref[hq] = alpha * l_ref[hq] + jnp.sum(p, axis=-1, keepdims=True)
        m_ref[hq] = m_new
        pv = _dot(p.astype(BF16), v)[:, hk * HEAD_DIM:(hk + 1) * HEAD_DIM]
        acc_ref[hq] = alpha * acc_ref[hq] + pv

    @pl.when(j == pl.num_programs(2) - 1)
    def _():
        for hq in range(N_Q_HEADS):
            o_ref[0, :, hq * HEAD_DIM:(hq + 1) * HEAD_DIM] = (acc_ref[hq] / l_ref[hq]).astype(BF16)


def _attention(q, k, v, tq=256, tk=512):
    b, t, _ = q.shape
    return pl.pallas_call(
        _attn_kernel,
        grid=(b, t // tq, t // tk),
        in_specs=[pl.BlockSpec((1, tq, ATTN_W), lambda bi, i, j: (bi, i, 0)),
                  pl.BlockSpec((1, tk, KV_W), lambda bi, i, j: (bi, j, 0)),
                  pl.BlockSpec((1, tk, KV_W), lambda bi, i, j: (bi, j, 0))],
        out_specs=pl.BlockSpec((1, tq, ATTN_W), lambda bi, i, j: (bi, i, 0)),
        out_shape=jax.ShapeDtypeStruct((b, t, ATTN_W), BF16),
        scratch_shapes=[pltpu.VMEM((N_Q_HEADS, tq, 1), F32), pltpu.VMEM((N_Q_HEADS, tq, 1), F32),
                        pltpu.VMEM((N_Q_HEADS, tq, HEAD_DIM), F32)],
        compiler_params=_cparams(("parallel", "parallel", "arbitrary")),
        name="attention",
    )(q, k, v)


def _gla_chunk(q, k, v, g, st_ref, reverse):
    c = GLA_CHUNK
    pair_w = 2 * GLA_DK
    t_kw = lax.broadcasted_iota(I32, (c, GLA_KW), 0)
    t_row = lax.broadcasted_iota(I32, (c, c), 0)
    s_col = lax.broadcasted_iota(I32, (c, c), 1)
    lane_pair = lax.broadcasted_iota(I32, (c, pair_w), 1)
    lo_lanes = lane_pair < GLA_DK

    a_heads = [jnp.zeros((c, c), F32) for _ in range(GLA_HEADS)]

    def add_level(a_heads, qh, kh, mask):
        qb = qh.astype(BF16)
        out = []
        for h in range(GLA_HEADS):
            p = h // 2
            qp = qb[:, p * pair_w:(p + 1) * pair_w]
            kp = kh[:, p * pair_w:(p + 1) * pair_w]
            kp = jnp.where(lo_lanes if h % 2 == 0 else jnp.logical_not(lo_lanes), kp, 0.0).astype(BF16)
            out.append(a_heads[h] + jnp.where(mask, _dot_nt(qp, kp), 0.0))
        return out

    pre = g
    tot = g
    for lvl in range(GLA_LEVELS):
        s = 1 << lvl
        odd = ((t_kw >> lvl) & 1) == 1
        if not reverse:
            e = jnp.exp(jnp.where(odd, pre, tot - pre))
            qh = jnp.where(odd, q * e, 0.0)
            kh = jnp.where(odd, 0.0, k * e)
        else:
            e = jnp.exp(jnp.where(odd, pre - g, tot - pre + g))
            qh = jnp.where(odd, 0.0, q * e)
            kh = jnp.where(odd, k * e, 0.0)
        same_parent = (t_row >> (lvl + 1)) == (s_col >> (lvl + 1))
        a_heads = add_level(a_heads, qh, kh, same_parent)
        sib = jnp.where(odd, pltpu.roll(tot, s, axis=0), pltpu.roll(tot, c - s, axis=0))
        pre = pre + jnp.where(odd, sib, 0.0)
        tot = tot + sib
    if not reverse:
        a_heads = add_level(a_heads, q, k, t_row == s_col)

    if not reverse:
        q_dec = q * jnp.exp(pre)
        k_dec = k * jnp.exp(tot - pre)
    else:
        q_dec = q * jnp.exp(tot - pre + g)
        k_dec = k * jnp.exp(pre - g)
    chunk_decay = jnp.exp(tot[0:1, :])
    sub_pair = lax.broadcasted_iota(I32, (2 * GLA_DV, pair_w), 0)
    lane_st = lax.broadcasted_iota(I32, (2 * GLA_DV, pair_w), 1)
    block_diag = (sub_pair // GLA_DV) == (lane_st // GLA_DK)

    outs = []
    for p in range(2):
        st = st_ref[p]
        qd = q_dec[:, p * pair_w:(p + 1) * pair_w].astype(BF16)
        o_pair = _dot_nt(qd, st.astype(BF16))
        vp = v[:, p * 2 * GLA_DV:(p + 1) * 2 * GLA_DV].astype(BF16)
        kd = k_dec[:, p * pair_w:(p + 1) * pair_w].astype(BF16)
        upd = _dot_tn(vp, kd)
        st_ref[p] = st * chunk_decay[:, p * pair_w:(p + 1) * pair_w] + jnp.where(block_diag, upd, 0.0)
        for hh in range(2):
            h = 2 * p + hh
            vh = vp[:, hh * GLA_DV:(hh + 1) * GLA_DV]
            outs.append(o_pair[:, hh * GLA_DV:(hh + 1) * GLA_DV] + _dot(a_heads[h].astype(BF16), vh))
    return jnp.concatenate(outs, axis=1)


def _gla_bwd_kernel(q_ref, k_ref, v_ref, g_ref, o_ref, st_ref):
    @pl.when(pl.program_id(1) == 0)
    def _():
        st_ref[...] = jnp.zeros(st_ref.shape, F32)

    o_ref[0] = _gla_chunk(q_ref[0], k_ref[0], v_ref[0], g_ref[0], st_ref, True)


def _gla_fwd_kernel(q_ref, k_ref, v_ref, g_ref, ob_ref, r_ref, gn_ref, o_ref, st_ref):
    @pl.when(pl.program_id(1) == 0)
    def _():
        st_ref[...] = jnp.zeros(st_ref.shape, F32)

    o = _gla_chunk(q_ref[0], k_ref[0], v_ref[0], g_ref[0], st_ref, False) + ob_ref[0]
    r = r_ref[0]
    gate = r * _sigmoid(r)
    gn = gn_ref[...]
    outs = []
    for h in range(GLA_HEADS):
        oh = o[:, h * GLA_DV:(h + 1) * GLA_DV]
        outs.append(_rms(oh, gn))
    o_ref[0] = (jnp.concatenate(outs, axis=1) * gate).astype(BF16)


def _gla(lq, lk, lv, lr, gf, gb, gla_norm):
    b, t, _ = lq.shape
    c = GLA_CHUNK
    nc = t // c
    fwd = lambda wd: pl.BlockSpec((1, c, wd), lambda bi, ci: (bi, ci, 0))
    rev = lambda wd: pl.BlockSpec((1, c, wd), lambda bi, ci: (bi, nc - 1 - ci, 0))
    st = pltpu.VMEM((2, 2 * GLA_DV, 2 * GLA_DK), F32)
    o_b = pl.pallas_call(
        _gla_bwd_kernel,
        grid=(b, nc),
        in_specs=[rev(GLA_KW), rev(GLA_KW), rev(GLA_VW), rev(GLA_KW)],
        out_specs=rev(GLA_VW),
        out_shape=jax.ShapeDtypeStruct((b, t, GLA_VW), F32),
        scratch_shapes=[st],
        compiler_params=_cparams(("parallel", "arbitrary")),
        name="gla_bwd",
    )(lq, lk, lv, gb)
    return pl.pallas_call(
        _gla_fwd_kernel,
        grid=(b, nc),
        in_specs=[fwd(GLA_KW), fwd(GLA_KW), fwd(GLA_VW), fwd(GLA_KW), fwd(GLA_VW), fwd(GLA_VW),
                  _full_spec(gla_norm.shape)],
        out_specs=fwd(GLA_VW),
        out_shape=jax.ShapeDtypeStruct((b, t, GLA_VW), BF16),
        scratch_shapes=[st],
        compiler_params=_cparams(("parallel", "arbitrary")),
        name="gla_fwd",
    )(lq, lk, lv, gf, o_b, lr, gla_norm)


def _post_mix_kernel(x_ref, oa_ref, og_ref, sga_ref, sgl_ref, wau_ref, wgu_ref, wout_ref, gmoe_ref,
                     wr_hi_ref, wr_lo_ref, x1_ref, hm_ref, aff_ref):
    ua = _dot(oa_ref[...], wau_ref[...])
    ul = _dot(og_ref[...], wgu_ref[...])
    merged = sga_ref[...] * ua + sgl_ref[...] * ul
    x1 = x_ref[...] + _dot(merged.astype(BF16), wout_ref[...])
    x1_ref[...] = x1
    hm = _rms(x1, gmoe_ref[...])
    hm_ref[...] = hm
    h_hi, h_lo = _split2(hm)
    wr_hi = wr_hi_ref[...]
    logits = _dot_nt(wr_hi, h_hi) + _dot_nt(wr_hi, h_lo) + _dot_nt(wr_lo_ref[...], h_hi)
    m = jnp.max(logits, axis=0, keepdims=True)
    e = jnp.exp(logits - m)
    aff_ref[...] = e / jnp.sum(e, axis=0, keepdims=True)


def _post_mix(x2d, oa, og, sga, sgl, w, tm=256):
    n, d = x2d.shape
    row = lambda wd: pl.BlockSpec((tm, wd), lambda i: (i, 0))
    consts = [w["wau"], w["wgu"], w["wout"], w["gmoe"], w["wr_hi"], w["wr_lo"]]
    return pl.pallas_call(
        _post_mix_kernel,
        grid=(n // tm,),
        in_specs=[row(d), row(ATTN_W), row(GLA_VW), row(d), row(d)] + [_full_spec(c.shape) for c in consts],
        out_specs=[row(d), row(d), pl.BlockSpec((N_EXPERTS, tm), lambda i: (0, i))],
        out_shape=[jax.ShapeDtypeStruct((n, d), F32), jax.ShapeDtypeStruct((n, d), F32),
                   jax.ShapeDtypeStruct((N_EXPERTS, n), F32)],
        compiler_params=_cparams(("parallel",)),
        name="post_mix",
    )(x2d, oa, og, sga, sgl, *consts)


def _route_kernel(a_ref, idx_ref, gate_ref, *, cap):
    a = a_ref[0]
    rows, ncol = a.shape
    bits = lax.bitcast_convert_type(a, I32)
    t_io = lax.broadcasted_iota(I32, (rows, ncol), 0)
    b_io = lax.broadcasted_iota(I32, (rows, ncol), 1)
    tok = t_io * ncol + b_io

    def thr_step(i, thr):
        cand = thr | lax.shift_left(jnp.int32(1), jnp.int32(30) - i)
        cnt = jnp.sum((bits >= cand).astype(I32))
        return jnp.where(cnt >= cap, cand, thr)

    thr = lax.fori_loop(0, 31, thr_step, jnp.int32(0))
    gt = bits > thr
    eq = bits == thr
    need = cap - jnp.sum(gt.astype(I32))

    def tie_step(i, lim):
        cand = lim + lax.shift_left(jnp.int32(1), jnp.int32(17) - i)
        cnt = jnp.sum((eq & (tok < cand)).astype(I32))
        return jnp.where(cnt < need, cand, lim)

    lim = lax.fori_loop(0, 18, tie_step, jnp.int32(0))
    sel = (gt | (eq & (tok <= lim))).astype(F32)

    r_io = lax.broadcasted_iota(I32, (rows, rows), 0)
    c_io = lax.broadcasted_iota(I32, (rows, rows), 1)
    ltri = (c_io <= r_io).astype(BF16)
    selb = sel.astype(BF16)
    cuml = _dot(ltri, selb)
    cumlb = cuml.astype(BF16)
    br_io = lax.broadcasted_iota(I32, (ncol, ncol), 0)
    bc_io = lax.broadcasted_iota(I32, (ncol, ncol), 1)
    lcol = (bc_io <= br_io).astype(BF16)
    col_incl = jnp.sum(_dot_nt(lcol, selb), axis=1, keepdims=True)
    col_cnt = jnp.sum(_dot_nt((bc_io == br_io).astype(BF16), selb), axis=1, keepdims=True)
    col_excl = col_incl - col_cnt
    a_hi, a_mid, a_lo = _split3(a)
    bcol = lax.broadcasted_iota(I32, (ncol, LANES), 0).astype(F32)
    trow = lax.broadcasted_iota(I32, (rows, LANES), 0).astype(F32)
    lane = lax.broadcasted_iota(I32, (1, LANES), 1).astype(F32)

    def slot_tile(jt, carry):
        j = lane + jnp.asarray(jt, F32) * float(LANES)
        bj = jnp.sum((col_incl <= j).astype(F32), axis=0, keepdims=True)
        onehot = (bcol == bj).astype(F32)
        base = jnp.sum(onehot * col_excl, axis=0, keepdims=True)
        jl = j - base
        ohb = onehot.astype(BF16)
        rt = _dot(cumlb, ohb)
        tj = jnp.sum((rt <= jl).astype(F32), axis=0, keepdims=True)
        acol = _dot(a_hi, ohb) + _dot(a_mid, ohb) + _dot(a_lo, ohb)
        gate = jnp.sum(jnp.where(trow == tj, acol, 0.0), axis=0, keepdims=True)
        idx_ref[0, pl.ds(jt, 1), :] = (tj * ncol + bj).astype(I32)
        gate_ref[0, pl.ds(jt, 1), :] = gate
        return carry

    lax.fori_loop(0, cap // LANES, slot_tile, 0)


def _route(aff_t, cap):
    e, n = aff_t.shape
    ncol = n // LANES
    a3 = aff_t.reshape(e, LANES, ncol)
    nt = cap // LANES
    return pl.pallas_call(
        functools.partial(_route_kernel, cap=cap),
        grid=(e,),
        in_specs=[pl.BlockSpec((1, LANES, ncol), lambda i: (i, 0, 0))],
        out_specs=[pl.BlockSpec((1, nt, LANES), lambda i: (i, 0, 0)),
                   pl.BlockSpec((1, nt, LANES), lambda i: (i, 0, 0))],
        out_shape=[jax.ShapeDtypeStruct((e, nt, LANES), I32), jax.ShapeDtypeStruct((e, nt, LANES), F32)],
        compiler_params=_cparams(("parallel",)),
        name="route",
    )(a3)


def _moe_kernel(idx_ref, gate_ref, hm_hbm, xin_hbm, wg_ref, wu_ref, wd_ref, xacc_hbm,
                hbuf, xbuf, sem_in, sem_out, *, rows):
    del xin_hbm

    def in_copies(r):
        n = idx_ref[0, 0, r]
        return (pltpu.make_async_copy(hm_hbm.at[pl.ds(n, 1)], hbuf.at[pl.ds(r, 1)], sem_in.at[0]),
                pltpu.make_async_copy(xacc_hbm.at[pl.ds(n, 1)], xbuf.at[pl.ds(r, 1)], sem_in.at[1]))

    def out_copy(r):
        n = idx_ref[0, 0, r]
        return pltpu.make_async_copy(xbuf.at[pl.ds(r, 1)], xacc_hbm.at[pl.ds(n, 1)], sem_out.at[0])

    def start_in(r, c):
        h_cp, x_cp = in_copies(r)
        h_cp.start()
        x_cp.start()
        return c

    def wait_in(r, c):
        h_cp, x_cp = in_copies(r)
        h_cp.wait()
        x_cp.wait()
        return c

    lax.fori_loop(0, rows, start_in, 0)
    lax.fori_loop(0, rows, wait_in, 0)

    xe = hbuf[...].astype(BF16)
    hid_g = _dot(xe, wg_ref[0])
    hid = (hid_g * _sigmoid(hid_g)) * _dot(xe, wu_ref[0])
    y = _dot(hid.astype(BF16), wd_ref[0])
    eye = (lax.broadcasted_iota(I32, (LANES, LANES), 0) == lax.broadcasted_iota(I32, (LANES, LANES), 1))
    for c in range(rows // LANES):
        g_row = gate_ref[0, :, c * LANES:(c + 1) * LANES]
        g_col = jnp.sum(jnp.where(eye, g_row, 0.0), axis=1, keepdims=True)
        sl = slice(c * LANES, (c + 1) * LANES)
        xbuf[sl, :] = xbuf[sl, :] + y[sl, :] * g_col

    def start_out(r, c):
        out_copy(r).start()
        return c

    def wait_out(r, c):
        out_copy(r).wait()
        return c

    lax.fori_loop(0, rows, start_out, 0)
    lax.fori_loop(0, rows, wait_out, 0)


def _moe_ffn(idx, gate, hm, x1, wg, wu, wd, rows=256):
    e, nt, _ = idx.shape
    cap = nt * LANES
    n, d = x1.shape
    nblk = cap // rows
    idx3 = idx.reshape(e * nblk, 1, rows)
    gate3 = gate.reshape(e * nblk, 1, rows)
    ff = wg.shape[-1]
    return pl.pallas_call(
        functools.partial(_moe_kernel, rows=rows),
        grid=(e, nblk),
        in_specs=[pl.BlockSpec((1, 1, rows), lambda ei, j: (ei * nblk + j, 0, 0), memory_space=pltpu.SMEM),
                  pl.BlockSpec((1, 1, rows), lambda ei, j: (ei * nblk + j, 0, 0)),
                  pl.BlockSpec(memory_space=pl.ANY),
                  pl.BlockSpec(memory_space=pl.ANY),
                  pl.BlockSpec((1, d, ff), lambda ei, j: (ei, 0, 0)),
                  pl.BlockSpec((1, d, ff), lambda ei, j: (ei, 0, 0)),
                  pl.BlockSpec((1, ff, d), lambda ei, j: (ei, 0, 0))],
        out_specs=pl.BlockSpec(memory_space=pl.ANY),
        out_shape=jax.ShapeDtypeStruct((n, d), F32),
        scratch_shapes=[pltpu.VMEM((rows, d), F32), pltpu.VMEM((rows, d), F32),
                        pltpu.SemaphoreType.DMA((2,)), pltpu.SemaphoreType.DMA((1,))],
        input_output_aliases={3: 0},
        compiler_params=_cparams(("arbitrary", "arbitrary")),
        name="moe_ffn",
    )(idx3, gate3, hm, x1, wg, wu, wd)


def _ple_final_kernel(x_ref, p_ref, gple_ref, wpg_ref, wpp_ref, gfin_ref, o_ref):
    x = x_ref[...]
    hp = _rms(x, gple_ref[...]).astype(BF16)
    gate = _sigmoid(_dot(hp, wpg_ref[...]))
    y = x + gate * _dot(p_ref[...].astype(BF16), wpp_ref[...])
    o_ref[...] = _rms(y, gfin_ref[...])


def _ple_final(x2, p2d, w, tm=256):
    n, d = x2.shape
    pd = p2d.shape[-1]
    row = lambda wd: pl.BlockSpec((tm, wd), lambda i: (i, 0))
    consts = [w["gple"], w["wpg"], w["wpp"], w["gfin"]]
    return pl.pallas_call(
        _ple_final_kernel,
        grid=(n // tm,),
        in_specs=[row(d), row(pd)] + [_full_spec(c.shape) for c in consts],
        out_specs=row(d),
        out_shape=jax.ShapeDtypeStruct((n, d), F32),
        compiler_params=_cparams(("parallel",)),
        name="ple_final",
    )(x2, p2d, *consts)


def _rope_tables(T):
    half = HEAD_DIM // 2
    inv = ROPE_THETA ** (-jnp.arange(0, half, 2, dtype=F32) / half)
    t = jnp.arange(T)
    ang_r = (t // GRID_W).astype(F32)[:, None] * inv
    ang_c = (t % GRID_W).astype(F32)[:, None] * inv
    cos = jnp.concatenate([jnp.cos(ang_r)] * 2 + [jnp.cos(ang_c)] * 2, axis=1)
    sin = jnp.concatenate([-jnp.sin(ang_r), jnp.sin(ang_r), -jnp.sin(ang_c), jnp.sin(ang_c)], axis=1)
    return jnp.tile(cos, (1, N_Q_HEADS)), jnp.tile(sin, (1, N_Q_HEADS))


def _block_diag_ones(width):
    i = np.arange(width)
    return jnp.asarray((i[:, None] // HEAD_DIM) == (i[None, :] // HEAD_DIM), BF16)


def _prep_weights(g_mix, w_in, q_norm, k_norm, w_af, b_af, w_ab, b_ab, gla_norm, w_attn_up, w_gla_up,
                  w_out, g_moe, w_router, w_gate_e, w_up_e, w_down_e, g_ple, w_ple_gate, w_ple_proj,
                  g_final):
    d = w_in.shape[0]
    o_gla = ATTN_W + 2 * KV_W
    o_lr = o_gla + 2 * GLA_KW + 2 * GLA_VW
    o_ga = o_lr + 2 * GLA_LOWRANK
    wlr = jnp.zeros((d, LANES), F32).at[:, :2 * GLA_LOWRANK].set(w_in[:, o_lr:o_ga])
    wgate = jnp.zeros((LANES, 2 * GLA_KW), F32)
    wgate = wgate.at[:GLA_LOWRANK, :GLA_KW].set(w_af)
    wgate = wgate.at[GLA_LOWRANK:2 * GLA_LOWRANK, GLA_KW:].set(w_ab)
    wr_t = w_router.T
    wr_hi = wr_t.astype(BF16)
    wr_lo = (wr_t - wr_hi.astype(F32)).astype(BF16)
    return dict(
        gmix=g_mix.reshape(1, d),
        wqkv=w_in[:, :o_gla].astype(BF16),
        wgla=w_in[:, o_gla:o_lr].astype(BF16),
        wlr=wlr.astype(BF16),
        wga=w_in[:, o_ga:o_ga + d].astype(BF16),
        wgl=w_in[:, o_ga + d:].astype(BF16),
        wgate=wgate.astype(BF16),
        bgate=jnp.concatenate([b_af, b_ab]).reshape(1, 2 * GLA_KW),
        qn=jnp.tile(q_norm, N_Q_HEADS).reshape(1, ATTN_W),
        kn=jnp.tile(k_norm, N_KV_HEADS).reshape(1, KV_W),
        bdq=_block_diag_ones(ATTN_W),
        bdk=_block_diag_ones(KV_W),
        gnorm=gla_norm.reshape(1, GLA_DV),
        wau=w_attn_up.astype(BF16),
        wgu=w_gla_up.astype(BF16),
        wout=w_out.astype(BF16),
        gmoe=g_moe.reshape(1, d),
        wr_hi=wr_hi,
        wr_lo=wr_lo,
        wge=w_gate_e.astype(BF16),
        wue=w_up_e.astype(BF16),
        wde=w_down_e.astype(BF16),
        gple=g_ple.reshape(1, d),
        wpg=w_ple_gate.astype(BF16),
        wpp=w_ple_proj.astype(BF16),
        gfin=g_final.reshape(1, d),
    )


def _trunk(x, p, w):
    b, t, d = x.shape
    n = b * t
    x2d = x.reshape(n, d)
    q, k, v, lq, lk, lv, lr, gf, gb, sga, sgl = _in_proj(x2d, t, _rope_tables(t), w)
    r3 = lambda a: a.reshape(b, t, a.shape[-1])
    o_attn = _attention(r3(q), r3(k), r3(v)).reshape(n, ATTN_W)
    o_gla = _gla(r3(lq), r3(lk), r3(lv), r3(lr), r3(gf), r3(gb), w["gnorm"]).reshape(n, GLA_VW)
    x1, hm, aff_t = _post_mix(x2d, o_attn, o_gla, sga, sgl, w)
    cap = EC_CAPACITY * n // N_EXPERTS
    idx, gate = _route(aff_t, cap)
    x2 = _moe_ffn(idx, gate, hm, x1, w["wge"], w["wue"], w["wde"])
    y = _ple_final(x2, p.reshape(n, p.shape[-1]), w)
    return y.reshape(b, t, d)


def kernel(x_prompt, x_sample, p_prompt, p_sample, g_mix, w_in, q_norm, k_norm, w_af, b_af, w_ab, b_ab,
           gla_norm, w_attn_up, w_gla_up, w_out, g_moe, w_router, w_gate_e, w_up_e, w_down_e, g_ple,
           w_ple_gate, w_ple_proj, g_final):
    assert g_mix.shape[0] == 1, "single layer"
    w = _prep_weights(g_mix[0], w_in[0], q_norm[0], k_norm[0], w_af[0], b_af[0], w_ab[0], b_ab[0],
                      gla_norm[0], w_attn_up[0], w_gla_up[0], w_out[0], g_moe[0], w_router[0],
                      w_gate_e[0], w_up_e[0], w_down_e[0], g_ple[0], w_ple_gate[0], w_ple_proj[0], g_final)
    return (_trunk(x_prompt, p_prompt[0], w), _trunk(x_sample, p_sample[0], w))
```

```python
import functools

import numpy as np
import jax
import jax.numpy as jnp
from jax import lax
from jax.experimental import pallas as pl
from jax.experimental.pallas import tpu as pltpu

F32 = jnp.float32
BF16 = jnp.bfloat16
I32 = jnp.int32

HEAD_DIM = 64
N_Q_HEADS = 8
N_KV_HEADS = 2
GRID_W = 64
ROPE_THETA = 10000.0
GLA_HEADS = 4
GLA_DK = 64
GLA_DV = 128
GLA_LOWRANK = 16
GLA_GATE_NORM = 16.0
N_EXPERTS = 16
EC_CAPACITY = 2
NORM_EPS = 1e-6
LOG2E = 1.4426950408889634

ATTN_W = N_Q_HEADS * HEAD_DIM
KV_W = N_KV_HEADS * HEAD_DIM
GLA_KW = GLA_HEADS * GLA_DK
GLA_VW = GLA_HEADS * GLA_DV

LANES = 128
SUBLANES = 8
BF16_ROWS = 16
VMEM_LIMIT = 56 * 1024 * 1024

GLA_CHUNK = 128
GLA_LEVELS = 7


def _cparams(sem):
    return pltpu.CompilerParams(dimension_semantics=sem, vmem_limit_bytes=VMEM_LIMIT)


def _dot(a, b):
    return jnp.dot(a, b, preferred_element_type=F32)


def _dot_nt(a, b):
    return lax.dot_general(a, b, (((1,), (1,)), ((), ())), preferred_element_type=F32)


def _dot_tn(a, b):
    return lax.dot_general(a, b, (((0,), (0,)), ((), ())), preferred_element_type=F32)


def _split2(x):
    hi = x.astype(BF16)
    lo = (x - hi.astype(F32)).astype(BF16)
    return hi, lo


def _split3(x):
    hi = x.astype(BF16)
    r = x - hi.astype(F32)
    mid = r.astype(BF16)
    lo = (r - mid.astype(F32)).astype(BF16)
    return hi, mid, lo


def _rms(x, gain):
    ms = jnp.mean(x * x, axis=-1, keepdims=True)
    return x * lax.rsqrt(ms + NORM_EPS) * gain


def _sigmoid(x):
    return 1.0 / (1.0 + jnp.exp(-x))


def _full_spec(shape):
    nd = len(shape)
    return pl.BlockSpec(shape, lambda *_: (0,) * nd)


def _head_ms(x, bd):
    hi, lo = _split2(x * x)
    return (_dot(hi, bd) + _dot(lo, bd)) * (1.0 / HEAD_DIM)


def _rope(x, cos, sin):
    w = x.shape[-1]
    lane = lax.broadcasted_iota(I32, x.shape, 1)
    fwd = pltpu.roll(x, w - 16, axis=1)
    bwd = pltpu.roll(x, 16, axis=1)
    partner = jnp.where((lane % 32) < 16, fwd, bwd)
    return x * cos + partner * sin


def _in_proj_kernel(x_ref, cos_ref, sin_ref, gmix_ref, wqk_ref, wvt_ref, wgla_ref, wlr_ref, wga_ref, wgl_ref,
                    wgate_ref, bgate_ref, qn_ref, kn_ref, bdq_ref, bdk_ref,
                    q_ref, k_ref, vt_ref, lq_ref, lk_ref, lv_ref, lr_ref, gf_ref, gb_ref, sga_ref, sgl_ref):
    x = x_ref[...]
    h = _rms(x, gmix_ref[...]).astype(BF16)

    qk = _dot(h, wqk_ref[...])
    cos = cos_ref[...]
    sin = sin_ref[...]
    q = qk[:, :ATTN_W]
    q = q * lax.rsqrt(_head_ms(q, bdq_ref[...]) + NORM_EPS) * qn_ref[...]
    q = _rope(q, cos, sin) * (HEAD_DIM ** -0.5 * LOG2E)
    q_ref[...] = q.astype(BF16)
    k = qk[:, ATTN_W:]
    k = k * lax.rsqrt(_head_ms(k, bdk_ref[...]) + NORM_EPS) * kn_ref[...]
    k = _rope(k, cos[:, :KV_W], sin[:, :KV_W])
    k_ref[...] = k.astype(BF16)
    vt_ref[0] = _dot_nt(wvt_ref[...], h).astype(BF16)

    gla = _dot(h, wgla_ref[...])
    lq_ref[...] = gla[:, :GLA_KW] * (GLA_DK ** -0.5)
    lk_ref[...] = gla[:, GLA_KW:2 * GLA_KW]
    lv_ref[...] = gla[:, 2 * GLA_KW:2 * GLA_KW + GLA_VW]
    lr_ref[...] = gla[:, 2 * GLA_KW + GLA_VW:]

    a = _dot(h, wlr_ref[...]).astype(BF16)
    z = _dot(a, wgate_ref[...]) + bgate_ref[...]
    g = (jnp.minimum(z, 0.0) - jnp.log(1.0 + jnp.exp(-jnp.abs(z)))) * (1.0 / GLA_GATE_NORM)
    gf_ref[...] = g[:, :GLA_KW]
    gb_ref[...] = g[:, GLA_KW:]

    sga_ref[...] = _sigmoid(_dot(h, wga_ref[...]))
    sgl_ref[...] = _sigmoid(_dot(h, wgl_ref[...]))


def _in_proj(x2d, T, tabs, w, tm=256):
    n, d = x2d.shape
    nt = T // tm
    row = lambda wd: pl.BlockSpec((tm, wd), lambda i: (i, 0))
    tab = pl.BlockSpec((tm, ATTN_W), lambda i: (i % nt, 0))
    consts = [w["gmix"], w["wqk"], w["wvt"], w["wgla"], w["wlr"], w["wga"], w["wgl"], w["wgate"], w["bgate"],
              w["qn"], w["kn"], w["bdq"], w["bdk"]]
    outs = [(ATTN_W, BF16), (KV_W, BF16), None, (GLA_KW, F32), (GLA_KW, F32), (GLA_VW, F32),
            (GLA_VW, F32), (GLA_KW, F32), (GLA_KW, F32), (d, F32), (d, F32)]
    vt_spec = pl.BlockSpec((1, KV_W, tm), lambda i: (i // nt, 0, i % nt))
    vt_shape = jax.ShapeDtypeStruct((n // T, KV_W, T), BF16)
    return pl.pallas_call(
        _in_proj_kernel,
        grid=(n // tm,),
        in_specs=[row(d), tab, tab] + [_full_spec(c.shape) for c in consts],
        out_specs=[vt_spec if o is None else row(o[0]) for o in outs],
        out_shape=[vt_shape if o is None else jax.ShapeDtypeStruct((n, o[0]), o[1]) for o in outs],
        compiler_params=_cparams(("parallel",)),
        name="in_proj",
    )(x2d, tabs[0], tabs[1], *consts)


def _attn_kernel(q_ref, k_ref, vt_ref, o_ref, qpad_ref, m_ref, acc_ref):
    j = pl.program_id(2)
    group = N_Q_HEADS // N_KV_HEADS

    @pl.when(j == 0)
    def _():
        m_ref[...] = jnp.full(m_ref.shape, -jnp.inf, F32)
        acc_ref[...] = jnp.zeros(acc_ref.shape, F32)
        lane = lax.broadcasted_iota(I32, (q_ref.shape[1], LANES), 1)
        for hq in range(N_Q_HEADS):
            hk = hq // group
            chunk = q_ref[0, :, (hq // 2) * LANES:(hq // 2 + 1) * LANES].astype(F32)
            if hq % 2 != hk:
                chunk = pltpu.roll(chunk, HEAD_DIM, axis=1)
            keep = (lane // HEAD_DIM) == hk
            qpad_ref[hq] = jnp.where(keep, chunk, 0.0).astype(BF16)

    k = k_ref[0]
    scores = [_dot_nt(k, qpad_ref[hq]) for hq in range(N_Q_HEADS)]
    probs, alphas = [], []
    for hq in range(N_Q_HEADS):
        s = scores[hq]
        m_old = m_ref[hq]
        m_new = jnp.maximum(m_old, jnp.max(s, axis=0, keepdims=True))
        alpha = jnp.exp2(m_old - m_new)
        p = jnp.exp2(s - m_new)
        m_ref[hq] = m_new
        probs.append(p.astype(BF16))
        alphas.append(alpha)
    ones = jnp.ones((BF16_ROWS, k.shape[0]), BF16)
    for hq in range(N_Q_HEADS):
        hk = hq // group
        vt = vt_ref[0, hk * HEAD_DIM:(hk + 1) * HEAD_DIM, :]
        vt1 = jnp.concatenate([vt, ones], axis=0)
        acc_ref[hq] = alphas[hq] * acc_ref[hq] + _dot(vt1, probs[hq])

    @pl.when(j == pl.num_programs(2) - 1)
    def _():
        outs = [acc_ref[hq, :HEAD_DIM, :] / acc_ref[hq, HEAD_DIM:HEAD_DIM + 1, :] for hq in range(N_Q_HEADS)]
        o_ref[0] = jnp.concatenate(outs, axis=0).T.astype(BF16)


def _attention(q, k, vt, tq=512, tk=512):
    b, t, _ = q.shape
    return pl.pallas_call(
        _attn_kernel,
        grid=(b, t // tq, t // tk),
        in_specs=[pl.BlockSpec((1, tq, ATTN_W), lambda bi, i, j: (bi, i, 0)),
                  pl.BlockSpec((1, tk, KV_W), lambda bi, i, j: (bi, j, 0)),
                  pl.BlockSpec((1, KV_W, tk), lambda bi, i, j: (bi, 0, j))],
        out_specs=pl.BlockSpec((1, tq, ATTN_W), lambda bi, i, j: (bi, i, 0)),
        out_shape=jax.ShapeDtypeStruct((b, t, ATTN_W), BF16),
        scratch_shapes=[pltpu.VMEM((N_Q_HEADS, tq, LANES), BF16),
                        pltpu.VMEM((N_Q_HEADS, 1, tq), F32),
                        pltpu.VMEM((N_Q_HEADS, HEAD_DIM + BF16_ROWS, tq), F32)],
        compiler_params=_cparams(("parallel", "parallel", "arbitrary")),
        name="attention",
    )(q, k, vt)


def _gla_chunk(q, k, v, g, st_ref, reverse):
    c = GLA_CHUNK
    pair_w = 2 * GLA_DK
    t_kw = lax.broadcasted_iota(I32, (c, GLA_KW), 0)
    t_row = lax.broadcasted_iota(I32, (c, c), 0)
    s_col = lax.broadcasted_iota(I32, (c, c), 1)
    lane_pair = lax.broadcasted_iota(I32, (c, pair_w), 1)
    lo_lanes = lane_pair < GLA_DK

    a_heads = [jnp.zeros((c, c), F32) for _ in range(GLA_HEADS)]

    def add_level(a_heads, qh, kh, mask):
        qb = qh.astype(BF16)
        out = []
        for h in range(GLA_HEADS):
            p = h // 2
            qp = qb[:, p * pair_w:(p + 1) * pair_w]
            kp = kh[:, p * pair_w:(p + 1) * pair_w]
            kp = jnp.where(lo_lanes if h % 2 == 0 else jnp.logical_not(lo_lanes), kp, 0.0).astype(BF16)
            out.append(a_heads[h] + jnp.where(mask, _dot_nt(qp, kp), 0.0))
        return out

    pre = g
    tot = g
    for lvl in range(GLA_LEVELS):
        s = 1 << lvl
        odd = ((t_kw >> lvl) & 1) == 1
        if not reverse:
            e = jnp.exp(jnp.where(odd, pre, tot - pre))
            qh = jnp.where(odd, q * e, 0.0)
            kh = jnp.where(odd, 0.0, k * e)
        else:
            e = jnp.exp(jnp.where(odd, pre - g, tot - pre + g))
            qh = jnp.where(odd, 0.0, q * e)
            kh = jnp.where(odd, k * e, 0.0)
        same_parent = (t_row >> (lvl + 1)) == (s_col >> (lvl + 1))
        a_heads = add_level(a_heads, qh, kh, same_parent)
        sib = jnp.where(odd, pltpu.roll(tot, s, axis=0), pltpu.roll(tot, c - s, axis=0))
        pre = pre + jnp.where(odd, sib, 0.0)
        tot = tot + sib
    if not reverse:
        a_heads = add_level(a_heads, q, k, t_row == s_col)

    if not reverse:
        q_dec = q * jnp.exp(pre)
        k_dec = k * jnp.exp(tot - pre)
    else:
        q_dec = q * jnp.exp(tot - pre + g)
        k_dec = k * jnp.exp(pre - g)
    chunk_decay = jnp.exp(tot[0:1, :])
    sub_pair = lax.broadcasted_iota(I32, (2 * GLA_DV, pair_w), 0)
    lane_st = lax.broadcasted_iota(I32, (2 * GLA_DV, pair_w), 1)
    block_diag = (sub_pair // GLA_DV) == (lane_st // GLA_DK)

    outs = []
    for p in range(2):
        st = st_ref[p]
        qd = q_dec[:, p * pair_w:(p + 1) * pair_w].astype(BF16)
        o_pair = _dot_nt(qd, st.astype(BF16))
        vp = v[:, p * 2 * GLA_DV:(p + 1) * 2 * GLA_DV].astype(BF16)
        kd = k_dec[:, p * pair_w:(p + 1) * pair_w].astype(BF16)
        upd = _dot_tn(vp, kd)
        st_ref[p] = st * chunk_decay[:, p * pair_w:(p + 1) * pair_w] + jnp.where(block_diag, upd, 0.0)
        for hh in range(2):
            h = 2 * p + hh
            vh = vp[:, hh * GLA_DV:(hh + 1) * GLA_DV]
            outs.append(o_pair[:, hh * GLA_DV:(hh + 1) * GLA_DV] + _dot(a_heads[h].astype(BF16), vh))
    return jnp.concatenate(outs, axis=1)


def _gla_bwd_kernel(q_ref, k_ref, v_ref, g_ref, o_ref, st_ref):
    @pl.when(pl.program_id(1) == 0)
    def _():
        st_ref[...] = jnp.zeros(st_ref.shape, F32)

    o_ref[0] = _gla_chunk(q_ref[0], k_ref[0], v_ref[0], g_ref[0], st_ref, True)


def _gla_fwd_kernel(q_ref, k_ref, v_ref, g_ref, ob_ref, r_ref, gn_ref, o_ref, st_ref):
    @pl.when(pl.program_id(1) == 0)
    def _():
        st_ref[...] = jnp.zeros(st_ref.shape, F32)

    o = _gla_chunk(q_ref[0], k_ref[0], v_ref[0], g_ref[0], st_ref, False) + ob_ref[0]
    r = r_ref[0]
    gate = r * _sigmoid(r)
    gn = gn_ref[...]
    outs = []
    for h in range(GLA_HEADS):
        oh = o[:, h * GLA_DV:(h + 1) * GLA_DV]
        outs.append(_rms(oh, gn))
    o_ref[0] = (jnp.concatenate(outs, axis=1) * gate).astype(BF16)


def _gla(lq, lk, lv, lr, gf, gb, gla_norm):
    b, t, _ = lq.shape
    c = GLA_CHUNK
    nc = t // c
    fwd = lambda wd: pl.BlockSpec((1, c, wd), lambda bi, ci: (bi, ci, 0))
    rev = lambda wd: pl.BlockSpec((1, c, wd), lambda bi, ci: (bi, nc - 1 - ci, 0))
    st = pltpu.VMEM((2, 2 * GLA_DV, 2 * GLA_DK), F32)
    o_b = pl.pallas_call(
        _gla_bwd_kernel,
        grid=(b, nc),
        in_specs=[rev(GLA_KW), rev(GLA_KW), rev(GLA_VW), rev(GLA_KW)],
        out_specs=rev(GLA_VW),
        out_shape=jax.ShapeDtypeStruct((b, t, GLA_VW), F32),
        scratch_shapes=[st],
        compiler_params=_cparams(("parallel", "arbitrary")),
        name="gla_bwd",
    )(lq, lk, lv, gb)
    return pl.pallas_call(
        _gla_fwd_kernel,
        grid=(b, nc),
        in_specs=[fwd(GLA_KW), fwd(GLA_KW), fwd(GLA_VW), fwd(GLA_KW), fwd(GLA_VW), fwd(GLA_VW),
                  _full_spec(gla_norm.shape)],
        out_specs=fwd(GLA_VW),
        out_shape=jax.ShapeDtypeStruct((b, t, GLA_VW), BF16),
        scratch_shapes=[st],
        compiler_params=_cparams(("parallel", "arbitrary")),
        name="gla_fwd",
    )(lq, lk, lv, gf, o_b, lr, gla_norm)


def _post_mix_kernel(x_ref, oa_ref, og_ref, sga_ref, sgl_ref, wau_ref, wgu_ref, wout_ref, gmoe_ref,
                     wr_hi_ref, wr_lo_ref, x1_ref, hm_ref, aff_ref):
    ua = _dot(oa_ref[...], wau_ref[...])
    ul = _dot(og_ref[...], wgu_ref[...])
    merged = sga_ref[...] * ua + sgl_ref[...] * ul
    x1 = x_ref[...] + _dot(merged.astype(BF16), wout_ref[...])
    x1_ref[...] = x1
    hm = _rms(x1, gmoe_ref[...])
    hm_ref[...] = hm
    h_hi, h_lo = _split2(hm)
    wr_hi = wr_hi_ref[...]
    logits = _dot_nt(wr_hi, h_hi) + _dot_nt(wr_hi, h_lo) + _dot_nt(wr_lo_ref[...], h_hi)
    m = jnp.max(logits, axis=0, keepdims=True)
    e = jnp.exp(logits - m)
    aff_ref[...] = e / jnp.sum(e, axis=0, keepdims=True)


def _post_mix(x2d, oa, og, sga, sgl, w, tm=256):
    n, d = x2d.shape
    row = lambda wd: pl.BlockSpec((tm, wd), lambda i: (i, 0))
    consts = [w["wau"], w["wgu"], w["wout"], w["gmoe"], w["wr_hi"], w["wr_lo"]]
    return pl.pallas_call(
        _post_mix_kernel,
        grid=(n // tm,),
        in_specs=[row(d), row(ATTN_W), row(GLA_VW), row(d), row(d)] + [_full_spec(c.shape) for c in consts],
        out_specs=[row(d), row(d), pl.BlockSpec((N_EXPERTS, tm), lambda i: (0, i))],
        out_shape=[jax.ShapeDtypeStruct((n, d), F32), jax.ShapeDtypeStruct((n, d), F32),
                   jax.ShapeDtypeStruct((N_EXPERTS, n), F32)],
        compiler_params=_cparams(("parallel",)),
        name="post_mix",
    )(x2d, oa, og, sga, sgl, *consts)


def _route_kernel(a_ref, idx_ref, gate_ref, pos_ref, lo_ref, *, cap):
    a = a_ref[0]
    nrow = a.shape[0]
    bits = lax.bitcast_convert_type(a, I32)
    b_io = lax.broadcasted_iota(I32, (nrow, LANES), 0)
    t_io = lax.broadcasted_iota(I32, (nrow, LANES), 1)
    tok = b_io * LANES + t_io

    def thr_step(i, thr):
        cand = thr | lax.shift_left(jnp.int32(1), jnp.int32(30) - i)
        cnt = jnp.sum((bits >= cand).astype(I32))
        return jnp.where(cnt >= cap, cand, thr)

    thr = lax.fori_loop(0, 31, thr_step, jnp.int32(0))
    gt = bits > thr
    eq = bits == thr
    need = cap - jnp.sum(gt.astype(I32))

    def tie_step(i, lim):
        cand = lim + lax.shift_left(jnp.int32(1), jnp.int32(17) - i)
        cnt = jnp.sum((eq & (tok < cand)).astype(I32))
        return jnp.where(cnt < need, cand, lim)

    lim = lax.fori_loop(0, 18, tie_step, jnp.int32(0))
    selm = gt | (eq & (tok <= lim))
    selb = selm.astype(F32).astype(BF16)

    r_io = lax.broadcasted_iota(I32, (LANES, LANES), 0)
    c_io = lax.broadcasted_iota(I32, (LANES, LANES), 1)
    utri = (r_io <= c_io).astype(BF16)
    cum_row = _dot(selb, utri)
    row_cnt = cum_row[:, LANES - 1:LANES]
    br_io = lax.broadcasted_iota(I32, (nrow, nrow), 0)
    bc_io = lax.broadcasted_iota(I32, (nrow, nrow), 1)
    lrow = (bc_io <= br_io).astype(BF16)
    row_incl = _dot(lrow, jnp.broadcast_to(row_cnt, (nrow, LANES)).astype(BF16))
    row_excl = row_incl - row_cnt
    pos_ref[0] = jnp.where(selm, row_excl + cum_row - 1.0, -1.0).astype(I32)
    lo_ref[0] = row_excl.astype(I32)
    row_incl_c = row_incl[:, 0:1]
    row_excl_c = row_excl[:, 0:1]
    cum_b = cum_row.astype(BF16)
    a_hi, a_mid, a_lo = _split3(a)
    brow = lax.broadcasted_iota(I32, (nrow, LANES), 0).astype(F32)
    tsub = lax.broadcasted_iota(I32, (LANES, LANES), 0).astype(F32)
    lane = lax.broadcasted_iota(I32, (1, LANES), 1).astype(F32)

    def slot_tile(jt, carry):
        j = lane + jnp.asarray(jt, F32) * float(LANES)
        bj = jnp.sum((row_incl_c <= j).astype(F32), axis=0, keepdims=True)
        onehot = (brow == bj).astype(F32)
        base = jnp.sum(onehot * row_excl_c, axis=0, keepdims=True)
        jl = j - base
        ohb = onehot.astype(BF16)
        rt = _dot_tn(cum_b, ohb)
        tj = jnp.sum((rt <= jl).astype(F32), axis=0, keepdims=True)
        acol = _dot_tn(a_hi, ohb) + _dot_tn(a_mid, ohb) + _dot_tn(a_lo, ohb)
        gate = jnp.sum(jnp.where(tsub == tj, acol, 0.0), axis=0, keepdims=True)
        idx_ref[0, pl.ds(jt, 1), :] = (bj * float(LANES) + tj).astype(I32)
        gate_ref[0, pl.ds(jt, 1), :] = gate
        return carry

    lax.fori_loop(0, cap // LANES, slot_tile, 0)


def _route(aff_t, cap):
    e, n = aff_t.shape
    nrow = n // LANES
    a3 = aff_t.reshape(e, nrow, LANES)
    nt = cap // LANES
    slot_spec = pl.BlockSpec((1, nt, LANES), lambda i: (i, 0, 0))
    tok_spec = pl.BlockSpec((1, nrow, LANES), lambda i: (i, 0, 0))
    idx, gate, pos, lo = pl.pallas_call(
        functools.partial(_route_kernel, cap=cap),
        grid=(e,),
        in_specs=[tok_spec],
        out_specs=[slot_spec, slot_spec, tok_spec, tok_spec],
        out_shape=[jax.ShapeDtypeStruct((e, nt, LANES), I32), jax.ShapeDtypeStruct((e, nt, LANES), F32),
                   jax.ShapeDtypeStruct((e, nrow, LANES), I32), jax.ShapeDtypeStruct((e, nrow, LANES), I32)],
        compiler_params=_cparams(("parallel",)),
        name="route",
    )(a3)
    lo_tab = jnp.concatenate([lo[:, :, 0], jnp.full((e, 1), cap, I32)], axis=1)
    return idx, gate, pos.reshape(e, n), lo_tab


MOE_ISSUE_UNROLL = 8


def _moe_kernel(idx_ref, idxn_ref, gate_ref, hm_hbm, wg_ref, wu_ref, wd_ref, ye_ref, hbuf, sem, *, rows):
    nblk = pl.num_programs(1)
    s = pl.program_id(0) * nblk + pl.program_id(1)
    nsteps = pl.num_programs(0) * nblk
    slot = lax.rem(s, 2)

    def issue(ids_ref, buf):
        def body(r0, c):
            for u in range(MOE_ISSUE_UNROLL):
                r = r0 * MOE_ISSUE_UNROLL + u
                n = ids_ref[0, 0, r]
                pltpu.make_async_copy(hm_hbm.at[pl.ds(n, 1)], hbuf.at[buf, pl.ds(r, 1)], sem.at[buf]).start()
            return c
        lax.fori_loop(0, rows // MOE_ISSUE_UNROLL, body, 0)

    @pl.when(s == 0)
    def _():
        issue(idx_ref, 0)

    @pl.when(s + 1 < nsteps)
    def _():
        issue(idxn_ref, 1 - slot)

    pltpu.make_async_copy(hm_hbm.at[pl.ds(0, rows)], hbuf.at[slot], sem.at[slot]).wait()

    xe = hbuf[slot].astype(BF16)
    hid_g = _dot(xe, wg_ref[0])
    hid = (hid_g * _sigmoid(hid_g)) * _dot(xe, wu_ref[0])
    y = _dot(hid.astype(BF16), wd_ref[0])
    eye = (lax.broadcasted_iota(I32, (LANES, LANES), 0) == lax.broadcasted_iota(I32, (LANES, LANES), 1))
    for c in range(rows // LANES):
        g_row = gate_ref[0, :, c * LANES:(c + 1) * LANES]
        g_col = jnp.sum(jnp.where(eye, g_row, 0.0), axis=1, keepdims=True)
        sl = slice(c * LANES, (c + 1) * LANES)
        ye_ref[sl, :] = (y[sl, :] * g_col).astype(BF16)


def _moe_ffn(idx, gate, hm, wg, wu, wd, rows=256):
    e, nt, _ = idx.shape
    cap = nt * LANES
    n, d = hm.shape
    nblk = cap // rows
    nsteps = e * nblk
    idx3 = idx.reshape(nsteps, 1, rows)
    gate3 = gate.reshape(nsteps, 1, rows)
    ff = wg.shape[-1]
    return pl.pallas_call(
        functools.partial(_moe_kernel, rows=rows),
        grid=(e, nblk),
        in_specs=[pl.BlockSpec((1, 1, rows), lambda ei, j: (ei * nblk + j, 0, 0), memory_space=pltpu.SMEM),
                  pl.BlockSpec((1, 1, rows), lambda ei, j: (jnp.minimum(ei * nblk + j + 1, nsteps - 1), 0, 0),
                               memory_space=pltpu.SMEM),
                  pl.BlockSpec((1, 1, rows), lambda ei, j: (ei * nblk + j, 0, 0)),
                  pl.BlockSpec(memory_space=pl.ANY),
                  pl.BlockSpec((1, d, ff), lambda ei, j: (ei, 0, 0)),
                  pl.BlockSpec((1, d, ff), lambda ei, j: (ei, 0, 0)),
                  pl.BlockSpec((1, ff, d), lambda ei, j: (ei, 0, 0))],
        out_specs=pl.BlockSpec((rows, d), lambda ei, j: (ei * nblk + j, 0)),
        out_shape=jax.ShapeDtypeStruct((e * cap, d), BF16),
        scratch_shapes=[pltpu.VMEM((2, rows, d), F32), pltpu.SemaphoreType.DMA((2,))],
        compiler_params=_cparams(("arbitrary", "arbitrary")),
        name="moe_ffn",
    )(idx3, idx3, gate3, hm, wg, wu, wd)


COMBINE_WIN = 64


def _ple_final_kernel(lo_ref, x_ref, p_ref, pos_ref, ye_hbm, gple_ref, wpg_ref, wpp_ref, gfin_ref, o_ref,
                      ywin, acc_ref, sem, *, cap, rows_per_tile):
    i = pl.program_id(0)
    win = COMBINE_WIN
    tm = x_ref.shape[0]
    los = [lo_ref[e, i * rows_per_tile] for e in range(N_EXPERTS)]
    his = [lo_ref[e, (i + 1) * rows_per_tile] for e in range(N_EXPERTS)]
    floors0 = [jnp.bitwise_and(lo, -BF16_ROWS) for lo in los]

    def fetch(floors):
        starts = [jnp.minimum(f, cap - win) for f in floors]
        copies = [pltpu.make_async_copy(ye_hbm.at[pl.ds(pl.multiple_of(e * cap + starts[e], BF16_ROWS), win)],
                                        ywin.at[pl.ds(e * win, win)], sem.at[0])
                  for e in range(N_EXPERTS)]
        for cp in copies:
            cp.start()
        return starts, copies

    def expand(starts, floors):
        pos = pos_ref[...]
        w_io = lax.broadcasted_iota(I32, (win, tm), 0)
        parts = []
        for e in range(N_EXPERTS):
            pe = pos[e:e + 1, :]
            hit = ((pe - starts[e]) == w_io) & (pe >= floors[e])
            parts.append(jnp.where(hit, 1.0, 0.0).astype(BF16))
        onehot_t = jnp.concatenate(parts, axis=0)
        return _dot_tn(onehot_t, ywin[...])

    starts, copies = fetch(floors0)
    pproj = _dot(p_ref[...].astype(BF16), wpp_ref[...])
    for cp in copies:
        cp.wait()
    acc_ref[...] = x_ref[...] + expand(starts, floors0)

    nrounds = jnp.int32(1)
    for e in range(N_EXPERTS):
        nrounds = jnp.maximum(nrounds, lax.div(his[e] - floors0[e] + (win - 1), jnp.int32(win)))

    def extra_round(r, c):
        floors = [f + r * win for f in floors0]
        starts_r, copies_r = fetch(floors)
        for cp in copies_r:
            cp.wait()
        acc_ref[...] += expand(starts_r, floors)
        return c

    lax.fori_loop(1, nrounds, extra_round, 0)

    x = acc_ref[...]
    hp = _rms(x, gple_ref[...]).astype(BF16)
    gate = _sigmoid(_dot(hp, wpg_ref[...]))
    y = x + gate * pproj
    o_ref[...] = _rms(y, gfin_ref[...])


def _ple_final(x1, p2d, pos, lo_tab, ye, cap, w, tm=256):
    n, d = x1.shape
    pd = p2d.shape[-1]
    row = lambda wd: pl.BlockSpec((tm, wd), lambda i, lo: (i, 0))
    consts = [w["gple"], w["wpg"], w["wpp"], w["gfin"]]
    const_spec = lambda c: pl.BlockSpec(c.shape, lambda i, lo: (0,) * c.ndim)
    grid_spec = pltpu.PrefetchScalarGridSpec(
        num_scalar_prefetch=1,
        grid=(n // tm,),
        in_specs=[row(d), row(pd), pl.BlockSpec((N_EXPERTS, tm), lambda i, lo: (0, i)),
                  pl.BlockSpec(memory_space=pl.ANY)] + [const_spec(c) for c in consts],
        out_specs=row(d),
        scratch_shapes=[pltpu.VMEM((N_EXPERTS * COMBINE_WIN, d), BF16), pltpu.VMEM((tm, d), F32),
                        pltpu.SemaphoreType.DMA((1,))],
    )
    return pl.pallas_call(
        functools.partial(_ple_final_kernel, cap=cap, rows_per_tile=tm // LANES),
        grid_spec=grid_spec,
        out_shape=jax.ShapeDtypeStruct((n, d), F32),
        compiler_params=_cparams(("arbitrary",)),
        name="ple_final",
    )(lo_tab, x1, p2d, pos, ye, *consts)


def _rope_tables(T):
    half = HEAD_DIM // 2
    inv = ROPE_THETA ** (-jnp.arange(0, half, 2, dtype=F32) / half)
    t = jnp.arange(T)
    ang_r = (t // GRID_W).astype(F32)[:, None] * inv
    ang_c = (t % GRID_W).astype(F32)[:, None] * inv
    cos = jnp.concatenate([jnp.cos(ang_r)] * 2 + [jnp.cos(ang_c)] * 2, axis=1)
    sin = jnp.concatenate([-jnp.sin(ang_r), jnp.sin(ang_r), -jnp.sin(ang_c), jnp.sin(ang_c)], axis=1)
    return jnp.tile(cos, (1, N_Q_HEADS)), jnp.tile(sin, (1, N_Q_HEADS))


def _block_diag_ones(width):
    i = np.arange(width)
    return jnp.asarray((i[:, None] // HEAD_DIM) == (i[None, :] // HEAD_DIM), BF16)


def _prep_weights(g_mix, w_in, q_norm, k_norm, w_af, b_af, w_ab, b_ab, gla_norm, w_attn_up, w_gla_up,
                  w_out, g_moe, w_router, w_gate_e, w_up_e, w_down_e, g_ple, w_ple_gate, w_ple_proj,
                  g_final):
    d = w_in.shape[0]
    o_gla = ATTN_W + 2 * KV_W
    o_lr = o_gla + 2 * GLA_KW + 2 * GLA_VW
    o_ga = o_lr + 2 * GLA_LOWRANK
    wlr = jnp.zeros((d, LANES), F32).at[:, :2 * GLA_LOWRANK].set(w_in[:, o_lr:o_ga])
    wgate = jnp.zeros((LANES, 2 * GLA_KW), F32)
    wgate = wgate.at[:GLA_LOWRANK, :GLA_KW].set(w_af)
    wgate = wgate.at[GLA_LOWRANK:2 * GLA_LOWRANK, GLA_KW:].set(w_ab)
    wr_t = w_router.T
    wr_hi = wr_t.astype(BF16)
    wr_lo = (wr_t - wr_hi.astype(F32)).astype(BF16)
    return dict(
        gmix=g_mix.reshape(1, d),
        wqk=w_in[:, :ATTN_W + KV_W].astype(BF16),
        wvt=w_in[:, ATTN_W + KV_W:o_gla].T.astype(BF16),
        wgla=w_in[:, o_gla:o_lr].astype(BF16),
        wlr=wlr.astype(BF16),
        wga=w_in[:, o_ga:o_ga + d].astype(BF16),
        wgl=w_in[:, o_ga + d:].astype(BF16),
        wgate=wgate.astype(BF16),
        bgate=jnp.concatenate([b_af, b_ab]).reshape(1, 2 * GLA_KW),
        qn=jnp.tile(q_norm, N_Q_HEADS).reshape(1, ATTN_W),
        kn=jnp.tile(k_norm, N_KV_HEADS).reshape(1, KV_W),
        bdq=_block_diag_ones(ATTN_W),
        bdk=_block_diag_ones(KV_W),
        gnorm=gla_norm.reshape(1, GLA_DV),
        wau=w_attn_up.astype(BF16),
        wgu=w_gla_up.astype(BF16),
        wout=w_out.astype(BF16),
        gmoe=g_moe.reshape(1, d),
        wr_hi=wr_hi,
        wr_lo=wr_lo,
        wge=w_gate_e.astype(BF16),
        wue=w_up_e.astype(BF16),
        wde=w_down_e.astype(BF16),
        gple=g_ple.reshape(1, d),
        wpg=w_ple_gate.astype(BF16),
        wpp=w_ple_proj.astype(BF16),
        gfin=g_final.reshape(1, d),
    )


def _trunk(x, p, w):
    b, t, d = x.shape
    n = b * t
    x2d = x.reshape(n, d)
    q, k, vt, lq, lk, lv, lr, gf, gb, sga, sgl = _in_proj(x2d, t, _rope_tables(t), w)
    r3 = lambda a: a.reshape(b, t, a.shape[-1])
    o_attn = _attention(r3(q), r3(k), vt).reshape(n, ATTN_W)
    o_gla = _gla(r3(lq), r3(lk), r3(lv), r3(lr), r3(gf), r3(gb), w["gnorm"]).reshape(n, GLA_VW)
    x1, hm, aff_t = _post_mix(x2d, o_attn, o_gla, sga, sgl, w)
    cap = EC_CAPACITY * n // N_EXPERTS
    idx, gate, pos, lo_tab = _route(aff_t, cap)
    ye = _moe_ffn(idx, gate, hm, w["wge"], w["wue"], w["wde"])
    y = _ple_final(x1, p.reshape(n, p.shape[-1]), pos, lo_tab, ye, cap, w)
    return y.reshape(b, t, d)


def kernel(x_prompt, x_sample, p_prompt, p_sample, g_mix, w_in, q_norm, k_norm, w_af, b_af, w_ab, b_ab,
           gla_norm, w_attn_up, w_gla_up, w_out, g_moe, w_router, w_gate_e, w_up_e, w_down_e, g_ple,
           w_ple_gate, w_ple_proj, g_final):
    assert g_mix.shape[0] == 1, "single layer"
    w = _prep_weights(g_mix[0], w_in[0], q_norm[0], k_norm[0], w_af[0], b_af[0], w_ab[0], b_ab[0],
                      gla_norm[0], w_attn_up[0], w_gla_up[0], w_out[0], g_moe[0], w_router[0],
                      w_gate_e[0], w_up_e[0], w_down_e[0], g_ple[0], w_ple_gate[0], w_ple_proj[0], g_final)
    return (_trunk(x_prompt, p_prompt[0], w), _trunk(x_sample, p_sample[0], w))
```

```python
import functools

import numpy as np
import jax
import jax.numpy as jnp
from jax import lax
from jax.experimental import pallas as pl
from jax.experimental.pallas import tpu as pltpu

F32 = jnp.float32
BF16 = jnp.bfloat16
I32 = jnp.int32

HEAD_DIM = 64
N_Q_HEADS = 8
N_KV_HEADS = 2
GRID_W = 64
ROPE_THETA = 10000.0
GLA_HEADS = 4
GLA_DK = 64
GLA_DV = 128
GLA_LOWRANK = 16
GLA_GATE_NORM = 16.0
N_EXPERTS = 16
EC_CAPACITY = 2
NORM_EPS = 1e-6
LOG2E = 1.4426950408889634

ATTN_W = N_Q_HEADS * HEAD_DIM
KV_W = N_KV_HEADS * HEAD_DIM
GLA_KW = GLA_HEADS * GLA_DK
GLA_VW = GLA_HEADS * GLA_DV

LANES = 128
SUBLANES = 8
BF16_ROWS = 16
VMEM_LIMIT = 56 * 1024 * 1024

GLA_CHUNK = 128
GLA_LEVELS = 7


def _cparams(sem):
    return pltpu.CompilerParams(dimension_semantics=sem, vmem_limit_bytes=VMEM_LIMIT)


def _dot(a, b):
    return jnp.dot(a, b, preferred_element_type=F32)


def _dot_nt(a, b):
    return lax.dot_general(a, b, (((1,), (1,)), ((), ())), preferred_element_type=F32)


def _dot_tn(a, b):
    return lax.dot_general(a, b, (((0,), (0,)), ((), ())), preferred_element_type=F32)


def _split2(x):
    hi = x.astype(BF16)
    lo = (x - hi.astype(F32)).astype(BF16)
    return hi, lo


def _split3(x):
    hi = x.astype(BF16)
    r = x - hi.astype(F32)
    mid = r.astype(BF16)
    lo = (r - mid.astype(F32)).astype(BF16)
    return hi, mid, lo


def _rms(x, gain):
    ms = jnp.mean(x * x, axis=-1, keepdims=True)
    return x * lax.rsqrt(ms + NORM_EPS) * gain


def _sigmoid(x):
    return 1.0 / (1.0 + jnp.exp(-x))


def _full_spec(shape):
    nd = len(shape)
    return pl.BlockSpec(shape, lambda *_: (0,) * nd)


def _head_ms(x, bd):
    hi, lo = _split2(x * x)
    return (_dot(hi, bd) + _dot(lo, bd)) * (1.0 / HEAD_DIM)


def _rope(x, cos, sin):
    w = x.shape[-1]
    lane = lax.broadcasted_iota(I32, x.shape, 1)
    fwd = pltpu.roll(x, w - 16, axis=1)
    bwd = pltpu.roll(x, 16, axis=1)
    partner = jnp.where((lane % 32) < 16, fwd, bwd)
    return x * cos + partner * sin


def _in_proj_kernel(x_ref, cos_ref, sin_ref, gmix_ref, wqk_ref, wvt_ref, wgla_ref, wlr_ref, wga_ref, wgl_ref,
                    wgate_ref, bgate_ref, qn_ref, kn_ref, bdq_ref, bdk_ref,
                    q_ref, k_ref, vt_ref, lq_ref, lk_ref, lv_ref, lr_ref, gf_ref, gb_ref, sga_ref, sgl_ref):
    x = x_ref[...]
    h = _rms(x, gmix_ref[...]).astype(BF16)

    qk = _dot(h, wqk_ref[...])
    cos = cos_ref[...]
    sin = sin_ref[...]
    q = qk[:, :ATTN_W]
    q = q * lax.rsqrt(_head_ms(q, bdq_ref[...]) + NORM_EPS) * qn_ref[...]
    q = _rope(q, cos, sin) * (HEAD_DIM ** -0.5 * LOG2E)
    q_ref[...] = q.astype(BF16)
    k = qk[:, ATTN_W:]
    k = k * lax.rsqrt(_head_ms(k, bdk_ref[...]) + NORM_EPS) * kn_ref[...]
    k = _rope(k, cos[:, :KV_W], sin[:, :KV_W])
    k_ref[...] = k.astype(BF16)
    vt_ref[0] = _dot_nt(wvt_ref[...], h).astype(BF16)

    gla = _dot(h, wgla_ref[...])
    lq_ref[...] = gla[:, :GLA_KW] * (GLA_DK ** -0.5)
    lk_ref[...] = gla[:, GLA_KW:2 * GLA_KW]
    lv_ref[...] = gla[:, 2 * GLA_KW:2 * GLA_KW + GLA_VW]
    lr_ref[...] = gla[:, 2 * GLA_KW + GLA_VW:]

    a = _dot(h, wlr_ref[...]).astype(BF16)
    z = _dot(a, wgate_ref[...]) + bgate_ref[...]
    g = (jnp.minimum(z, 0.0) - jnp.log(1.0 + jnp.exp(-jnp.abs(z)))) * (1.0 / GLA_GATE_NORM)
    gf_ref[...] = g[:, :GLA_KW]
    gb_ref[...] = g[:, GLA_KW:]

    sga_ref[...] = _sigmoid(_dot(h, wga_ref[...]))
    sgl_ref[...] = _sigmoid(_dot(h, wgl_ref[...]))


def _in_proj(x2d, T, tabs, w, tm=256):
    n, d = x2d.shape
    nt = T // tm
    row = lambda wd: pl.BlockSpec((tm, wd), lambda i: (i, 0))
    tab = pl.BlockSpec((tm, ATTN_W), lambda i: (i % nt, 0))
    consts = [w["gmix"], w["wqk"], w["wvt"], w["wgla"], w["wlr"], w["wga"], w["wgl"], w["wgate"], w["bgate"],
              w["qn"], w["kn"], w["bdq"], w["bdk"]]
    outs = [(ATTN_W, BF16), (KV_W, BF16), None, (GLA_KW, F32), (GLA_KW, F32), (GLA_VW, F32),
            (GLA_VW, F32), (GLA_KW, F32), (GLA_KW, F32), (d, F32), (d, F32)]
    vt_spec = pl.BlockSpec((1, KV_W, tm), lambda i: (i // nt, 0, i % nt))
    vt_shape = jax.ShapeDtypeStruct((n // T, KV_W, T), BF16)
    return pl.pallas_call(
        _in_proj_kernel,
        grid=(n // tm,),
        in_specs=[row(d), tab, tab] + [_full_spec(c.shape) for c in consts],
        out_specs=[vt_spec if o is None else row(o[0]) for o in outs],
        out_shape=[vt_shape if o is None else jax.ShapeDtypeStruct((n, o[0]), o[1]) for o in outs],
        compiler_params=_cparams(("parallel",)),
        name="in_proj",
    )(x2d, tabs[0], tabs[1], *consts)


def _attn_kernel(q_ref, k_ref, vt_ref, o_ref, qpad_ref, m_ref, acc_ref, s_even, s_odd, *, nk):
    j = pl.program_id(2)
    group = N_Q_HEADS // N_KV_HEADS

    def head_scores(dst, hq):
        dst[hq] = _dot_nt(k_ref[0], qpad_ref[hq])

    def head_softmax(src, hq):
        m_old = m_ref[hq]
        m_new = jnp.maximum(m_old, jnp.max(src[hq], axis=0, keepdims=True))
        alpha = jnp.exp2(m_old - m_new)
        p = jnp.exp2(src[hq] - m_new)
        m_ref[hq] = m_new
        return p.astype(BF16), alpha

    def head_values(hq, p, alpha):
        hk = hq // group
        ones = jnp.ones((BF16_ROWS, vt_ref.shape[2]), BF16)
        vt = vt_ref[0, hk * HEAD_DIM:(hk + 1) * HEAD_DIM, :]
        vt1 = jnp.concatenate([vt, ones], axis=0)
        acc_ref[hq] = alpha * acc_ref[hq] + _dot(vt1, p)

    def scores_into(dst):
        for hq in range(N_Q_HEADS):
            head_scores(dst, hq)

    def consume(src, dst=None):
        pending = None
        for hq in range(N_Q_HEADS):
            if dst is not None:
                head_scores(dst, hq)
            p, alpha = head_softmax(src, hq)
            if pending is not None:
                head_values(*pending)
            pending = (hq, p, alpha)
        head_values(*pending)

    @pl.when(j == 0)
    def _():
        m_ref[...] = jnp.full(m_ref.shape, -jnp.inf, F32)
        acc_ref[...] = jnp.zeros(acc_ref.shape, F32)
        lane = lax.broadcasted_iota(I32, (q_ref.shape[1], LANES), 1)
        for hq in range(N_Q_HEADS):
            hk = hq // group
            chunk = q_ref[0, :, (hq // 2) * LANES:(hq // 2 + 1) * LANES].astype(F32)
            if hq % 2 != hk:
                chunk = pltpu.roll(chunk, HEAD_DIM, axis=1)
            keep = (lane // HEAD_DIM) == hk
            qpad_ref[hq] = jnp.where(keep, chunk, 0.0).astype(BF16)
        scores_into(s_even)

    @pl.when((j > 0) & (j < nk) & (lax.rem(j, 2) == 1))
    def _():
        consume(s_even, s_odd)

    @pl.when((j > 0) & (j < nk) & (lax.rem(j, 2) == 0))
    def _():
        consume(s_odd, s_even)

    @pl.when(j == nk)
    def _():
        consume(s_odd if nk % 2 == 0 else s_even)
        outs = [acc_ref[hq, :HEAD_DIM, :] / acc_ref[hq, HEAD_DIM:HEAD_DIM + 1, :] for hq in range(N_Q_HEADS)]
        o_ref[0] = jnp.concatenate(outs, axis=0).T.astype(BF16)


def _attention(q, k, vt, tq=512, tk=512):
    b, t, _ = q.shape
    nk = t // tk
    s_buf = pltpu.VMEM((N_Q_HEADS, tk, tq), F32)
    return pl.pallas_call(
        functools.partial(_attn_kernel, nk=nk),
        grid=(b, t // tq, nk + 1),
        in_specs=[pl.BlockSpec((1, tq, ATTN_W), lambda bi, i, j: (bi, i, 0)),
                  pl.BlockSpec((1, tk, KV_W), lambda bi, i, j: (bi, jnp.minimum(j, nk - 1), 0)),
                  pl.BlockSpec((1, KV_W, tk), lambda bi, i, j: (bi, 0, jnp.maximum(j - 1, 0)))],
        out_specs=pl.BlockSpec((1, tq, ATTN_W), lambda bi, i, j: (bi, i, 0)),
        out_shape=jax.ShapeDtypeStruct((b, t, ATTN_W), BF16),
        scratch_shapes=[pltpu.VMEM((N_Q_HEADS, tq, LANES), BF16),
                        pltpu.VMEM((N_Q_HEADS, 1, tq), F32),
                        pltpu.VMEM((N_Q_HEADS, HEAD_DIM + BF16_ROWS, tq), F32),
                        s_buf, s_buf],
        compiler_params=_cparams(("parallel", "parallel", "arbitrary")),
        name="attention",
    )(q, k, vt)


def _gla_chunk(q, k, v, g, st_ref, mask_ref, reverse):
    c = GLA_CHUNK
    pair_w = 2 * GLA_DK
    t_kw = lax.broadcasted_iota(I32, (c, GLA_KW), 0)
    lo_lanes = lax.broadcasted_iota(I32, (c, pair_w), 1) < GLA_DK

    a_pairs = [jnp.zeros((c, 2 * c), F32) for _ in range(2)]

    def add_level(a_pairs, qh, kh, mask2):
        qb = qh.astype(BF16)
        kb = kh.astype(BF16)
        out = []
        for p in range(2):
            qp = qb[:, p * pair_w:(p + 1) * pair_w]
            kp = kb[:, p * pair_w:(p + 1) * pair_w]
            k2 = jnp.concatenate([jnp.where(lo_lanes, kp, 0.0), jnp.where(lo_lanes, 0.0, kp)], axis=0)
            out.append(a_pairs[p] + mask2 * _dot_nt(qp, k2))
        return out

    pre = g
    tot = g
    for lvl in range(GLA_LEVELS):
        s = 1 << lvl
        odd = ((t_kw >> lvl) & 1) == 1
        if not reverse:
            qh = q * jnp.exp(pre)
            kh = k * jnp.exp(tot - pre)
        else:
            qh = q * jnp.exp(tot - pre + g)
            kh = k * jnp.exp(pre - g)
        a_pairs = add_level(a_pairs, qh, kh, mask_ref[lvl])
        sib = jnp.where(odd, pltpu.roll(tot, s, axis=0), pltpu.roll(tot, c - s, axis=0))
        pre = pre + jnp.where(odd, sib, 0.0)
        tot = tot + sib
    if not reverse:
        a_pairs = add_level(a_pairs, q, k, mask_ref[GLA_LEVELS])

    if not reverse:
        q_dec = q * jnp.exp(pre)
        k_dec = k * jnp.exp(tot - pre)
    else:
        q_dec = q * jnp.exp(tot - pre + g)
        k_dec = k * jnp.exp(pre - g)
    chunk_decay = jnp.exp(tot[0:1, :])
    sub_pair = lax.broadcasted_iota(I32, (2 * GLA_DV, pair_w), 0)
    lane_st = lax.broadcasted_iota(I32, (2 * GLA_DV, pair_w), 1)
    block_diag = (sub_pair // GLA_DV) == (lane_st // GLA_DK)
    first_head_v = lax.broadcasted_iota(I32, (c, 2 * GLA_DV), 1) < GLA_DV

    outs = []
    for p in range(2):
        st = st_ref[p]
        qd = q_dec[:, p * pair_w:(p + 1) * pair_w].astype(BF16)
        o_pair = _dot_nt(qd, st.astype(BF16))
        vp = v[:, p * 2 * GLA_DV:(p + 1) * 2 * GLA_DV].astype(BF16)
        kd = k_dec[:, p * pair_w:(p + 1) * pair_w].astype(BF16)
        upd = _dot_tn(vp, kd)
        st_ref[p] = st * chunk_decay[:, p * pair_w:(p + 1) * pair_w] + jnp.where(block_diag, upd, 0.0)
        v_diag = jnp.concatenate([jnp.where(first_head_v, vp, 0.0), jnp.where(first_head_v, 0.0, vp)], axis=0)
        outs.append(o_pair + _dot(a_pairs[p].astype(BF16), v_diag))
    return jnp.concatenate(outs, axis=1)


def _gla_level_masks(reverse):
    c = GLA_CHUNK
    t = np.arange(c)[:, None]
    s = np.arange(c)[None, :]
    masks = []
    for lvl in range(GLA_LEVELS):
        bt, bs = t >> lvl, s >> lvl
        masks.append((bs == bt + 1) & (bt % 2 == 0) if reverse else (bt == bs + 1) & (bs % 2 == 0))
    masks.append(t == s)
    m = np.stack(masks).astype(np.float32)
    return jnp.asarray(np.concatenate([m, m], axis=2))


def _gla_bwd_kernel(q_ref, k_ref, v_ref, g_ref, mask_ref, o_ref, st_ref):
    @pl.when(pl.program_id(1) == 0)
    def _():
        st_ref[...] = jnp.zeros(st_ref.shape, F32)

    o_ref[0] = _gla_chunk(q_ref[0], k_ref[0], v_ref[0], g_ref[0], st_ref, mask_ref, True)


def _gla_fwd_kernel(q_ref, k_ref, v_ref, g_ref, ob_ref, r_ref, gn_ref, mask_ref, o_ref, st_ref):
    @pl.when(pl.program_id(1) == 0)
    def _():
        st_ref[...] = jnp.zeros(st_ref.shape, F32)

    o = _gla_chunk(q_ref[0], k_ref[0], v_ref[0], g_ref[0], st_ref, mask_ref, False) + ob_ref[0]
    r = r_ref[0]
    gate = r * _sigmoid(r)
    gn = gn_ref[...]
    outs = []
    for h in range(GLA_HEADS):
        oh = o[:, h * GLA_DV:(h + 1) * GLA_DV]
        outs.append(_rms(oh, gn))
    o_ref[0] = (jnp.concatenate(outs, axis=1) * gate).astype(BF16)


def _gla(lq, lk, lv, lr, gf, gb, gla_norm):
    b, t, _ = lq.shape
    c = GLA_CHUNK
    nc = t // c
    fwd = lambda wd: pl.BlockSpec((1, c, wd), lambda bi, ci: (bi, ci, 0))
    rev = lambda wd: pl.BlockSpec((1, c, wd), lambda bi, ci: (bi, nc - 1 - ci, 0))
    st = pltpu.VMEM((2, 2 * GLA_DV, 2 * GLA_DK), F32)
    mask_b = _gla_level_masks(True)
    mask_f = _gla_level_masks(False)
    o_b = pl.pallas_call(
        _gla_bwd_kernel,
        grid=(b, nc),
        in_specs=[rev(GLA_KW), rev(GLA_KW), rev(GLA_VW), rev(GLA_KW), _full_spec(mask_b.shape)],
        out_specs=rev(GLA_VW),
        out_shape=jax.ShapeDtypeStruct((b, t, GLA_VW), F32),
        scratch_shapes=[st],
        compiler_params=_cparams(("parallel", "arbitrary")),
        name="gla_bwd",
    )(lq, lk, lv, gb, mask_b)
    return pl.pallas_call(
        _gla_fwd_kernel,
        grid=(b, nc),
        in_specs=[fwd(GLA_KW), fwd(GLA_KW), fwd(GLA_VW), fwd(GLA_KW), fwd(GLA_VW), fwd(GLA_VW),
                  _full_spec(gla_norm.shape), _full_spec(mask_f.shape)],
        out_specs=fwd(GLA_VW),
        out_shape=jax.ShapeDtypeStruct((b, t, GLA_VW), BF16),
        scratch_shapes=[st],
        compiler_params=_cparams(("parallel", "arbitrary")),
        name="gla_fwd",
    )(lq, lk, lv, gf, o_b, lr, gla_norm, mask_f)


def _post_mix_kernel(x_ref, oa_ref, og_ref, sga_ref, sgl_ref, wau_ref, wgu_ref, wout_ref, gmoe_ref,
                     wr_hi_ref, wr_lo_ref, x1_ref, hm_ref, aff_ref):
    ua = _dot(oa_ref[...], wau_ref[...])
    ul = _dot(og_ref[...], wgu_ref[...])
    merged = sga_ref[...] * ua + sgl_ref[...] * ul
    x1 = x_ref[...] + _dot(merged.astype(BF16), wout_ref[...])
    x1_ref[...] = x1
    hm = _rms(x1, gmoe_ref[...])
    hm_ref[...] = hm
    h_hi, h_lo = _split2(hm)
    wr_hi = wr_hi_ref[...]
    logits = _dot_nt(wr_hi, h_hi) + _dot_nt(wr_hi, h_lo) + _dot_nt(wr_lo_ref[...], h_hi)
    m = jnp.max(logits, axis=0, keepdims=True)
    e = jnp.exp(logits - m)
    aff_ref[...] = e / jnp.sum(e, axis=0, keepdims=True)


def _post_mix(x2d, oa, og, sga, sgl, w, tm=256):
    n, d = x2d.shape
    row = lambda wd: pl.BlockSpec((tm, wd), lambda i: (i, 0))
    consts = [w["wau"], w["wgu"], w["wout"], w["gmoe"], w["wr_hi"], w["wr_lo"]]
    return pl.pallas_call(
        _post_mix_kernel,
        grid=(n // tm,),
        in_specs=[row(d), row(ATTN_W), row(GLA_VW), row(d), row(d)] + [_full_spec(c.shape) for c in consts],
        out_specs=[row(d), row(d), pl.BlockSpec((N_EXPERTS, tm), lambda i: (0, i))],
        out_shape=[jax.ShapeDtypeStruct((n, d), F32), jax.ShapeDtypeStruct((n, d), F32),
                   jax.ShapeDtypeStruct((N_EXPERTS, n), F32)],
        compiler_params=_cparams(("parallel",)),
        name="post_mix",
    )(x2d, oa, og, sga, sgl, *consts)


def _route_kernel(a_ref, idx_ref, gate_ref, pos_ref, lo_ref, *, cap):
    a = a_ref[0]
    nrow = a.shape[0]
    b_io = lax.broadcasted_iota(I32, (nrow, LANES), 0)
    t_io = lax.broadcasted_iota(I32, (nrow, LANES), 1)
    tok = b_io * LANES + t_io

    def as_float(word):
        return lax.bitcast_convert_type(jnp.full((1, LANES), word, I32), F32)

    def thr_step(i, thr):
        cand = thr | lax.shift_left(jnp.int32(1), jnp.int32(30) - i)
        cnt = jnp.sum((a >= as_float(cand)).astype(I32))
        return jnp.where(cnt >= cap, cand, thr)

    thr = lax.fori_loop(0, 31, thr_step, jnp.int32(0))
    gt = a >= as_float(thr + 1)
    eq = (a >= as_float(thr)) & jnp.logical_not(gt)
    need = cap - jnp.sum(gt.astype(I32))

    def tie_step(i, lim):
        cand = lim + lax.shift_left(jnp.int32(1), jnp.int32(17) - i)
        cnt = jnp.sum((eq & (tok < cand)).astype(I32))
        return jnp.where(cnt < need, cand, lim)

    lim = lax.fori_loop(0, 18, tie_step, jnp.int32(0))
    selm = gt | (eq & (tok <= lim))
    selb = selm.astype(F32).astype(BF16)

    r_io = lax.broadcasted_iota(I32, (LANES, LANES), 0)
    c_io = lax.broadcasted_iota(I32, (LANES, LANES), 1)
    utri = (r_io <= c_io).astype(BF16)
    cum_row = _dot(selb, utri)
    row_cnt = cum_row[:, LANES - 1:LANES]
    br_io = lax.broadcasted_iota(I32, (nrow, nrow), 0)
    bc_io = lax.broadcasted_iota(I32, (nrow, nrow), 1)
    lrow = (bc_io <= br_io).astype(BF16)
    row_incl = _dot(lrow, jnp.broadcast_to(row_cnt, (nrow, LANES)).astype(BF16))
    row_excl = row_incl - row_cnt
    pos_ref[0] = jnp.where(selm, row_excl + cum_row - 1.0, -1.0).astype(I32)
    lo_ref[0] = row_excl.astype(I32)
    row_incl_c = row_incl[:, 0:1]
    row_excl_c = row_excl[:, 0:1]
    cum_t = cum_row.T.astype(BF16)
    a_hi, a_mid, a_lo = _split3(a.T)
    brow = lax.broadcasted_iota(I32, (nrow, LANES), 0).astype(F32)
    tsub = lax.broadcasted_iota(I32, (LANES, LANES), 0).astype(F32)
    lane = lax.broadcasted_iota(I32, (1, LANES), 1).astype(F32)

    def slot_tile(jt, carry):
        j = lane + jnp.asarray(jt, F32) * float(LANES)
        bj = jnp.sum((row_incl_c <= j).astype(F32), axis=0, keepdims=True)
        onehot = (brow == bj).astype(F32)
        base = jnp.sum(onehot * row_excl_c, axis=0, keepdims=True)
        jl = j - base
        ohb = onehot.astype(BF16)
        rt = _dot(cum_t, ohb)
        tj = jnp.sum((rt <= jl).astype(F32), axis=0, keepdims=True)
        acol = _dot(a_hi, ohb) + _dot(a_mid, ohb) + _dot(a_lo, ohb)
        gate = jnp.sum(jnp.where(tsub == tj, acol, 0.0), axis=0, keepdims=True)
        idx_ref[0, pl.ds(jt, 1), :] = (bj * float(LANES) + tj).astype(I32)
        gate_ref[0, pl.ds(jt, 1), :] = gate
        return carry

    lax.fori_loop(0, cap // LANES, slot_tile, 0)


def _route(aff_t, cap):
    e, n = aff_t.shape
    nrow = n // LANES
    a3 = aff_t.reshape(e, nrow, LANES)
    nt = cap // LANES
    slot_spec = pl.BlockSpec((1, nt, LANES), lambda i: (i, 0, 0))
    tok_spec = pl.BlockSpec((1, nrow, LANES), lambda i: (i, 0, 0))
    idx, gate, pos, lo = pl.pallas_call(
        functools.partial(_route_kernel, cap=cap),
        grid=(e,),
        in_specs=[tok_spec],
        out_specs=[slot_spec, slot_spec, tok_spec, tok_spec],
        out_shape=[jax.ShapeDtypeStruct((e, nt, LANES), I32), jax.ShapeDtypeStruct((e, nt, LANES), F32),
                   jax.ShapeDtypeStruct((e, nrow, LANES), I32), jax.ShapeDtypeStruct((e, nrow, LANES), I32)],
        compiler_params=_cparams(("parallel",)),
        name="route",
    )(a3)
    lo_tab = jnp.concatenate([lo[:, :, 0], jnp.full((e, 1), cap, I32)], axis=1)
    return idx, gate, pos.reshape(e, n), lo_tab


MOE_ISSUE_UNROLL = 8


def _moe_kernel(idx_ref, idxn_ref, gate_ref, hm_hbm, wg_ref, wu_ref, wd_ref, ye_ref, buf_a, buf_b, sem, *, rows):
    nblk = pl.num_programs(1)
    s = pl.program_id(0) * nblk + pl.program_id(1)
    nsteps = pl.num_programs(0) * nblk

    def buffer_wait(buf, k):
        pltpu.make_async_copy(hm_hbm.at[pl.ds(0, rows)], buf, sem.at[k]).wait()

    @pl.when(s == 0)
    def _():
        def body(r0, c):
            for u in range(MOE_ISSUE_UNROLL):
                r = r0 * MOE_ISSUE_UNROLL + u
                n = idx_ref[0, 0, r]
                pltpu.make_async_copy(hm_hbm.at[pl.ds(n, 1)], buf_a.at[pl.ds(r, 1)], sem.at[0]).start()
            return c
        lax.fori_loop(0, rows // MOE_ISSUE_UNROLL, body, 0)

    def step(cur, k_cur, nxt, k_nxt):
        buffer_wait(cur, k_cur)
        for r in range(rows):
            n = idxn_ref[0, 0, r]
            pltpu.make_async_copy(hm_hbm.at[pl.ds(n, 1)], nxt.at[pl.ds(r, 1)], sem.at[k_nxt]).start()

        xe = cur[...].astype(BF16)
        hid_g = _dot(xe, wg_ref[0])
        hid = (hid_g * _sigmoid(hid_g)) * _dot(xe, wu_ref[0])
        y = _dot(hid.astype(BF16), wd_ref[0])
        eye = (lax.broadcasted_iota(I32, (LANES, LANES), 0) == lax.broadcasted_iota(I32, (LANES, LANES), 1))
        for c in range(rows // LANES):
            g_row = gate_ref[0, :, c * LANES:(c + 1) * LANES]
            g_col = jnp.sum(jnp.where(eye, g_row, 0.0), axis=1, keepdims=True)
            sl = slice(c * LANES, (c + 1) * LANES)
            ye_ref[sl, :] = (y[sl, :] * g_col).astype(BF16)

        @pl.when(s == nsteps - 1)
        def _():
            buffer_wait(nxt, k_nxt)

    @pl.when(lax.rem(s, 2) == 0)
    def _():
        step(buf_a, 0, buf_b, 1)

    @pl.when(lax.rem(s, 2) == 1)
    def _():
        step(buf_b, 1, buf_a, 0)


def _moe_ffn(idx, gate, hm, wg, wu, wd, rows=256):
    e, nt, _ = idx.shape
    cap = nt * LANES
    n, d = hm.shape
    nblk = cap // rows
    nsteps = e * nblk
    idx3 = idx.reshape(nsteps, 1, rows)
    gate3 = gate.reshape(nsteps, 1, rows)
    ff = wg.shape[-1]
    return pl.pallas_call(
        functools.partial(_moe_kernel, rows=rows),
        grid=(e, nblk),
        in_specs=[pl.BlockSpec((1, 1, rows), lambda ei, j: (ei * nblk + j, 0, 0), memory_space=pltpu.SMEM),
                  pl.BlockSpec((1, 1, rows), lambda ei, j: (jnp.minimum(ei * nblk + j + 1, nsteps - 1), 0, 0),
                               memory_space=pltpu.SMEM),
                  pl.BlockSpec((1, 1, rows), lambda ei, j: (ei * nblk + j, 0, 0)),
                  pl.BlockSpec(memory_space=pl.ANY),
                  pl.BlockSpec((1, d, ff), lambda ei, j: (ei, 0, 0)),
                  pl.BlockSpec((1, d, ff), lambda ei, j: (ei, 0, 0)),
                  pl.BlockSpec((1, ff, d), lambda ei, j: (ei, 0, 0))],
        out_specs=pl.BlockSpec((rows, d), lambda ei, j: (ei * nblk + j, 0)),
        out_shape=jax.ShapeDtypeStruct((e * cap, d), BF16),
        scratch_shapes=[pltpu.VMEM((rows, d), F32), pltpu.VMEM((rows, d), F32), pltpu.SemaphoreType.DMA((2,))],
        compiler_params=_cparams(("arbitrary", "arbitrary")),
        name="moe_ffn",
    )(idx3, idx3, gate3, hm, wg, wu, wd)


COMBINE_WIN = 64


def _ple_final_kernel(lo_ref, x_ref, p_ref, pos_ref, ye_hbm, gple_ref, wpg_ref, wpp_ref, gfin_ref, o_ref,
                      win_a, win_b, acc_ref, sem, *, cap, rows_per_tile):
    i = pl.program_id(0)
    ntiles = pl.num_programs(0)
    win = COMBINE_WIN
    tm = x_ref.shape[0]

    def tile_floors(ti):
        return [jnp.bitwise_and(lo_ref[e, ti * rows_per_tile], -BF16_ROWS) for e in range(N_EXPERTS)]

    def clamp(floors):
        return [jnp.minimum(f, cap - win) for f in floors]

    def start_fetch(floors, buf, k):
        starts = clamp(floors)
        for e in range(N_EXPERTS):
            src = ye_hbm.at[pl.ds(pl.multiple_of(e * cap + starts[e], BF16_ROWS), win)]
            pltpu.make_async_copy(src, buf.at[pl.ds(e * win, win)], sem.at[k]).start()

    def wait_fetch(buf, k):
        pltpu.make_async_copy(ye_hbm.at[pl.ds(0, N_EXPERTS * win)], buf, sem.at[k]).wait()

    def expand(floors, buf):
        starts = clamp(floors)
        pos = pos_ref[...]
        w_io = lax.broadcasted_iota(I32, (win, tm), 0)
        parts = []
        for e in range(N_EXPERTS):
            pe = pos[e:e + 1, :]
            hit = ((pe - starts[e]) == w_io) & (pe >= floors[e])
            parts.append(jnp.where(hit, 1.0, 0.0).astype(BF16))
        onehot_t = jnp.concatenate(parts, axis=0)
        return _dot_tn(onehot_t, buf[...])

    @pl.when(i == 0)
    def _():
        start_fetch(tile_floors(0), win_a, 0)

    def step(cur, k_cur, nxt, k_nxt):
        floors0 = tile_floors(i)
        wait_fetch(cur, k_cur)

        @pl.when(i + 1 < ntiles)
        def _():
            start_fetch(tile_floors(i + 1), nxt, k_nxt)

        acc_ref[...] = x_ref[...] + expand(floors0, cur)

        nrounds = jnp.int32(1)
        for e in range(N_EXPERTS):
            hi = lo_ref[e, (i + 1) * rows_per_tile]
            nrounds = jnp.maximum(nrounds, lax.div(hi - floors0[e] + (win - 1), jnp.int32(win)))

        def extra_round(r, c):
            floors = [f + r * win for f in floors0]
            start_fetch(floors, cur, k_cur)
            wait_fetch(cur, k_cur)
            acc_ref[...] += expand(floors, cur)
            return c

        lax.fori_loop(1, nrounds, extra_round, 0)

        x = acc_ref[...]
        hp = _rms(x, gple_ref[...]).astype(BF16)
        gate = _sigmoid(_dot(hp, wpg_ref[...]))
        y = x + gate * _dot(p_ref[...].astype(BF16), wpp_ref[...])
        o_ref[...] = _rms(y, gfin_ref[...])

    @pl.when(lax.rem(i, 2) == 0)
    def _():
        step(win_a, 0, win_b, 1)

    @pl.when(lax.rem(i, 2) == 1)
    def _():
        step(win_b, 1, win_a, 0)


def _ple_final(x1, p2d, pos, lo_tab, ye, cap, w, tm=256):
    n, d = x1.shape
    pd = p2d.shape[-1]
    row = lambda wd: pl.BlockSpec((tm, wd), lambda i, lo: (i, 0))
    consts = [w["gple"], w["wpg"], w["wpp"], w["gfin"]]
    const_spec = lambda c: pl.BlockSpec(c.shape, lambda i, lo: (0,) * c.ndim)
    grid_spec = pltpu.PrefetchScalarGridSpec(
        num_scalar_prefetch=1,
        grid=(n // tm,),
        in_specs=[row(d), row(pd), pl.BlockSpec((N_EXPERTS, tm), lambda i, lo: (0, i)),
                  pl.BlockSpec(memory_space=pl.ANY)] + [const_spec(c) for c in consts],
        out_specs=row(d),
        scratch_shapes=[pltpu.VMEM((N_EXPERTS * COMBINE_WIN, d), BF16),
                        pltpu.VMEM((N_EXPERTS * COMBINE_WIN, d), BF16), pltpu.VMEM((tm, d), F32),
                        pltpu.SemaphoreType.DMA((2,))],
    )
    return pl.pallas_call(
        functools.partial(_ple_final_kernel, cap=cap, rows_per_tile=tm // LANES),
        grid_spec=grid_spec,
        out_shape=jax.ShapeDtypeStruct((n, d), F32),
        compiler_params=_cparams(("arbitrary",)),
        name="ple_final",
    )(lo_tab, x1, p2d, pos, ye, *consts)


def _rope_tables(T):
    half = HEAD_DIM // 2
    inv = ROPE_THETA ** (-jnp.arange(0, half, 2, dtype=F32) / half)
    t = jnp.arange(T)
    ang_r = (t // GRID_W).astype(F32)[:, None] * inv
    ang_c = (t % GRID_W).astype(F32)[:, None] * inv
    cos = jnp.concatenate([jnp.cos(ang_r)] * 2 + [jnp.cos(ang_c)] * 2, axis=1)
    sin = jnp.concatenate([-jnp.sin(ang_r), jnp.sin(ang_r), -jnp.sin(ang_c), jnp.sin(ang_c)], axis=1)
    return jnp.tile(cos, (1, N_Q_HEADS)), jnp.tile(sin, (1, N_Q_HEADS))


def _block_diag_ones(width):
    i = np.arange(width)
    return jnp.asarray((i[:, None] // HEAD_DIM) == (i[None, :] // HEAD_DIM), BF16)


def _prep_weights(g_mix, w_in, q_norm, k_norm, w_af, b_af, w_ab, b_ab, gla_norm, w_attn_up, w_gla_up,
                  w_out, g_moe, w_router, w_gate_e, w_up_e, w_down_e, g_ple, w_ple_gate, w_ple_proj,
                  g_final):
    d = w_in.shape[0]
    o_gla = ATTN_W + 2 * KV_W
    o_lr = o_gla + 2 * GLA_KW + 2 * GLA_VW
    o_ga = o_lr + 2 * GLA_LOWRANK
    wlr = jnp.zeros((d, LANES), F32).at[:, :2 * GLA_LOWRANK].set(w_in[:, o_lr:o_ga])
    wgate = jnp.zeros((LANES, 2 * GLA_KW), F32)
    wgate = wgate.at[:GLA_LOWRANK, :GLA_KW].set(w_af)
    wgate = wgate.at[GLA_LOWRANK:2 * GLA_LOWRANK, GLA_KW:].set(w_ab)
    wr_t = w_router.T
    wr_hi = wr_t.astype(BF16)
    wr_lo = (wr_t - wr_hi.astype(F32)).astype(BF16)
    return dict(
        gmix=g_mix.reshape(1, d),
        wqk=w_in[:, :ATTN_W + KV_W].astype(BF16),
        wvt=w_in[:, ATTN_W + KV_W:o_gla].T.astype(BF16),
        wgla=w_in[:, o_gla:o_lr].astype(BF16),
        wlr=wlr.astype(BF16),
        wga=w_in[:, o_ga:o_ga + d].astype(BF16),
        wgl=w_in[:, o_ga + d:].astype(BF16),
        wgate=wgate.astype(BF16),
        bgate=jnp.concatenate([b_af, b_ab]).reshape(1, 2 * GLA_KW),
        qn=jnp.tile(q_norm, N_Q_HEADS).reshape(1, ATTN_W),
        kn=jnp.tile(k_norm, N_KV_HEADS).reshape(1, KV_W),
        bdq=_block_diag_ones(ATTN_W),
        bdk=_block_diag_ones(KV_W),
        gnorm=gla_norm.reshape(1, GLA_DV),
        wau=w_attn_up.astype(BF16),
        wgu=w_gla_up.astype(BF16),
        wout=w_out.astype(BF16),
        gmoe=g_moe.reshape(1, d),
        wr_hi=wr_hi,
        wr_lo=wr_lo,
        wge=w_gate_e.astype(BF16),
        wue=w_up_e.astype(BF16),
        wde=w_down_e.astype(BF16),
        gple=g_ple.reshape(1, d),
        wpg=w_ple_gate.astype(BF16),
        wpp=w_ple_proj.astype(BF16),
        gfin=g_final.reshape(1, d),
    )


def _trunk(x, p, w):
    b, t, d = x.shape
    n = b * t
    x2d = x.reshape(n, d)
    q, k, vt, lq, lk, lv, lr, gf, gb, sga, sgl = _in_proj(x2d, t, _rope_tables(t), w)
    r3 = lambda a: a.reshape(b, t, a.shape[-1])
    o_attn = _attention(r3(q), r3(k), vt).reshape(n, ATTN_W)
    o_gla = _gla(r3(lq), r3(lk), r3(lv), r3(lr), r3(gf), r3(gb), w["gnorm"]).reshape(n, GLA_VW)
    x1, hm, aff_t = _post_mix(x2d, o_attn, o_gla, sga, sgl, w)
    cap = EC_CAPACITY * n // N_EXPERTS
    idx, gate, pos, lo_tab = _route(aff_t, cap)
    ye = _moe_ffn(idx, gate, hm, w["wge"], w["wue"], w["wde"])
    y = _ple_final(x1, p.reshape(n, p.shape[-1]), pos, lo_tab, ye, cap, w)
    return y.reshape(b, t, d)


def kernel(x_prompt, x_sample, p_prompt, p_sample, g_mix, w_in, q_norm, k_norm, w_af, b_af, w_ab, b_ab,
           gla_norm, w_attn_up, w_gla_up, w_out, g_moe, w_router, w_gate_e, w_up_e, w_down_e, g_ple,
           w_ple_gate, w_ple_proj, g_final):
    assert g_mix.shape[0] == 1, "single layer"
    w = _prep_weights(g_mix[0], w_in[0], q_norm[0], k_norm[0], w_af[0], b_af[0], w_ab[0], b_ab[0],
                      gla_norm[0], w_attn_up[0], w_gla_up[0], w_out[0], g_moe[0], w_router[0],
                      w_gate_e[0], w_up_e[0], w_down_e[0], g_ple[0], w_ple_gate[0], w_ple_proj[0], g_final)
    return (_trunk(x_prompt, p_prompt[0], w), _trunk(x_sample, p_sample[0], w))
```

```python
import functools

import numpy as np
import jax
import jax.numpy as jnp
from jax import lax
from jax.experimental import pallas as pl
from jax.experimental.pallas import tpu as pltpu

F32 = jnp.float32
BF16 = jnp.bfloat16
I32 = jnp.int32

HEAD_DIM = 64
N_Q_HEADS = 8
N_KV_HEADS = 2
GRID_W = 64
ROPE_THETA = 10000.0
GLA_HEADS = 4
GLA_DK = 64
GLA_DV = 128
GLA_LOWRANK = 16
GLA_GATE_NORM = 16.0
N_EXPERTS = 16
EC_CAPACITY = 2
NORM_EPS = 1e-6
LOG2E = 1.4426950408889634

ATTN_W = N_Q_HEADS * HEAD_DIM
KV_W = N_KV_HEADS * HEAD_DIM
GLA_KW = GLA_HEADS * GLA_DK
GLA_VW = GLA_HEADS * GLA_DV

LANES = 128
SUBLANES = 8
BF16_ROWS = 16
VMEM_LIMIT = 56 * 1024 * 1024

GLA_CHUNK = 128
GLA_LEVELS = 7


def _cparams(sem):
    return pltpu.CompilerParams(dimension_semantics=sem, vmem_limit_bytes=VMEM_LIMIT)


def _dot(a, b):
    return jnp.dot(a, b, preferred_element_type=F32)


def _dot_nt(a, b):
    return lax.dot_general(a, b, (((1,), (1,)), ((), ())), preferred_element_type=F32)


def _dot_tn(a, b):
    return lax.dot_general(a, b, (((0,), (0,)), ((), ())), preferred_element_type=F32)


def _split2(x):
    hi = x.astype(BF16)
    lo = (x - hi.astype(F32)).astype(BF16)
    return hi, lo


def _split3(x):
    hi = x.astype(BF16)
    r = x - hi.astype(F32)
    mid = r.astype(BF16)
    lo = (r - mid.astype(F32)).astype(BF16)
    return hi, mid, lo


def _rms(x, gain):
    ms = jnp.mean(x * x, axis=-1, keepdims=True)
    return x * lax.rsqrt(ms + NORM_EPS) * gain


def _sigmoid(x):
    return 1.0 / (1.0 + jnp.exp(-x))


def _full_spec(shape):
    nd = len(shape)
    return pl.BlockSpec(shape, lambda *_: (0,) * nd)


def _head_ms(x, bd):
    hi, lo = _split2(x * x)
    return (_dot(hi, bd) + _dot(lo, bd)) * (1.0 / HEAD_DIM)


def _rope(x, cos, sin):
    w = x.shape[-1]
    lane = lax.broadcasted_iota(I32, x.shape, 1)
    fwd = pltpu.roll(x, w - 16, axis=1)
    bwd = pltpu.roll(x, 16, axis=1)
    partner = jnp.where((lane % 32) < 16, fwd, bwd)
    return x * cos + partner * sin


def _in_proj_kernel(x_ref, cos_ref, sin_ref, gmix_ref, wqk_ref, wvt_ref, wgla_ref, wlr_ref,
                    wgate_ref, bgate_ref, qn_ref, kn_ref, bdq_ref, bdk_ref,
                    q_ref, k_ref, vt_ref, lq_ref, lk_ref, lv_ref, lr_ref, gf_ref, gb_ref):
    x = x_ref[...]
    h = _rms(x, gmix_ref[...]).astype(BF16)

    qk = _dot(h, wqk_ref[...])
    cos = cos_ref[...]
    sin = sin_ref[...]
    q = qk[:, :ATTN_W]
    q = q * lax.rsqrt(_head_ms(q, bdq_ref[...]) + NORM_EPS) * qn_ref[...]
    q = _rope(q, cos, sin) * (HEAD_DIM ** -0.5 * LOG2E)
    q_ref[...] = q.astype(BF16)
    k = qk[:, ATTN_W:]
    k = k * lax.rsqrt(_head_ms(k, bdk_ref[...]) + NORM_EPS) * kn_ref[...]
    k = _rope(k, cos[:, :KV_W], sin[:, :KV_W])
    k_ref[...] = k.astype(BF16)
    vt_ref[0] = _dot_nt(wvt_ref[...], h).astype(BF16)

    gla = _dot(h, wgla_ref[...])
    lq_ref[...] = gla[:, :GLA_KW] * (GLA_DK ** -0.5)
    lk_ref[...] = gla[:, GLA_KW:2 * GLA_KW]
    lv_ref[...] = gla[:, 2 * GLA_KW:2 * GLA_KW + GLA_VW].astype(BF16)
    lr_ref[...] = gla[:, 2 * GLA_KW + GLA_VW:]

    a = _dot(h, wlr_ref[...]).astype(BF16)
    z = _dot(a, wgate_ref[...]) + bgate_ref[...]
    g = (jnp.minimum(z, 0.0) - jnp.log(1.0 + jnp.exp(-jnp.abs(z)))) * (1.0 / GLA_GATE_NORM)
    gf_ref[...] = g[:, :GLA_KW]
    gb_ref[...] = g[:, GLA_KW:]


def _in_proj(x2d, T, tabs, w, tm=512):
    n, d = x2d.shape
    nt = T // tm
    row = lambda wd: pl.BlockSpec((tm, wd), lambda i: (i, 0))
    tab = pl.BlockSpec((tm, ATTN_W), lambda i: (i % nt, 0))
    consts = [w["gmix"], w["wqk"], w["wvt"], w["wgla"], w["wlr"], w["wgate"], w["bgate"],
              w["qn"], w["kn"], w["bdq"], w["bdk"]]
    outs = [(ATTN_W, BF16), (KV_W, BF16), None, (GLA_KW, F32), (GLA_KW, F32), (GLA_VW, BF16),
            (GLA_VW, F32), (GLA_KW, F32), (GLA_KW, F32)]
    vt_spec = pl.BlockSpec((1, KV_W, tm), lambda i: (i // nt, 0, i % nt))
    vt_shape = jax.ShapeDtypeStruct((n // T, KV_W, T), BF16)
    return pl.pallas_call(
        _in_proj_kernel,
        grid=(n // tm,),
        in_specs=[row(d), tab, tab] + [_full_spec(c.shape) for c in consts],
        out_specs=[vt_spec if o is None else row(o[0]) for o in outs],
        out_shape=[vt_shape if o is None else jax.ShapeDtypeStruct((n, o[0]), o[1]) for o in outs],
        compiler_params=_cparams(("parallel",)),
        name="in_proj",
    )(x2d, tabs[0], tabs[1], *consts)


def _attn_kernel(q_ref, k_ref, vt_ref, o_ref, qpad_ref, m_ref, acc_ref, s_even, s_odd, *, nk):
    j = pl.program_id(2)
    group = N_Q_HEADS // N_KV_HEADS

    def head_scores(dst, hq):
        dst[hq] = _dot_nt(k_ref[0], qpad_ref[hq])

    def head_softmax(src, hq):
        m_old = m_ref[hq]
        m_new = jnp.maximum(m_old, jnp.max(src[hq], axis=0, keepdims=True))
        alpha = jnp.exp2(m_old - m_new)
        p = jnp.exp2(src[hq] - m_new)
        m_ref[hq] = m_new
        return p.astype(BF16), alpha

    def head_values(hq, p, alpha):
        hk = hq // group
        ones = jnp.ones((BF16_ROWS, vt_ref.shape[2]), BF16)
        vt = vt_ref[0, hk * HEAD_DIM:(hk + 1) * HEAD_DIM, :]
        vt1 = jnp.concatenate([vt, ones], axis=0)
        acc_ref[hq] = alpha * acc_ref[hq] + _dot(vt1, p)

    def scores_into(dst):
        for hq in range(N_Q_HEADS):
            head_scores(dst, hq)

    def consume(src, dst=None):
        pending = None
        for hq in range(N_Q_HEADS):
            if dst is not None:
                head_scores(dst, hq)
            p, alpha = head_softmax(src, hq)
            if pending is not None:
                head_values(*pending)
            pending = (hq, p, alpha)
        head_values(*pending)

    @pl.when(j == 0)
    def _():
        m_ref[...] = jnp.full(m_ref.shape, -jnp.inf, F32)
        acc_ref[...] = jnp.zeros(acc_ref.shape, F32)
        lane = lax.broadcasted_iota(I32, (q_ref.shape[1], LANES), 1)
        for hq in range(N_Q_HEADS):
            hk = hq // group
            chunk = q_ref[0, :, (hq // 2) * LANES:(hq // 2 + 1) * LANES].astype(F32)
            if hq % 2 != hk:
                chunk = pltpu.roll(chunk, HEAD_DIM, axis=1)
            keep = (lane // HEAD_DIM) == hk
            qpad_ref[hq] = jnp.where(keep, chunk, 0.0).astype(BF16)
        scores_into(s_even)

    @pl.when((j > 0) & (j < nk) & (lax.rem(j, 2) == 1))
    def _():
        consume(s_even, s_odd)

    @pl.when((j > 0) & (j < nk) & (lax.rem(j, 2) == 0))
    def _():
        consume(s_odd, s_even)

    @pl.when(j == nk)
    def _():
        consume(s_odd if nk % 2 == 0 else s_even)
        outs = [acc_ref[hq, :HEAD_DIM, :] / acc_ref[hq, HEAD_DIM:HEAD_DIM + 1, :] for hq in range(N_Q_HEADS)]
        o_ref[0] = jnp.concatenate(outs, axis=0).T.astype(BF16)


def _attention(q, k, vt, tq=512, tk=512):
    b, t, _ = q.shape
    nk = t // tk
    s_buf = pltpu.VMEM((N_Q_HEADS, tk, tq), F32)
    return pl.pallas_call(
        functools.partial(_attn_kernel, nk=nk),
        grid=(b, t // tq, nk + 1),
        in_specs=[pl.BlockSpec((1, tq, ATTN_W), lambda bi, i, j: (bi, i, 0)),
                  pl.BlockSpec((1, tk, KV_W), lambda bi, i, j: (bi, jnp.minimum(j, nk - 1), 0)),
                  pl.BlockSpec((1, KV_W, tk), lambda bi, i, j: (bi, 0, jnp.maximum(j - 1, 0)))],
        out_specs=pl.BlockSpec((1, tq, ATTN_W), lambda bi, i, j: (bi, i, 0)),
        out_shape=jax.ShapeDtypeStruct((b, t, ATTN_W), BF16),
        scratch_shapes=[pltpu.VMEM((N_Q_HEADS, tq, LANES), BF16),
                        pltpu.VMEM((N_Q_HEADS, 1, tq), F32),
                        pltpu.VMEM((N_Q_HEADS, HEAD_DIM + BF16_ROWS, tq), F32),
                        s_buf, s_buf],
        compiler_params=_cparams(("parallel", "parallel", "arbitrary")),
        name="attention",
    )(q, k, vt)


def _gla_chunk(q, k, v, g, st_ref, mask_ref, reverse):
    c = GLA_CHUNK
    pair_w = 2 * GLA_DK
    t_kw = lax.broadcasted_iota(I32, (c, GLA_KW), 0)
    lo_lanes = lax.broadcasted_iota(I32, (c, pair_w), 1) < GLA_DK

    a_pairs = [jnp.zeros((c, 2 * c), F32) for _ in range(2)]

    def add_level(a_pairs, qh, kh, mask2):
        qb = qh.astype(BF16)
        kb = kh.astype(BF16)
        out = []
        for p in range(2):
            qp = qb[:, p * pair_w:(p + 1) * pair_w]
            kp = kb[:, p * pair_w:(p + 1) * pair_w]
            k2 = jnp.concatenate([jnp.where(lo_lanes, kp, 0.0), jnp.where(lo_lanes, 0.0, kp)], axis=0)
            out.append(a_pairs[p] + mask2 * _dot_nt(qp, k2))
        return out

    pre = g
    tot = g
    for lvl in range(GLA_LEVELS):
        s = 1 << lvl
        odd = ((t_kw >> lvl) & 1) == 1
        if not reverse:
            qh = q * jnp.exp(pre)
            kh = k * jnp.exp(tot - pre)
        else:
            qh = q * jnp.exp(tot - pre + g)
            kh = k * jnp.exp(pre - g)
        a_pairs = add_level(a_pairs, qh, kh, mask_ref[lvl])
        sib = jnp.where(odd, pltpu.roll(tot, s, axis=0), pltpu.roll(tot, c - s, axis=0))
        pre = pre + jnp.where(odd, sib, 0.0)
        tot = tot + sib
    if not reverse:
        a_pairs = add_level(a_pairs, q, k, mask_ref[GLA_LEVELS])

    if not reverse:
        q_dec = q * jnp.exp(pre)
        k_dec = k * jnp.exp(tot - pre)
    else:
        q_dec = q * jnp.exp(tot - pre + g)
        k_dec = k * jnp.exp(pre - g)
    chunk_decay = jnp.exp(tot[0:1, :])
    sub_pair = lax.broadcasted_iota(I32, (2 * GLA_DV, pair_w), 0)
    lane_st = lax.broadcasted_iota(I32, (2 * GLA_DV, pair_w), 1)
    block_diag = (sub_pair // GLA_DV) == (lane_st // GLA_DK)
    first_head_v = lax.broadcasted_iota(I32, (c, 2 * GLA_DV), 1) < GLA_DV

    outs = []
    for p in range(2):
        st = st_ref[p]
        qd = q_dec[:, p * pair_w:(p + 1) * pair_w].astype(BF16)
        o_pair = _dot_nt(qd, st.astype(BF16))
        vp = v[:, p * 2 * GLA_DV:(p + 1) * 2 * GLA_DV].astype(BF16)
        kd = k_dec[:, p * pair_w:(p + 1) * pair_w].astype(BF16)
        upd = _dot_tn(vp, kd)
        st_ref[p] = st * chunk_decay[:, p * pair_w:(p + 1) * pair_w] + jnp.where(block_diag, upd, 0.0)
        v_diag = jnp.concatenate([jnp.where(first_head_v, vp, 0.0), jnp.where(first_head_v, 0.0, vp)], axis=0)
        outs.append(o_pair + _dot(a_pairs[p].astype(BF16), v_diag))
    return jnp.concatenate(outs, axis=1)


def _gla_level_masks(reverse):
    c = GLA_CHUNK
    t = np.arange(c)[:, None]
    s = np.arange(c)[None, :]
    masks = []
    for lvl in range(GLA_LEVELS):
        bt, bs = t >> lvl, s >> lvl
        masks.append((bs == bt + 1) & (bt % 2 == 0) if reverse else (bt == bs + 1) & (bs % 2 == 0))
    masks.append(t == s)
    m = np.stack(masks).astype(np.float32)
    return jnp.asarray(np.concatenate([m, m], axis=2))


def _gla_bwd_kernel(q_ref, k_ref, v_ref, g_ref, mask_ref, o_ref, st_ref):
    @pl.when(pl.program_id(1) == 0)
    def _():
        st_ref[...] = jnp.zeros(st_ref.shape, F32)

    o_ref[0] = _gla_chunk(q_ref[0], k_ref[0], v_ref[0], g_ref[0], st_ref, mask_ref, True)


def _gla_fwd_kernel(q_ref, k_ref, v_ref, g_ref, ob_ref, r_ref, gn_ref, mask_ref, o_ref, st_ref):
    @pl.when(pl.program_id(1) == 0)
    def _():
        st_ref[...] = jnp.zeros(st_ref.shape, F32)

    o = _gla_chunk(q_ref[0], k_ref[0], v_ref[0], g_ref[0], st_ref, mask_ref, False) + ob_ref[0]
    r = r_ref[0]
    gate = r * _sigmoid(r)
    gn = gn_ref[...]
    outs = []
    for h in range(GLA_HEADS):
        oh = o[:, h * GLA_DV:(h + 1) * GLA_DV]
        outs.append(_rms(oh, gn))
    o_ref[0] = (jnp.concatenate(outs, axis=1) * gate).astype(BF16)


def _gla(lq, lk, lv, lr, gf, gb, gla_norm):
    b, t, _ = lq.shape
    c = GLA_CHUNK
    nc = t // c
    fwd = lambda wd: pl.BlockSpec((1, c, wd), lambda bi, ci: (bi, ci, 0))
    rev = lambda wd: pl.BlockSpec((1, c, wd), lambda bi, ci: (bi, nc - 1 - ci, 0))
    st = pltpu.VMEM((2, 2 * GLA_DV, 2 * GLA_DK), F32)
    mask_b = _gla_level_masks(True)
    mask_f = _gla_level_masks(False)
    o_b = pl.pallas_call(
        _gla_bwd_kernel,
        grid=(b, nc),
        in_specs=[rev(GLA_KW), rev(GLA_KW), rev(GLA_VW), rev(GLA_KW), _full_spec(mask_b.shape)],
        out_specs=rev(GLA_VW),
        out_shape=jax.ShapeDtypeStruct((b, t, GLA_VW), F32),
        scratch_shapes=[st],
        compiler_params=_cparams(("parallel", "arbitrary")),
        name="gla_bwd",
    )(lq, lk, lv, gb, mask_b)
    return pl.pallas_call(
        _gla_fwd_kernel,
        grid=(b, nc),
        in_specs=[fwd(GLA_KW), fwd(GLA_KW), fwd(GLA_VW), fwd(GLA_KW), fwd(GLA_VW), fwd(GLA_VW),
                  _full_spec(gla_norm.shape), _full_spec(mask_f.shape)],
        out_specs=fwd(GLA_VW),
        out_shape=jax.ShapeDtypeStruct((b, t, GLA_VW), BF16),
        scratch_shapes=[st],
        compiler_params=_cparams(("parallel", "arbitrary")),
        name="gla_fwd",
    )(lq, lk, lv, gf, o_b, lr, gla_norm, mask_f)


def _post_mix_kernel(x_ref, oa_ref, og_ref, gmix_ref, wga_ref, wgl_ref, wau_ref, wgu_ref, wout_ref, gmoe_ref,
                     wr_hi_ref, wr_lo_ref, x1_ref, hm_ref, aff_ref):
    x = x_ref[...]
    h = _rms(x, gmix_ref[...]).astype(BF16)
    merged = (_sigmoid(_dot(h, wga_ref[...])) * _dot(oa_ref[...], wau_ref[...])
              + _sigmoid(_dot(h, wgl_ref[...])) * _dot(og_ref[...], wgu_ref[...]))
    x1 = x + _dot(merged.astype(BF16), wout_ref[...])
    x1_ref[...] = x1
    hm = _rms(x1, gmoe_ref[...])
    hm_ref[...] = hm
    h_hi, h_lo = _split2(hm)
    wr_hi = wr_hi_ref[...]
    logits = _dot_nt(wr_hi, h_hi) + _dot_nt(wr_hi, h_lo) + _dot_nt(wr_lo_ref[...], h_hi)
    m = jnp.max(logits, axis=0, keepdims=True)
    e = jnp.exp(logits - m)
    aff_ref[...] = e / jnp.sum(e, axis=0, keepdims=True)


def _post_mix(x2d, oa, og, w, tm=512):
    n, d = x2d.shape
    row = lambda wd: pl.BlockSpec((tm, wd), lambda i: (i, 0))
    consts = [w["gmix"], w["wga"], w["wgl"], w["wau"], w["wgu"], w["wout"], w["gmoe"], w["wr_hi"], w["wr_lo"]]
    return pl.pallas_call(
        _post_mix_kernel,
        grid=(n // tm,),
        in_specs=[row(d), row(ATTN_W), row(GLA_VW)] + [_full_spec(c.shape) for c in consts],
        out_specs=[row(d), row(d), pl.BlockSpec((N_EXPERTS, tm), lambda i: (0, i))],
        out_shape=[jax.ShapeDtypeStruct((n, d), F32), jax.ShapeDtypeStruct((n, d), F32),
                   jax.ShapeDtypeStruct((N_EXPERTS, n), F32)],
        compiler_params=_cparams(("parallel",)),
        name="post_mix",
    )(x2d, oa, og, *consts)


def _route_kernel(a_ref, idx_ref, gate_ref, pos_ref, lo_ref, *, cap):
    a = a_ref[0]
    nrow = a.shape[0]
    b_io = lax.broadcasted_iota(I32, (nrow, LANES), 0)
    t_io = lax.broadcasted_iota(I32, (nrow, LANES), 1)
    tok = b_io * LANES + t_io

    def as_float(word):
        return lax.bitcast_convert_type(jnp.full((1, LANES), word, I32), F32)

    def thr_step(i, thr):
        cand = thr | lax.shift_left(jnp.int32(1), jnp.int32(30) - i)
        cnt = jnp.sum((a >= as_float(cand)).astype(I32))
        return jnp.where(cnt >= cap, cand, thr)

    thr = lax.fori_loop(0, 31, thr_step, jnp.int32(0))
    gt = a >= as_float(thr + 1)
    eq = (a >= as_float(thr)) & jnp.logical_not(gt)
    need = cap - jnp.sum(gt.astype(I32))

    def tie_step(i, lim):
        cand = lim + lax.shift_left(jnp.int32(1), jnp.int32(17) - i)
        cnt = jnp.sum((eq & (tok < cand)).astype(I32))
        return jnp.where(cnt < need, cand, lim)

    lim = lax.fori_loop(0, 18, tie_step, jnp.int32(0))
    selm = gt | (eq & (tok <= lim))
    selb = selm.astype(F32).astype(BF16)

    r_io = lax.broadcasted_iota(I32, (LANES, LANES), 0)
    c_io = lax.broadcasted_iota(I32, (LANES, LANES), 1)
    utri = (r_io <= c_io).astype(BF16)
    cum_row = _dot(selb, utri)
    row_cnt = cum_row[:, LANES - 1:LANES]
    br_io = lax.broadcasted_iota(I32, (nrow, nrow), 0)
    bc_io = lax.broadcasted_iota(I32, (nrow, nrow), 1)
    lrow = (bc_io <= br_io).astype(BF16)
    row_incl = _dot(lrow, jnp.broadcast_to(row_cnt, (nrow, LANES)).astype(BF16))
    row_excl = row_incl - row_cnt
    pos_ref[0] = jnp.where(selm, row_excl + cum_row - 1.0, -1.0).astype(I32)
    lo_ref[0] = row_excl.astype(I32)
    row_incl_c = row_incl[:, 0:1]
    row_excl_c = row_excl[:, 0:1]
    cum_t = cum_row.T.astype(BF16)
    a_hi, a_mid, a_lo = _split3(a.T)
    brow = lax.broadcasted_iota(I32, (nrow, LANES), 0).astype(F32)
    tsub = lax.broadcasted_iota(I32, (LANES, LANES), 0).astype(F32)
    lane = lax.broadcasted_iota(I32, (1, LANES), 1).astype(F32)

    def slot_tile(jt, carry):
        j = lane + jnp.asarray(jt, F32) * float(LANES)
        bj = jnp.sum((row_incl_c <= j).astype(F32), axis=0, keepdims=True)
        onehot = (brow == bj).astype(F32)
        base = jnp.sum(onehot * row_excl_c, axis=0, keepdims=True)
        jl = j - base
        ohb = onehot.astype(BF16)
        rt = _dot(cum_t, ohb)
        tj = jnp.sum((rt <= jl).astype(F32), axis=0, keepdims=True)
        acol = _dot(a_hi, ohb) + _dot(a_mid, ohb) + _dot(a_lo, ohb)
        gate = jnp.sum(jnp.where(tsub == tj, acol, 0.0), axis=0, keepdims=True)
        idx_ref[0, pl.ds(jt, 1), :] = (bj * float(LANES) + tj).astype(I32)
        gate_ref[0, pl.ds(jt, 1), :] = gate
        return carry

    lax.fori_loop(0, cap // LANES, slot_tile, 0)


def _route(aff_t, cap):
    e, n = aff_t.shape
    nrow = n // LANES
    a3 = aff_t.reshape(e, nrow, LANES)
    nt = cap // LANES
    slot_spec = pl.BlockSpec((1, nt, LANES), lambda i: (i, 0, 0))
    tok_spec = pl.BlockSpec((1, nrow, LANES), lambda i: (i, 0, 0))
    idx, gate, pos, lo = pl.pallas_call(
        functools.partial(_route_kernel, cap=cap),
        grid=(e,),
        in_specs=[tok_spec],
        out_specs=[slot_spec, slot_spec, tok_spec, tok_spec],
        out_shape=[jax.ShapeDtypeStruct((e, nt, LANES), I32), jax.ShapeDtypeStruct((e, nt, LANES), F32),
                   jax.ShapeDtypeStruct((e, nrow, LANES), I32), jax.ShapeDtypeStruct((e, nrow, LANES), I32)],
        compiler_params=_cparams(("parallel",)),
        name="route",
    )(a3)
    lo_tab = jnp.concatenate([lo[:, :, 0], jnp.full((e, 1), cap, I32)], axis=1)
    return idx, gate, pos.reshape(e, n), lo_tab


MOE_ISSUE_UNROLL = 8


def _moe_kernel(idx_ref, idxn_ref, gate_ref, hm_hbm, wg_ref, wu_ref, wd_ref, ye_ref, buf_a, buf_b, sem, *, rows):
    nblk = pl.num_programs(1)
    s = pl.program_id(0) * nblk + pl.program_id(1)
    nsteps = pl.num_programs(0) * nblk

    def buffer_wait(buf, k):
        pltpu.make_async_copy(hm_hbm.at[pl.ds(0, rows)], buf, sem.at[k]).wait()

    @pl.when(s == 0)
    def _():
        def body(r0, c):
            for u in range(MOE_ISSUE_UNROLL):
                r = r0 * MOE_ISSUE_UNROLL + u
                n = idx_ref[0, 0, r]
                pltpu.make_async_copy(hm_hbm.at[pl.ds(n, 1)], buf_a.at[pl.ds(r, 1)], sem.at[0]).start()
            return c
        lax.fori_loop(0, rows // MOE_ISSUE_UNROLL, body, 0)

    def step(cur, k_cur, nxt, k_nxt):
        buffer_wait(cur, k_cur)

        @pl.when(s < nsteps)
        def _():
            for r in range(rows):
                n = idxn_ref[0, 0, r]
                pltpu.make_async_copy(hm_hbm.at[pl.ds(n, 1)], nxt.at[pl.ds(r, 1)], sem.at[k_nxt]).start()

        xe = cur[...].astype(BF16)
        hid_g = _dot(xe, wg_ref[0])
        hid = (hid_g * _sigmoid(hid_g)) * _dot(xe, wu_ref[0])
        y = _dot(hid.astype(BF16), wd_ref[0])
        eye = (lax.broadcasted_iota(I32, (LANES, LANES), 0) == lax.broadcasted_iota(I32, (LANES, LANES), 1))
        for c in range(rows // LANES):
            g_row = gate_ref[0, :, c * LANES:(c + 1) * LANES]
            g_col = jnp.sum(jnp.where(eye, g_row, 0.0), axis=1, keepdims=True)
            sl = slice(c * LANES, (c + 1) * LANES)
            ye_ref[sl, :] = (y[sl, :] * g_col).astype(BF16)

        @pl.when(s == nsteps - 1)
        def _():
            buffer_wait(nxt, k_nxt)

    @pl.when(lax.rem(s, 2) == 0)
    def _():
        step(buf_a, 0, buf_b, 1)

    @pl.when(lax.rem(s, 2) == 1)
    def _():
        step(buf_b, 1, buf_a, 0)


def _moe_ffn(idx, gate, hm, wg, wu, wd, rows=256):
    e, nt, _ = idx.shape
    cap = nt * LANES
    n, d = hm.shape
    nblk = cap // rows
    nsteps = e * nblk
    idx3 = idx.reshape(nsteps, 1, rows)
    gate3 = gate.reshape(nsteps, 1, rows)
    ff = wg.shape[-1]
    return pl.pallas_call(
        functools.partial(_moe_kernel, rows=rows),
        grid=(e, nblk),
        in_specs=[pl.BlockSpec((1, 1, rows), lambda ei, j: (ei * nblk + j, 0, 0), memory_space=pltpu.SMEM),
                  pl.BlockSpec((1, 1, rows), lambda ei, j: (jnp.minimum(ei * nblk + j + 1, nsteps - 1), 0, 0),
                               memory_space=pltpu.SMEM),
                  pl.BlockSpec((1, 1, rows), lambda ei, j: (ei * nblk + j, 0, 0)),
                  pl.BlockSpec(memory_space=pl.ANY),
                  pl.BlockSpec((1, d, ff), lambda ei, j: (ei, 0, 0)),
                  pl.BlockSpec((1, d, ff), lambda ei, j: (ei, 0, 0)),
                  pl.BlockSpec((1, ff, d), lambda ei, j: (ei, 0, 0))],
        out_specs=pl.BlockSpec((rows, d), lambda ei, j: (ei * nblk + j, 0)),
        out_shape=jax.ShapeDtypeStruct((e * cap, d), BF16),
        scratch_shapes=[pltpu.VMEM((rows, d), F32), pltpu.VMEM((rows, d), F32), pltpu.SemaphoreType.DMA((2,))],
        compiler_params=_cparams(("arbitrary", "arbitrary")),
        name="moe_ffn",
    )(idx3, idx3, gate3, hm, wg, wu, wd)


COMBINE_WIN = 64


def _ple_final_kernel(lo_ref, x_ref, p_ref, pos_ref, ye_hbm, gple_ref, wpg_ref, wpp_ref, gfin_ref, o_ref,
                      win_a, win_b, acc_ref, sem, *, cap, rows_per_tile):
    i = pl.program_id(0)
    ntiles = pl.num_programs(0)
    win = COMBINE_WIN
    tm = x_ref.shape[0]

    def tile_floors(ti):
        return [jnp.bitwise_and(lo_ref[e, ti * rows_per_tile], -BF16_ROWS) for e in range(N_EXPERTS)]

    def clamp(floors):
        return [jnp.minimum(f, cap - win) for f in floors]

    def start_fetch(floors, buf, k):
        starts = clamp(floors)
        for e in range(N_EXPERTS):
            src = ye_hbm.at[pl.ds(pl.multiple_of(e * cap + starts[e], BF16_ROWS), win)]
            pltpu.make_async_copy(src, buf.at[pl.ds(e * win, win)], sem.at[k]).start()

    def wait_fetch(buf, k):
        pltpu.make_async_copy(ye_hbm.at[pl.ds(0, N_EXPERTS * win)], buf, sem.at[k]).wait()

    def expand(floors, buf):
        starts = clamp(floors)
        pos = pos_ref[...]
        w_io = lax.broadcasted_iota(I32, (win, tm), 0)
        parts = []
        for e in range(N_EXPERTS):
            pe = pos[e:e + 1, :]
            hit = ((pe - starts[e]) == w_io) & (pe >= floors[e])
            parts.append(jnp.where(hit, 1.0, 0.0).astype(BF16))
        onehot_t = jnp.concatenate(parts, axis=0)
        return _dot_tn(onehot_t, buf[...])

    @pl.when(i == 0)
    def _():
        start_fetch(tile_floors(0), win_a, 0)

    def step(cur, k_cur, nxt, k_nxt):
        floors0 = tile_floors(i)
        wait_fetch(cur, k_cur)

        @pl.when(i + 1 < ntiles)
        def _():
            start_fetch(tile_floors(i + 1), nxt, k_nxt)

        acc_ref[...] = x_ref[...] + expand(floors0, cur)

        nrounds = jnp.int32(1)
        for e in range(N_EXPERTS):
            hi = lo_ref[e, (i + 1) * rows_per_tile]
            nrounds = jnp.maximum(nrounds, lax.div(hi - floors0[e] + (win - 1), jnp.int32(win)))

        def extra_round(r, c):
            floors = [f + r * win for f in floors0]
            start_fetch(floors, cur, k_cur)
            wait_fetch(cur, k_cur)
            acc_ref[...] += expand(floors, cur)
            return c

        lax.fori_loop(1, nrounds, extra_round, 0)

        x = acc_ref[...]
        hp = _rms(x, gple_ref[...]).astype(BF16)
        gate = _sigmoid(_dot(hp, wpg_ref[...]))
        y = x + gate * _dot(p_ref[...].astype(BF16), wpp_ref[...])
        o_ref[...] = _rms(y, gfin_ref[...])

    @pl.when(lax.rem(i, 2) == 0)
    def _():
        step(win_a, 0, win_b, 1)

    @pl.when(lax.rem(i, 2) == 1)
    def _():
        step(win_b, 1, win_a, 0)


def _ple_final(x1, p2d, pos, lo_tab, ye, cap, w, tm=256):
    n, d = x1.shape
    pd = p2d.shape[-1]
    row = lambda wd: pl.BlockSpec((tm, wd), lambda i, lo: (i, 0))
    consts = [w["gple"], w["wpg"], w["wpp"], w["gfin"]]
    const_spec = lambda c: pl.BlockSpec(c.shape, lambda i, lo: (0,) * c.ndim)
    grid_spec = pltpu.PrefetchScalarGridSpec(
        num_scalar_prefetch=1,
        grid=(n // tm,),
        in_specs=[row(d), row(pd), pl.BlockSpec((N_EXPERTS, tm), lambda i, lo: (0, i)),
                  pl.BlockSpec(memory_space=pl.ANY)] + [const_spec(c) for c in consts],
        out_specs=row(d),
        scratch_shapes=[pltpu.VMEM((N_EXPERTS * COMBINE_WIN, d), BF16),
                        pltpu.VMEM((N_EXPERTS * COMBINE_WIN, d), BF16), pltpu.VMEM((tm, d), F32),
                        pltpu.SemaphoreType.DMA((2,))],
    )
    return pl.pallas_call(
        functools.partial(_ple_final_kernel, cap=cap, rows_per_tile=tm // LANES),
        grid_spec=grid_spec,
        out_shape=jax.ShapeDtypeStruct((n, d), F32),
        compiler_params=_cparams(("arbitrary",)),
        name="ple_final",
    )(lo_tab, x1, p2d, pos, ye, *consts)


def _rope_tables(T):
    half = HEAD_DIM // 2
    inv = ROPE_THETA ** (-jnp.arange(0, half, 2, dtype=F32) / half)
    t = jnp.arange(T)
    ang_r = (t // GRID_W).astype(F32)[:, None] * inv
    ang_c = (t % GRID_W).astype(F32)[:, None] * inv
    cos = jnp.concatenate([jnp.cos(ang_r)] * 2 + [jnp.cos(ang_c)] * 2, axis=1)
    sin = jnp.concatenate([-jnp.sin(ang_r), jnp.sin(ang_r), -jnp.sin(ang_c), jnp.sin(ang_c)], axis=1)
    return jnp.tile(cos, (1, N_Q_HEADS)), jnp.tile(sin, (1, N_Q_HEADS))


def _block_diag_ones(width):
    i = np.arange(width)
    return jnp.asarray((i[:, None] // HEAD_DIM) == (i[None, :] // HEAD_DIM), BF16)


def _prep_weights(g_mix, w_in, q_norm, k_norm, w_af, b_af, w_ab, b_ab, gla_norm, w_attn_up, w_gla_up,
                  w_out, g_moe, w_router, w_gate_e, w_up_e, w_down_e, g_ple, w_ple_gate, w_ple_proj,
                  g_final):
    d = w_in.shape[0]
    o_gla = ATTN_W + 2 * KV_W
    o_lr = o_gla + 2 * GLA_KW + 2 * GLA_VW
    o_ga = o_lr + 2 * GLA_LOWRANK
    wlr = jnp.zeros((d, LANES), F32).at[:, :2 * GLA_LOWRANK].set(w_in[:, o_lr:o_ga])
    wgate = jnp.zeros((LANES, 2 * GLA_KW), F32)
    wgate = wgate.at[:GLA_LOWRANK, :GLA_KW].set(w_af)
    wgate = wgate.at[GLA_LOWRANK:2 * GLA_LOWRANK, GLA_KW:].set(w_ab)
    wr_t = w_router.T
    wr_hi = wr_t.astype(BF16)
    wr_lo = (wr_t - wr_hi.astype(F32)).astype(BF16)
    return dict(
        gmix=g_mix.reshape(1, d),
        wqk=w_in[:, :ATTN_W + KV_W].astype(BF16),
        wvt=w_in[:, ATTN_W + KV_W:o_gla].T.astype(BF16),
        wgla=w_in[:, o_gla:o_lr].astype(BF16),
        wlr=wlr.astype(BF16),
        wga=w_in[:, o_ga:o_ga + d].astype(BF16),
        wgl=w_in[:, o_ga + d:].astype(BF16),
        wgate=wgate.astype(BF16),
        bgate=jnp.concatenate([b_af, b_ab]).reshape(1, 2 * GLA_KW),
        qn=jnp.tile(q_norm, N_Q_HEADS).reshape(1, ATTN_W),
        kn=jnp.tile(k_norm, N_KV_HEADS).reshape(1, KV_W),
        bdq=_block_diag_ones(ATTN_W),
        bdk=_block_diag_ones(KV_W),
        gnorm=gla_norm.reshape(1, GLA_DV),
        wau=w_attn_up.astype(BF16),
        wgu=w_gla_up.astype(BF16),
        wout=w_out.astype(BF16),
        gmoe=g_moe.reshape(1, d),
        wr_hi=wr_hi,
        wr_lo=wr_lo,
        wge=w_gate_e.astype(BF16),
        wue=w_up_e.astype(BF16),
        wde=w_down_e.astype(BF16),
        gple=g_ple.reshape(1, d),
        wpg=w_ple_gate.astype(BF16),
        wpp=w_ple_proj.astype(BF16),
        gfin=g_final.reshape(1, d),
    )


def _trunk(x, p, w):
    b, t, d = x.shape
    n = b * t
    x2d = x.reshape(n, d)
    q, k, vt, lq, lk, lv, lr, gf, gb = _in_proj(x2d, t, _rope_tables(t), w)
    r3 = lambda a: a.reshape(b, t, a.shape[-1])
    o_attn = _attention(r3(q), r3(k), vt).reshape(n, ATTN_W)
    o_gla = _gla(r3(lq), r3(lk), r3(lv), r3(lr), r3(gf), r3(gb), w["gnorm"]).reshape(n, GLA_VW)
    x1, hm, aff_t = _post_mix(x2d, o_attn, o_gla, w)
    cap = EC_CAPACITY * n // N_EXPERTS
    idx, gate, pos, lo_tab = _route(aff_t, cap)
    ye = _moe_ffn(idx, gate, hm, w["wge"], w["wue"], w["wde"])
    y = _ple_final(x1, p.reshape(n, p.shape[-1]), pos, lo_tab, ye, cap, w)
    return y.reshape(b, t, d)


def kernel(x_prompt, x_sample, p_prompt, p_sample, g_mix, w_in, q_norm, k_norm, w_af, b_af, w_ab, b_ab,
           gla_norm, w_attn_up, w_gla_up, w_out, g_moe, w_router, w_gate_e, w_up_e, w_down_e, g_ple,
           w_ple_gate, w_ple_proj, g_final):
    assert g_mix.shape[0] == 1, "single layer"
    w = _prep_weights(g_mix[0], w_in[0], q_norm[0], k_norm[0], w_af[0], b_af[0], w_ab[0], b_ab[0],
                      gla_norm[0], w_attn_up[0], w_gla_up[0], w_out[0], g_moe[0], w_router[0],
                      w_gate_e[0], w_up_e[0], w_down_e[0], g_ple[0], w_ple_gate[0], w_ple_proj[0], g_final)
    return (_trunk(x_prompt, p_prompt[0], w), _trunk(x_sample, p_sample[0], w))
```

```python
import functools

import numpy as np
import jax
import jax.numpy as jnp
from jax import lax
from jax.experimental import pallas as pl
from jax.experimental.pallas import tpu as pltpu

F32 = jnp.float32
BF16 = jnp.bfloat16
I32 = jnp.int32

HEAD_DIM = 64
N_Q_HEADS = 8
N_KV_HEADS = 2
GRID_W = 64
ROPE_THETA = 10000.0
GLA_HEADS = 4
GLA_DK = 64
GLA_DV = 128
GLA_LOWRANK = 16
GLA_GATE_NORM = 16.0
N_EXPERTS = 16
EC_CAPACITY = 2
NORM_EPS = 1e-6
LOG2E = 1.4426950408889634

ATTN_W = N_Q_HEADS * HEAD_DIM
KV_W = N_KV_HEADS * HEAD_DIM
GLA_KW = GLA_HEADS * GLA_DK
GLA_VW = GLA_HEADS * GLA_DV

LANES = 128
SUBLANES = 8
BF16_ROWS = 16
VMEM_LIMIT = 56 * 1024 * 1024

GLA_CHUNK = 128
GLA_LEVELS = 7


def _cparams(sem):
    return pltpu.CompilerParams(dimension_semantics=sem, vmem_limit_bytes=VMEM_LIMIT)


def _dot(a, b):
    return jnp.dot(a, b, preferred_element_type=F32)


def _dot_nt(a, b):
    return lax.dot_general(a, b, (((1,), (1,)), ((), ())), preferred_element_type=F32)


def _dot_tn(a, b):
    return lax.dot_general(a, b, (((0,), (0,)), ((), ())), preferred_element_type=F32)


def _split2(x):
    hi = x.astype(BF16)
    lo = (x - hi.astype(F32)).astype(BF16)
    return hi, lo


def _split3(x):
    hi = x.astype(BF16)
    r = x - hi.astype(F32)
    mid = r.astype(BF16)
    lo = (r - mid.astype(F32)).astype(BF16)
    return hi, mid, lo


def _rms(x, gain):
    ms = jnp.mean(x * x, axis=-1, keepdims=True)
    return x * lax.rsqrt(ms + NORM_EPS) * gain


def _sigmoid(x):
    return 1.0 / (1.0 + jnp.exp(-x))


def _full_spec(shape):
    nd = len(shape)
    return pl.BlockSpec(shape, lambda *_: (0,) * nd)


def _head_ms(x, bd):
    hi, lo = _split2(x * x)
    return (_dot(hi, bd) + _dot(lo, bd)) * (1.0 / HEAD_DIM)


def _rope(x, cos, sin):
    w = x.shape[-1]
    lane = lax.broadcasted_iota(I32, x.shape, 1)
    fwd = pltpu.roll(x, w - 16, axis=1)
    bwd = pltpu.roll(x, 16, axis=1)
    partner = jnp.where((lane % 32) < 16, fwd, bwd)
    return x * cos + partner * sin


def _in_proj_kernel(x_ref, cos_ref, sin_ref, gmix_ref, wqk_ref, wvt_ref, wgla_ref, wlr_ref,
                    wgate_ref, bgate_ref, qn_ref, kn_ref, bdq_ref, bdk_ref,
                    q_ref, k_ref, vt_ref, lq_ref, lk_ref, lv_ref, lr_ref, gf_ref, gb_ref):
    x = x_ref[...]
    h = _rms(x, gmix_ref[...]).astype(BF16)

    qk = _dot(h, wqk_ref[...])
    cos = cos_ref[...]
    sin = sin_ref[...]
    q = qk[:, :ATTN_W]
    q = q * lax.rsqrt(_head_ms(q, bdq_ref[...]) + NORM_EPS) * qn_ref[...]
    q = _rope(q, cos, sin) * (HEAD_DIM ** -0.5 * LOG2E)
    q_ref[...] = q.astype(BF16)
    k = qk[:, ATTN_W:]
    k = k * lax.rsqrt(_head_ms(k, bdk_ref[...]) + NORM_EPS) * kn_ref[...]
    k = _rope(k, cos[:, :KV_W], sin[:, :KV_W])
    k_ref[...] = k.astype(BF16)
    vt_ref[0] = _dot_nt(wvt_ref[...], h).astype(BF16)

    gla = _dot(h, wgla_ref[...])
    lq_ref[...] = gla[:, :GLA_KW] * (GLA_DK ** -0.5)
    lk_ref[...] = gla[:, GLA_KW:2 * GLA_KW]
    lv_ref[...] = gla[:, 2 * GLA_KW:2 * GLA_KW + GLA_VW].astype(BF16)
    lr_ref[...] = gla[:, 2 * GLA_KW + GLA_VW:]

    a = _dot(h, wlr_ref[...]).astype(BF16)
    z = _dot(a, wgate_ref[...]) + bgate_ref[...]
    g = (jnp.minimum(z, 0.0) - jnp.log(1.0 + jnp.exp(-jnp.abs(z)))) * (1.0 / GLA_GATE_NORM)
    gf_ref[...] = g[:, :GLA_KW]
    gb_ref[...] = g[:, GLA_KW:]


def _in_proj(x2d, T, tabs, w, tm=512):
    n, d = x2d.shape
    nt = T // tm
    row = lambda wd: pl.BlockSpec((tm, wd), lambda i: (i, 0))
    tab = pl.BlockSpec((tm, ATTN_W), lambda i: (i % nt, 0))
    consts = [w["gmix"], w["wqk"], w["wvt"], w["wgla"], w["wlr"], w["wgate"], w["bgate"],
              w["qn"], w["kn"], w["bdq"], w["bdk"]]
    outs = [(ATTN_W, BF16), (KV_W, BF16), None, (GLA_KW, F32), (GLA_KW, F32), (GLA_VW, BF16),
            (GLA_VW, F32), (GLA_KW, F32), (GLA_KW, F32)]
    vt_spec = pl.BlockSpec((1, KV_W, tm), lambda i: (i // nt, 0, i % nt))
    vt_shape = jax.ShapeDtypeStruct((n // T, KV_W, T), BF16)
    return pl.pallas_call(
        _in_proj_kernel,
        grid=(n // tm,),
        in_specs=[row(d), tab, tab] + [_full_spec(c.shape) for c in consts],
        out_specs=[vt_spec if o is None else row(o[0]) for o in outs],
        out_shape=[vt_shape if o is None else jax.ShapeDtypeStruct((n, o[0]), o[1]) for o in outs],
        compiler_params=_cparams(("parallel",)),
        name="in_proj",
    )(x2d, tabs[0], tabs[1], *consts)


def _attn_kernel(q_ref, k_ref, vt_ref, o_ref, qpad_ref, m_ref, acc_ref, s_even, s_odd, mx_even, mx_odd, *, nk):
    j = pl.program_id(2)
    group = N_Q_HEADS // N_KV_HEADS
    even = (s_even, mx_even)
    odd = (s_odd, mx_odd)

    def head_scores(dst, hq):
        s = _dot_nt(k_ref[0], qpad_ref[hq])
        dst[0][hq] = s
        dst[1][hq] = jnp.max(s, axis=0, keepdims=True)

    def head_softmax(src, hq):
        m_old = m_ref[hq]
        m_new = jnp.maximum(m_old, src[1][hq])
        alpha = jnp.exp2(m_old - m_new)
        p = jnp.exp2(src[0][hq] - m_new)
        m_ref[hq] = m_new
        return p.astype(BF16), alpha

    def head_values(hq, p, alpha):
        hk = hq // group
        ones = jnp.ones((BF16_ROWS, vt_ref.shape[2]), BF16)
        vt = vt_ref[0, hk * HEAD_DIM:(hk + 1) * HEAD_DIM, :]
        vt1 = jnp.concatenate([vt, ones], axis=0)
        acc_ref[hq] = alpha * acc_ref[hq] + _dot(vt1, p)

    def scores_into(dst):
        for hq in range(N_Q_HEADS):
            head_scores(dst, hq)

    def consume(src, dst=None):
        pending = None
        for hq in range(N_Q_HEADS):
            if dst is not None:
                head_scores(dst, hq)
            p, alpha = head_softmax(src, hq)
            if pending is not None:
                head_values(*pending)
            pending = (hq, p, alpha)
        head_values(*pending)

    @pl.when(j == 0)
    def _():
        m_ref[...] = jnp.full(m_ref.shape, -jnp.inf, F32)
        acc_ref[...] = jnp.zeros(acc_ref.shape, F32)
        lane = lax.broadcasted_iota(I32, (q_ref.shape[1], LANES), 1)
        for hq in range(N_Q_HEADS):
            hk = hq // group
            chunk = q_ref[0, :, (hq // 2) * LANES:(hq // 2 + 1) * LANES].astype(F32)
            if hq % 2 != hk:
                chunk = pltpu.roll(chunk, HEAD_DIM, axis=1)
            keep = (lane // HEAD_DIM) == hk
            qpad_ref[hq] = jnp.where(keep, chunk, 0.0).astype(BF16)
        scores_into(even)

    @pl.when((j > 0) & (j < nk) & (lax.rem(j, 2) == 1))
    def _():
        consume(even, odd)

    @pl.when((j > 0) & (j < nk) & (lax.rem(j, 2) == 0))
    def _():
        consume(odd, even)

    @pl.when(j == nk)
    def _():
        consume(odd if nk % 2 == 0 else even)
        outs = [acc_ref[hq, :HEAD_DIM, :] / acc_ref[hq, HEAD_DIM:HEAD_DIM + 1, :] for hq in range(N_Q_HEADS)]
        o_ref[0] = jnp.concatenate(outs, axis=0).T.astype(BF16)


def _attention(q, k, vt, tq=512, tk=512):
    b, t, _ = q.shape
    nk = t // tk
    s_buf = pltpu.VMEM((N_Q_HEADS, tk, tq), F32)
    mx_buf = pltpu.VMEM((N_Q_HEADS, 1, tq), F32)
    return pl.pallas_call(
        functools.partial(_attn_kernel, nk=nk),
        grid=(b, t // tq, nk + 1),
        in_specs=[pl.BlockSpec((1, tq, ATTN_W), lambda bi, i, j: (bi, i, 0)),
                  pl.BlockSpec((1, tk, KV_W), lambda bi, i, j: (bi, jnp.minimum(j, nk - 1), 0)),
                  pl.BlockSpec((1, KV_W, tk), lambda bi, i, j: (bi, 0, jnp.maximum(j - 1, 0)))],
        out_specs=pl.BlockSpec((1, tq, ATTN_W), lambda bi, i, j: (bi, i, 0)),
        out_shape=jax.ShapeDtypeStruct((b, t, ATTN_W), BF16),
        scratch_shapes=[pltpu.VMEM((N_Q_HEADS, tq, LANES), BF16),
                        pltpu.VMEM((N_Q_HEADS, 1, tq), F32),
                        pltpu.VMEM((N_Q_HEADS, HEAD_DIM + BF16_ROWS, tq), F32),
                        s_buf, s_buf, mx_buf, mx_buf],
        compiler_params=_cparams(("parallel", "parallel", "arbitrary")),
        name="attention",
    )(q, k, vt)


def _gla_chunk(q, k, v, g, st_ref, mask_ref, reverse):
    c = GLA_CHUNK
    pair_w = 2 * GLA_DK
    t_kw = lax.broadcasted_iota(I32, (c, GLA_KW), 0)
    lo_lanes = lax.broadcasted_iota(I32, (c, pair_w), 1) < GLA_DK

    a_pairs = [jnp.zeros((c, 2 * c), F32) for _ in range(2)]

    def add_level(a_pairs, qh, kh, mask2):
        qb = qh.astype(BF16)
        kb = kh.astype(BF16)
        out = []
        for p in range(2):
            qp = qb[:, p * pair_w:(p + 1) * pair_w]
            kp = kb[:, p * pair_w:(p + 1) * pair_w]
            k2 = jnp.concatenate([jnp.where(lo_lanes, kp, 0.0), jnp.where(lo_lanes, 0.0, kp)], axis=0)
            out.append(a_pairs[p] + mask2 * _dot_nt(qp, k2))
        return out

    pre = g
    tot = g
    for lvl in range(GLA_LEVELS):
        s = 1 << lvl
        odd = ((t_kw >> lvl) & 1) == 1
        if not reverse:
            qh = q * jnp.exp(pre)
            kh = k * jnp.exp(tot - pre)
        else:
            qh = q * jnp.exp(tot - pre + g)
            kh = k * jnp.exp(pre - g)
        a_pairs = add_level(a_pairs, qh, kh, mask_ref[lvl])
        sib = jnp.where(odd, pltpu.roll(tot, s, axis=0), pltpu.roll(tot, c - s, axis=0))
        pre = pre + jnp.where(odd, sib, 0.0)
        tot = tot + sib
    if not reverse:
        a_pairs = add_level(a_pairs, q, k, mask_ref[GLA_LEVELS])

    if not reverse:
        q_dec = q * jnp.exp(pre)
        k_dec = k * jnp.exp(tot - pre)
    else:
        q_dec = q * jnp.exp(tot - pre + g)
        k_dec = k * jnp.exp(pre - g)
    chunk_decay = jnp.exp(tot[0:1, :])
    sub_pair = lax.broadcasted_iota(I32, (2 * GLA_DV, pair_w), 0)
    lane_st = lax.broadcasted_iota(I32, (2 * GLA_DV, pair_w), 1)
    block_diag = (sub_pair // GLA_DV) == (lane_st // GLA_DK)
    first_head_v = lax.broadcasted_iota(I32, (c, 2 * GLA_DV), 1) < GLA_DV

    outs = []
    for p in range(2):
        st = st_ref[p]
        qd = q_dec[:, p * pair_w:(p + 1) * pair_w].astype(BF16)
        o_pair = _dot_nt(qd, st.astype(BF16))
        vp = v[:, p * 2 * GLA_DV:(p + 1) * 2 * GLA_DV].astype(BF16)
        kd = k_dec[:, p * pair_w:(p + 1) * pair_w].astype(BF16)
        upd = _dot_tn(vp, kd)
        st_ref[p] = st * chunk_decay[:, p * pair_w:(p + 1) * pair_w] + jnp.where(block_diag, upd, 0.0)
        v_diag = jnp.concatenate([jnp.where(first_head_v, vp, 0.0), jnp.where(first_head_v, 0.0, vp)], axis=0)
        outs.append(o_pair + _dot(a_pairs[p].astype(BF16), v_diag))
    return jnp.concatenate(outs, axis=1)


def _gla_level_masks(reverse):
    c = GLA_CHUNK
    t = np.arange(c)[:, None]
    s = np.arange(c)[None, :]
    masks = []
    for lvl in range(GLA_LEVELS):
        bt, bs = t >> lvl, s >> lvl
        masks.append((bs == bt + 1) & (bt % 2 == 0) if reverse else (bt == bs + 1) & (bs % 2 == 0))
    masks.append(t == s)
    m = np.stack(masks).astype(np.float32)
    return jnp.asarray(np.concatenate([m, m], axis=2))


def _gla_bwd_kernel(q_ref, k_ref, v_ref, g_ref, mask_ref, o_ref, st_ref):
    @pl.when(pl.program_id(1) == 0)
    def _():
        st_ref[...] = jnp.zeros(st_ref.shape, F32)

    o_ref[0] = _gla_chunk(q_ref[0], k_ref[0], v_ref[0], g_ref[0], st_ref, mask_ref, True)


def _gla_fwd_kernel(q_ref, k_ref, v_ref, g_ref, ob_ref, r_ref, gn_ref, mask_ref, o_ref, st_ref):
    @pl.when(pl.program_id(1) == 0)
    def _():
        st_ref[...] = jnp.zeros(st_ref.shape, F32)

    o = _gla_chunk(q_ref[0], k_ref[0], v_ref[0], g_ref[0], st_ref, mask_ref, False) + ob_ref[0]
    r = r_ref[0]
    gate = r * _sigmoid(r)
    gn = gn_ref[...]
    outs = []
    for h in range(GLA_HEADS):
        oh = o[:, h * GLA_DV:(h + 1) * GLA_DV]
        outs.append(_rms(oh, gn))
    o_ref[0] = (jnp.concatenate(outs, axis=1) * gate).astype(BF16)


def _gla(lq, lk, lv, lr, gf, gb, gla_norm):
    b, t, _ = lq.shape
    c = GLA_CHUNK
    nc = t // c
    fwd = lambda wd: pl.BlockSpec((1, c, wd), lambda bi, ci: (bi, ci, 0))
    rev = lambda wd: pl.BlockSpec((1, c, wd), lambda bi, ci: (bi, nc - 1 - ci, 0))
    st = pltpu.VMEM((2, 2 * GLA_DV, 2 * GLA_DK), F32)
    mask_b = _gla_level_masks(True)
    mask_f = _gla_level_masks(False)
    o_b = pl.pallas_call(
        _gla_bwd_kernel,
        grid=(b, nc),
        in_specs=[rev(GLA_KW), rev(GLA_KW), rev(GLA_VW), rev(GLA_KW), _full_spec(mask_b.shape)],
        out_specs=rev(GLA_VW),
        out_shape=jax.ShapeDtypeStruct((b, t, GLA_VW), F32),
        scratch_shapes=[st],
        compiler_params=_cparams(("parallel", "arbitrary")),
        name="gla_bwd",
    )(lq, lk, lv, gb, mask_b)
    return pl.pallas_call(
        _gla_fwd_kernel,
        grid=(b, nc),
        in_specs=[fwd(GLA_KW), fwd(GLA_KW), fwd(GLA_VW), fwd(GLA_KW), fwd(GLA_VW), fwd(GLA_VW),
                  _full_spec(gla_norm.shape), _full_spec(mask_f.shape)],
        out_specs=fwd(GLA_VW),
        out_shape=jax.ShapeDtypeStruct((b, t, GLA_VW), BF16),
        scratch_shapes=[st],
        compiler_params=_cparams(("parallel", "arbitrary")),
        name="gla_fwd",
    )(lq, lk, lv, gf, o_b, lr, gla_norm, mask_f)


def _post_mix_kernel(x_ref, oa_ref, og_ref, gmix_ref, wga_ref, wgl_ref, wau_ref, wgu_ref, wout_ref, gmoe_ref,
                     wr_hi_ref, wr_lo_ref, x1_ref, hm_ref, aff_ref):
    x = x_ref[...]
    h = _rms(x, gmix_ref[...]).astype(BF16)
    merged = (_sigmoid(_dot(h, wga_ref[...])) * _dot(oa_ref[...], wau_ref[...])
              + _sigmoid(_dot(h, wgl_ref[...])) * _dot(og_ref[...], wgu_ref[...]))
    x1 = x + _dot(merged.astype(BF16), wout_ref[...])
    x1_ref[...] = x1
    hm = _rms(x1, gmoe_ref[...])
    hm_ref[...] = hm
    h_hi, h_lo = _split2(hm)
    wr_hi = wr_hi_ref[...]
    logits = _dot_nt(wr_hi, h_hi) + _dot_nt(wr_hi, h_lo) + _dot_nt(wr_lo_ref[...], h_hi)
    m = jnp.max(logits, axis=0, keepdims=True)
    e = jnp.exp(logits - m)
    aff_ref[...] = e / jnp.sum(e, axis=0, keepdims=True)


def _post_mix(x2d, oa, og, w, tm=512):
    n, d = x2d.shape
    row = lambda wd: pl.BlockSpec((tm, wd), lambda i: (i, 0))
    consts = [w["gmix"], w["wga"], w["wgl"], w["wau"], w["wgu"], w["wout"], w["gmoe"], w["wr_hi"], w["wr_lo"]]
    return pl.pallas_call(
        _post_mix_kernel,
        grid=(n // tm,),
        in_specs=[row(d), row(ATTN_W), row(GLA_VW)] + [_full_spec(c.shape) for c in consts],
        out_specs=[row(d), row(d), pl.BlockSpec((N_EXPERTS, tm), lambda i: (0, i))],
        out_shape=[jax.ShapeDtypeStruct((n, d), F32), jax.ShapeDtypeStruct((n, d), F32),
                   jax.ShapeDtypeStruct((N_EXPERTS, n), F32)],
        compiler_params=_cparams(("parallel",)),
        name="post_mix",
    )(x2d, oa, og, *consts)


ROUTE_SLOTS = 256


def _route_kernel(a_ref, idx_ref, gate_ref, pos_ref, lo_ref, *, cap):
    a = a_ref[0]
    nrow = a.shape[0]
    b_io = lax.broadcasted_iota(I32, (nrow, LANES), 0)
    t_io = lax.broadcasted_iota(I32, (nrow, LANES), 1)
    tok = b_io * LANES + t_io

    def as_float(word):
        return lax.bitcast_convert_type(jnp.full((1, LANES), word, I32), F32)

    def thr_step(i, thr):
        cand = thr | lax.shift_left(jnp.int32(1), jnp.int32(30) - i)
        cnt = jnp.sum((a >= as_float(cand)).astype(I32))
        return jnp.where(cnt >= cap, cand, thr)

    thr = lax.fori_loop(0, 31, thr_step, jnp.int32(0))
    gt = a >= as_float(thr + 1)
    eq = (a >= as_float(thr)) & jnp.logical_not(gt)
    need = cap - jnp.sum(gt.astype(I32))

    def tie_step(i, lim):
        cand = lim + lax.shift_left(jnp.int32(1), jnp.int32(17) - i)
        cnt = jnp.sum((eq & (tok < cand)).astype(I32))
        return jnp.where(cnt < need, cand, lim)

    lim = lax.fori_loop(0, 18, tie_step, jnp.int32(0))
    selm = gt | (eq & (tok <= lim))
    selb = selm.astype(F32).astype(BF16)

    r_io = lax.broadcasted_iota(I32, (LANES, LANES), 0)
    c_io = lax.broadcasted_iota(I32, (LANES, LANES), 1)
    utri = (r_io <= c_io).astype(BF16)
    cum_row = _dot(selb, utri)
    row_cnt = cum_row[:, LANES - 1:LANES]
    br_io = lax.broadcasted_iota(I32, (nrow, nrow), 0)
    bc_io = lax.broadcasted_iota(I32, (nrow, nrow), 1)
    lrow = (bc_io <= br_io).astype(BF16)
    row_incl = _dot(lrow, jnp.broadcast_to(row_cnt, (nrow, LANES)).astype(BF16))
    row_excl = row_incl - row_cnt
    pos_ref[0] = jnp.where(selm, row_excl + cum_row - 1.0, -1.0).astype(I32)
    lo_ref[0] = row_excl.astype(I32)
    row_incl_c = row_incl[:, 0:1]
    row_excl_c = row_excl[:, 0:1]
    cum_t = cum_row.T.astype(BF16)
    a_hi, a_mid, a_lo = _split3(a.T)
    sw = ROUTE_SLOTS
    brow = lax.broadcasted_iota(I32, (nrow, sw), 0).astype(F32)
    tsub = lax.broadcasted_iota(I32, (LANES, sw), 0).astype(F32)
    lane = lax.broadcasted_iota(I32, (1, sw), 1).astype(F32)

    def slot_tile(jt, carry):
        j = lane + jnp.asarray(jt, F32) * float(sw)
        bj = jnp.sum((row_incl_c <= j).astype(F32), axis=0, keepdims=True)
        onehot = (brow == bj).astype(F32)
        base = jnp.sum(onehot * row_excl_c, axis=0, keepdims=True)
        jl = j - base
        ohb = onehot.astype(BF16)
        rt = _dot(cum_t, ohb)
        tj = jnp.sum((rt <= jl).astype(F32), axis=0, keepdims=True)
        acol = _dot(a_hi, ohb) + _dot(a_mid, ohb) + _dot(a_lo, ohb)
        gate = jnp.sum(jnp.where(tsub == tj, acol, 0.0), axis=0, keepdims=True)
        tok_id = (bj * float(LANES) + tj).astype(I32)
        for c in range(sw // LANES):
            row = jt * (sw // LANES) + c
            idx_ref[0, pl.ds(row, 1), :] = tok_id[:, c * LANES:(c + 1) * LANES]
            gate_ref[0, pl.ds(row, 1), :] = gate[:, c * LANES:(c + 1) * LANES]
        return carry

    lax.fori_loop(0, cap // sw, slot_tile, 0)


def _route(aff_t, cap):
    e, n = aff_t.shape
    nrow = n // LANES
    a3 = aff_t.reshape(e, nrow, LANES)
    nt = cap // LANES
    slot_spec = pl.BlockSpec((1, nt, LANES), lambda i: (i, 0, 0))
    tok_spec = pl.BlockSpec((1, nrow, LANES), lambda i: (i, 0, 0))
    idx, gate, pos, lo = pl.pallas_call(
        functools.partial(_route_kernel, cap=cap),
        grid=(e,),
        in_specs=[tok_spec],
        out_specs=[slot_spec, slot_spec, tok_spec, tok_spec],
        out_shape=[jax.ShapeDtypeStruct((e, nt, LANES), I32), jax.ShapeDtypeStruct((e, nt, LANES), F32),
                   jax.ShapeDtypeStruct((e, nrow, LANES), I32), jax.ShapeDtypeStruct((e, nrow, LANES), I32)],
        compiler_params=_cparams(("parallel",)),
        name="route",
    )(a3)
    lo_tab = jnp.concatenate([lo[:, :, 0], jnp.full((e, 1), cap, I32)], axis=1)
    return idx, gate, pos.reshape(e, n), lo_tab


MOE_ISSUE_UNROLL = 8


def _moe_kernel(idx_ref, idxn_ref, gate_ref, hm_hbm, wg_ref, wu_ref, wd_ref, ye_ref, buf_a, buf_b, sem, *, rows):
    nblk = pl.num_programs(1)
    s = pl.program_id(0) * nblk + pl.program_id(1)
    nsteps = pl.num_programs(0) * nblk

    def buffer_wait(buf, k):
        pltpu.make_async_copy(hm_hbm.at[pl.ds(0, rows)], buf, sem.at[k]).wait()

    @pl.when(s == 0)
    def _():
        def body(r0, c):
            for u in range(MOE_ISSUE_UNROLL):
                r = r0 * MOE_ISSUE_UNROLL + u
                n = idx_ref[0, 0, r]
                pltpu.make_async_copy(hm_hbm.at[pl.ds(n, 1)], buf_a.at[pl.ds(r, 1)], sem.at[0]).start()
            return c
        lax.fori_loop(0, rows // MOE_ISSUE_UNROLL, body, 0)

    def step(cur, k_cur, nxt, k_nxt):
        buffer_wait(cur, k_cur)

        @pl.when(s < nsteps)
        def _():
            for r in range(rows):
                n = idxn_ref[0, 0, r]
                pltpu.make_async_copy(hm_hbm.at[pl.ds(n, 1)], nxt.at[pl.ds(r, 1)], sem.at[k_nxt]).start()

        xe = cur[...].astype(BF16)
        hid_g = _dot(xe, wg_ref[0])
        hid = (hid_g * _sigmoid(hid_g)) * _dot(xe, wu_ref[0])
        y = _dot(hid.astype(BF16), wd_ref[0])
        eye = (lax.broadcasted_iota(I32, (LANES, LANES), 0) == lax.broadcasted_iota(I32, (LANES, LANES), 1))
        for c in range(rows // LANES):
            g_row = gate_ref[0, :, c * LANES:(c + 1) * LANES]
            g_col = jnp.sum(jnp.where(eye, g_row, 0.0), axis=1, keepdims=True)
            sl = slice(c * LANES, (c + 1) * LANES)
            ye_ref[sl, :] = (y[sl, :] * g_col).astype(BF16)

        @pl.when(s == nsteps - 1)
        def _():
            buffer_wait(nxt, k_nxt)

    @pl.when(lax.rem(s, 2) == 0)
    def _():
        step(buf_a, 0, buf_b, 1)

    @pl.when(lax.rem(s, 2) == 1)
    def _():
        step(buf_b, 1, buf_a, 0)


def _moe_ffn(idx, gate, hm, wg, wu, wd, rows=256):
    e, nt, _ = idx.shape
    cap = nt * LANES
    n, d = hm.shape
    nblk = cap // rows
    nsteps = e * nblk
    idx3 = idx.reshape(nsteps, 1, rows)
    gate3 = gate.reshape(nsteps, 1, rows)
    ff = wg.shape[-1]
    return pl.pallas_call(
        functools.partial(_moe_kernel, rows=rows),
        grid=(e, nblk),
        in_specs=[pl.BlockSpec((1, 1, rows), lambda ei, j: (ei * nblk + j, 0, 0), memory_space=pltpu.SMEM),
                  pl.BlockSpec((1, 1, rows), lambda ei, j: (jnp.minimum(ei * nblk + j + 1, nsteps - 1), 0, 0),
                               memory_space=pltpu.SMEM),
                  pl.BlockSpec((1, 1, rows), lambda ei, j: (ei * nblk + j, 0, 0)),
                  pl.BlockSpec(memory_space=pl.ANY),
                  pl.BlockSpec((1, d, ff), lambda ei, j: (ei, 0, 0)),
                  pl.BlockSpec((1, d, ff), lambda ei, j: (ei, 0, 0)),
                  pl.BlockSpec((1, ff, d), lambda ei, j: (ei, 0, 0))],
        out_specs=pl.BlockSpec((rows, d), lambda ei, j: (ei * nblk + j, 0)),
        out_shape=jax.ShapeDtypeStruct((e * cap, d), BF16),
        scratch_shapes=[pltpu.VMEM((rows, d), F32), pltpu.VMEM((rows, d), F32), pltpu.SemaphoreType.DMA((2,))],
        compiler_params=_cparams(("arbitrary", "arbitrary")),
        name="moe_ffn",
    )(idx3, idx3, gate3, hm, wg, wu, wd)


COMBINE_WIN = 64


def _ple_final_kernel(lo_ref, x_ref, p_ref, pos_ref, ye_hbm, gple_ref, wpg_ref, wpp_ref, gfin_ref, o_ref,
                      win_a, win_b, acc_ref, sem, *, cap, rows_per_tile):
    i = pl.program_id(0)
    ntiles = pl.num_programs(0)
    win = COMBINE_WIN
    tm = x_ref.shape[0]

    def tile_floors(ti):
        return [jnp.bitwise_and(lo_ref[e, ti * rows_per_tile], -BF16_ROWS) for e in range(N_EXPERTS)]

    def clamp(floors):
        return [jnp.minimum(f, cap - win) for f in floors]

    def start_fetch(floors, buf, k):
        starts = clamp(floors)
        for e in range(N_EXPERTS):
            src = ye_hbm.at[pl.ds(pl.multiple_of(e * cap + starts[e], BF16_ROWS), win)]
            pltpu.make_async_copy(src, buf.at[pl.ds(e * win, win)], sem.at[k]).start()

    def wait_fetch(buf, k):
        pltpu.make_async_copy(ye_hbm.at[pl.ds(0, N_EXPERTS * win)], buf, sem.at[k]).wait()

    def expand(floors, buf):
        starts = clamp(floors)
        pos = pos_ref[...]
        w_io = lax.broadcasted_iota(I32, (win, tm), 0)
        parts = []
        for e in range(N_EXPERTS):
            pe = pos[e:e + 1, :]
            hit = ((pe - starts[e]) == w_io) & (pe >= floors[e])
            parts.append(jnp.where(hit, 1.0, 0.0).astype(BF16))
        onehot_t = jnp.concatenate(parts, axis=0)
        return _dot_tn(onehot_t, buf[...])

    @pl.when(i == 0)
    def _():
        start_fetch(tile_floors(0), win_a, 0)

    def step(cur, k_cur, nxt, k_nxt):
        floors0 = tile_floors(i)
        wait_fetch(cur, k_cur)

        @pl.when(i + 1 < ntiles)
        def _():
            start_fetch(tile_floors(i + 1), nxt, k_nxt)

        acc_ref[...] = x_ref[...] + expand(floors0, cur)

        nrounds = jnp.int32(1)
        for e in range(N_EXPERTS):
            hi = lo_ref[e, (i + 1) * rows_per_tile]
            nrounds = jnp.maximum(nrounds, lax.div(hi - floors0[e] + (win - 1), jnp.int32(win)))

        def extra_round(r, c):
            floors = [f + r * win for f in floors0]
            start_fetch(floors, cur, k_cur)
            wait_fetch(cur, k_cur)
            acc_ref[...] += expand(floors, cur)
            return c

        lax.fori_loop(1, nrounds, extra_round, 0)

        x = acc_ref[...]
        hp = _rms(x, gple_ref[...]).astype(BF16)
        gate = _sigmoid(_dot(hp, wpg_ref[...]))
        y = x + gate * _dot(p_ref[...].astype(BF16), wpp_ref[...])
        o_ref[...] = _rms(y, gfin_ref[...])

    @pl.when(lax.rem(i, 2) == 0)
    def _():
        step(win_a, 0, win_b, 1)

    @pl.when(lax.rem(i, 2) == 1)
    def _():
        step(win_b, 1, win_a, 0)


def _ple_final(x1, p2d, pos, lo_tab, ye, cap, w, tm=256):
    n, d = x1.shape
    pd = p2d.shape[-1]
    row = lambda wd: pl.BlockSpec((tm, wd), lambda i, lo: (i, 0))
    consts = [w["gple"], w["wpg"], w["wpp"], w["gfin"]]
    const_spec = lambda c: pl.BlockSpec(c.shape, lambda i, lo: (0,) * c.ndim)
    grid_spec = pltpu.PrefetchScalarGridSpec(
        num_scalar_prefetch=1,
        grid=(n // tm,),
        in_specs=[row(d), row(pd), pl.BlockSpec((N_EXPERTS, tm), lambda i, lo: (0, i)),
                  pl.BlockSpec(memory_space=pl.ANY)] + [const_spec(c) for c in consts],
        out_specs=row(d),
        scratch_shapes=[pltpu.VMEM((N_EXPERTS * COMBINE_WIN, d), BF16),
                        pltpu.VMEM((N_EXPERTS * COMBINE_WIN, d), BF16), pltpu.VMEM((tm, d), F32),
                        pltpu.SemaphoreType.DMA((2,))],
    )
    return pl.pallas_call(
        functools.partial(_ple_final_kernel, cap=cap, rows_per_tile=tm // LANES),
        grid_spec=grid_spec,
        out_shape=jax.ShapeDtypeStruct((n, d), F32),
        compiler_params=_cparams(("arbitrary",)),
        name="ple_final",
    )(lo_tab, x1, p2d, pos, ye, *consts)


def _rope_tables(T):
    half = HEAD_DIM // 2
    inv = ROPE_THETA ** (-jnp.arange(0, half, 2, dtype=F32) / half)
    t = jnp.arange(T)
    ang_r = (t // GRID_W).astype(F32)[:, None] * inv
    ang_c = (t % GRID_W).astype(F32)[:, None] * inv
    cos = jnp.concatenate([jnp.cos(ang_r)] * 2 + [jnp.cos(ang_c)] * 2, axis=1)
    sin = jnp.concatenate([-jnp.sin(ang_r), jnp.sin(ang_r), -jnp.sin(ang_c), jnp.sin(ang_c)], axis=1)
    return jnp.tile(cos, (1, N_Q_HEADS)), jnp.tile(sin, (1, N_Q_HEADS))


def _block_diag_ones(width):
    i = np.arange(width)
    return jnp.asarray((i[:, None] // HEAD_DIM) == (i[None, :] // HEAD_DIM), BF16)


def _prep_weights(g_mix, w_in, q_norm, k_norm, w_af, b_af, w_ab, b_ab, gla_norm, w_attn_up, w_gla_up,
                  w_out, g_moe, w_router, w_gate_e, w_up_e, w_down_e, g_ple, w_ple_gate, w_ple_proj,
                  g_final):
    d = w_in.shape[0]
    o_gla = ATTN_W + 2 * KV_W
    o_lr = o_gla + 2 * GLA_KW + 2 * GLA_VW
    o_ga = o_lr + 2 * GLA_LOWRANK
    wlr = jnp.zeros((d, LANES), F32).at[:, :2 * GLA_LOWRANK].set(w_in[:, o_lr:o_ga])
    wgate = jnp.zeros((LANES, 2 * GLA_KW), F32)
    wgate = wgate.at[:GLA_LOWRANK, :GLA_KW].set(w_af)
    wgate = wgate.at[GLA_LOWRANK:2 * GLA_LOWRANK, GLA_KW:].set(w_ab)
    wr_t = w_router.T
    wr_hi = wr_t.astype(BF16)
    wr_lo = (wr_t - wr_hi.astype(F32)).astype(BF16)
    return dict(
        gmix=g_mix.reshape(1, d),
        wqk=w_in[:, :ATTN_W + KV_W].astype(BF16),
        wvt=w_in[:, ATTN_W + KV_W:o_gla].T.astype(BF16),
        wgla=w_in[:, o_gla:o_lr].astype(BF16),
        wlr=wlr.astype(BF16),
        wga=w_in[:, o_ga:o_ga + d].astype(BF16),
        wgl=w_in[:, o_ga + d:].astype(BF16),
        wgate=wgate.astype(BF16),
        bgate=jnp.concatenate([b_af, b_ab]).reshape(1, 2 * GLA_KW),
        qn=jnp.tile(q_norm, N_Q_HEADS).reshape(1, ATTN_W),
        kn=jnp.tile(k_norm, N_KV_HEADS).reshape(1, KV_W),
        bdq=_block_diag_ones(ATTN_W),
        bdk=_block_diag_ones(KV_W),
        gnorm=gla_norm.reshape(1, GLA_DV),
        wau=w_attn_up.astype(BF16),
        wgu=w_gla_up.astype(BF16),
        wout=w_out.astype(BF16),
        gmoe=g_moe.reshape(1, d),
        wr_hi=wr_hi,
        wr_lo=wr_lo,
        wge=w_gate_e.astype(BF16),
        wue=w_up_e.astype(BF16),
        wde=w_down_e.astype(BF16),
        gple=g_ple.reshape(1, d),
        wpg=w_ple_gate.astype(BF16),
        wpp=w_ple_proj.astype(BF16),
        gfin=g_final.reshape(1, d),
    )


def _trunk(x, p, w):
    b, t, d = x.shape
    n = b * t
    x2d = x.reshape(n, d)
    q, k, vt, lq, lk, lv, lr, gf, gb = _in_proj(x2d, t, _rope_tables(t), w)
    r3 = lambda a: a.reshape(b, t, a.shape[-1])
    o_attn = _attention(r3(q), r3(k), vt).reshape(n, ATTN_W)
    o_gla = _gla(r3(lq), r3(lk), r3(lv), r3(lr), r3(gf), r3(gb), w["gnorm"]).reshape(n, GLA_VW)
    x1, hm, aff_t = _post_mix(x2d, o_attn, o_gla, w)
    cap = EC_CAPACITY * n // N_EXPERTS
    idx, gate, pos, lo_tab = _route(aff_t, cap)
    ye = _moe_ffn(idx, gate, hm, w["wge"], w["wue"], w["wde"])
    y = _ple_final(x1, p.reshape(n, p.shape[-1]), pos, lo_tab, ye, cap, w)
    return y.reshape(b, t, d)


def kernel(x_prompt, x_sample, p_prompt, p_sample, g_mix, w_in, q_norm, k_norm, w_af, b_af, w_ab, b_ab,
           gla_norm, w_attn_up, w_gla_up, w_out, g_moe, w_router, w_gate_e, w_up_e, w_down_e, g_ple,
           w_ple_gate, w_ple_proj, g_final):
    assert g_mix.shape[0] == 1, "single layer"
    w = _prep_weights(g_mix[0], w_in[0], q_norm[0], k_norm[0], w_af[0], b_af[0], w_ab[0], b_ab[0],
                      gla_norm[0], w_attn_up[0], w_gla_up[0], w_out[0], g_moe[0], w_router[0],
                      w_gate_e[0], w_up_e[0], w_down_e[0], g_ple[0], w_ple_gate[0], w_ple_proj[0], g_final)
    return (_trunk(x_prompt, p_prompt[0], w), _trunk(x_sample, p_sample[0], w))
```

```python
import functools

import numpy as np
import jax
import jax.numpy as jnp
from jax import lax
from jax.experimental import pallas as pl
from jax.experimental.pallas import tpu as pltpu

F32 = jnp.float32
BF16 = jnp.bfloat16
I32 = jnp.int32

HEAD_DIM = 64
N_Q_HEADS = 8
N_KV_HEADS = 2
GRID_W = 64
ROPE_THETA = 10000.0
GLA_HEADS = 4
GLA_DK = 64
GLA_DV = 128
GLA_LOWRANK = 16
GLA_GATE_NORM = 16.0
N_EXPERTS = 16
EC_CAPACITY = 2
NORM_EPS = 1e-6
LOG2E = 1.4426950408889634

ATTN_W = N_Q_HEADS * HEAD_DIM
KV_W = N_KV_HEADS * HEAD_DIM
GLA_KW = GLA_HEADS * GLA_DK
GLA_VW = GLA_HEADS * GLA_DV

LANES = 128
SUBLANES = 8
BF16_ROWS = 16
VMEM_LIMIT = 56 * 1024 * 1024

GLA_CHUNK = 128
GLA_LEVELS = 7
GLA_STEP_CHUNKS = 4


def _cparams(sem):
    return pltpu.CompilerParams(dimension_semantics=sem, vmem_limit_bytes=VMEM_LIMIT)


def _dot(a, b):
    return jnp.dot(a, b, preferred_element_type=F32)


def _dot_nt(a, b):
    return lax.dot_general(a, b, (((1,), (1,)), ((), ())), preferred_element_type=F32)


def _dot_tn(a, b):
    return lax.dot_general(a, b, (((0,), (0,)), ((), ())), preferred_element_type=F32)


def _split2(x):
    hi = x.astype(BF16)
    lo = (x - hi.astype(F32)).astype(BF16)
    return hi, lo


def _split3(x):
    hi = x.astype(BF16)
    r = x - hi.astype(F32)
    mid = r.astype(BF16)
    lo = (r - mid.astype(F32)).astype(BF16)
    return hi, mid, lo


def _rms(x, gain):
    ms = jnp.mean(x * x, axis=-1, keepdims=True)
    return x * lax.rsqrt(ms + NORM_EPS) * gain


def _sigmoid(x):
    return 1.0 / (1.0 + jnp.exp(-x))


def _full_spec(shape):
    nd = len(shape)
    return pl.BlockSpec(shape, lambda *_: (0,) * nd)


def _head_ms(x, bd):
    hi, lo = _split2(x * x)
    return (_dot(hi, bd) + _dot(lo, bd)) * (1.0 / HEAD_DIM)


def _rope(x, cos, sin):
    w = x.shape[-1]
    lane = lax.broadcasted_iota(I32, x.shape, 1)
    fwd = pltpu.roll(x, w - 16, axis=1)
    bwd = pltpu.roll(x, 16, axis=1)
    partner = jnp.where((lane % 32) < 16, fwd, bwd)
    return x * cos + partner * sin


def _in_proj_kernel(x_ref, cos_ref, sin_ref, gmix_ref, wqk_ref, wvt_ref, wgla_ref, wlr_ref,
                    wgate_ref, bgate_ref, qn_ref, kn_ref, bdq_ref, bdk_ref,
                    q_ref, k_ref, vt_ref, lq_ref, lk_ref, lv_ref, lr_ref, gf_ref, gb_ref):
    x = x_ref[...]
    h = _rms(x, gmix_ref[...]).astype(BF16)

    qk = _dot(h, wqk_ref[...])
    cos = cos_ref[...]
    sin = sin_ref[...]
    q = qk[:, :ATTN_W]
    q = q * lax.rsqrt(_head_ms(q, bdq_ref[...]) + NORM_EPS) * qn_ref[...]
    q = _rope(q, cos, sin) * (HEAD_DIM ** -0.5 * LOG2E)
    q_ref[...] = q.astype(BF16)
    k = qk[:, ATTN_W:]
    k = k * lax.rsqrt(_head_ms(k, bdk_ref[...]) + NORM_EPS) * kn_ref[...]
    k = _rope(k, cos[:, :KV_W], sin[:, :KV_W])
    k_ref[...] = k.astype(BF16)
    vt_ref[0] = _dot_nt(wvt_ref[...], h).astype(BF16)

    gla = _dot(h, wgla_ref[...])
    lq_ref[...] = gla[:, :GLA_KW] * (GLA_DK ** -0.5)
    lk_ref[...] = gla[:, GLA_KW:2 * GLA_KW]
    lv_ref[...] = gla[:, 2 * GLA_KW:2 * GLA_KW + GLA_VW].astype(BF16)
    lr_ref[...] = gla[:, 2 * GLA_KW + GLA_VW:]

    a = _dot(h, wlr_ref[...]).astype(BF16)
    z = _dot(a, wgate_ref[...]) + bgate_ref[...]
    g = (jnp.minimum(z, 0.0) - jnp.log(1.0 + jnp.exp(-jnp.abs(z)))) * (1.0 / GLA_GATE_NORM)
    gf_ref[...] = g[:, :GLA_KW]
    gb_ref[...] = g[:, GLA_KW:]


def _in_proj(x2d, T, tabs, w, tm=512):
    n, d = x2d.shape
    nt = T // tm
    row = lambda wd: pl.BlockSpec((tm, wd), lambda i: (i, 0))
    tab = pl.BlockSpec((tm, ATTN_W), lambda i: (i % nt, 0))
    consts = [w["gmix"], w["wqk"], w["wvt"], w["wgla"], w["wlr"], w["wgate"], w["bgate"],
              w["qn"], w["kn"], w["bdq"], w["bdk"]]
    outs = [(ATTN_W, BF16), (KV_W, BF16), None, (GLA_KW, F32), (GLA_KW, F32), (GLA_VW, BF16),
            (GLA_VW, F32), (GLA_KW, F32), (GLA_KW, F32)]
    vt_spec = pl.BlockSpec((1, KV_W, tm), lambda i: (i // nt, 0, i % nt))
    vt_shape = jax.ShapeDtypeStruct((n // T, KV_W, T), BF16)
    return pl.pallas_call(
        _in_proj_kernel,
        grid=(n // tm,),
        in_specs=[row(d), tab, tab] + [_full_spec(c.shape) for c in consts],
        out_specs=[vt_spec if o is None else row(o[0]) for o in outs],
        out_shape=[vt_shape if o is None else jax.ShapeDtypeStruct((n, o[0]), o[1]) for o in outs],
        compiler_params=_cparams(("parallel",)),
        name="in_proj",
    )(x2d, tabs[0], tabs[1], *consts)


def _attn_kernel(q_ref, k_ref, vt_ref, o_ref, qpad_ref, m_ref, acc_ref, s_even, s_odd, mx_even, mx_odd, *, nk):
    j = pl.program_id(2)
    group = N_Q_HEADS // N_KV_HEADS
    even = (s_even, mx_even)
    odd = (s_odd, mx_odd)

    def head_scores(dst, hq):
        s = _dot_nt(k_ref[0], qpad_ref[hq])
        dst[0][hq] = s
        dst[1][hq] = jnp.max(s, axis=0, keepdims=True)

    def head_softmax(src, hq):
        m_old = m_ref[hq]
        m_new = jnp.maximum(m_old, src[1][hq])
        alpha = jnp.exp2(m_old - m_new)
        p = jnp.exp2(src[0][hq] - m_new)
        m_ref[hq] = m_new
        return p.astype(BF16), alpha

    def head_values(hq, p, alpha):
        hk = hq // group
        ones = jnp.ones((BF16_ROWS, vt_ref.shape[2]), BF16)
        vt = vt_ref[0, hk * HEAD_DIM:(hk + 1) * HEAD_DIM, :]
        vt1 = jnp.concatenate([vt, ones], axis=0)
        acc_ref[hq] = alpha * acc_ref[hq] + _dot(vt1, p)

    def scores_into(dst):
        for hq in range(N_Q_HEADS):
            head_scores(dst, hq)

    def consume(src, dst=None):
        pending = None
        for hq in range(N_Q_HEADS):
            if dst is not None:
                head_scores(dst, hq)
            p, alpha = head_softmax(src, hq)
            if pending is not None:
                head_values(*pending)
            pending = (hq, p, alpha)
        head_values(*pending)

    @pl.when(j == 0)
    def _():
        m_ref[...] = jnp.full(m_ref.shape, -jnp.inf, F32)
        acc_ref[...] = jnp.zeros(acc_ref.shape, F32)
        lane = lax.broadcasted_iota(I32, (q_ref.shape[1], LANES), 1)
        for hq in range(N_Q_HEADS):
            hk = hq // group
            chunk = q_ref[0, :, (hq // 2) * LANES:(hq // 2 + 1) * LANES].astype(F32)
            if hq % 2 != hk:
                chunk = pltpu.roll(chunk, HEAD_DIM, axis=1)
            keep = (lane // HEAD_DIM) == hk
            qpad_ref[hq] = jnp.where(keep, chunk, 0.0).astype(BF16)
        scores_into(even)

    @pl.when((j > 0) & (j < nk) & (lax.rem(j, 2) == 1))
    def _():
        consume(even, odd)

    @pl.when((j > 0) & (j < nk) & (lax.rem(j, 2) == 0))
    def _():
        consume(odd, even)

    @pl.when(j == nk)
    def _():
        consume(odd if nk % 2 == 0 else even)
        outs = [acc_ref[hq, :HEAD_DIM, :] / acc_ref[hq, HEAD_DIM:HEAD_DIM + 1, :] for hq in range(N_Q_HEADS)]
        o_ref[0] = jnp.concatenate(outs, axis=0).T.astype(BF16)


def _attention(q, k, vt, tq=1024, tk=512):
    b, t, _ = q.shape
    nk = t // tk
    s_buf = pltpu.VMEM((N_Q_HEADS, tk, tq), F32)
    mx_buf = pltpu.VMEM((N_Q_HEADS, 1, tq), F32)
    return pl.pallas_call(
        functools.partial(_attn_kernel, nk=nk),
        grid=(b, t // tq, nk + 1),
        in_specs=[pl.BlockSpec((1, tq, ATTN_W), lambda bi, i, j: (bi, i, 0)),
                  pl.BlockSpec((1, tk, KV_W), lambda bi, i, j: (bi, jnp.minimum(j, nk - 1), 0)),
                  pl.BlockSpec((1, KV_W, tk), lambda bi, i, j: (bi, 0, jnp.maximum(j - 1, 0)))],
        out_specs=pl.BlockSpec((1, tq, ATTN_W), lambda bi, i, j: (bi, i, 0)),
        out_shape=jax.ShapeDtypeStruct((b, t, ATTN_W), BF16),
        scratch_shapes=[pltpu.VMEM((N_Q_HEADS, tq, LANES), BF16),
                        pltpu.VMEM((N_Q_HEADS, 1, tq), F32),
                        pltpu.VMEM((N_Q_HEADS, HEAD_DIM + BF16_ROWS, tq), F32),
                        s_buf, s_buf, mx_buf, mx_buf],
        compiler_params=_cparams(("parallel", "parallel", "arbitrary")),
        name="attention",
    )(q, k, vt)


def _gla_chunk(q, k, v, g, st_ref, mask_ref, reverse):
    c = GLA_CHUNK
    pair_w = 2 * GLA_DK
    t_kw = lax.broadcasted_iota(I32, (c, GLA_KW), 0)
    lo_lanes = lax.broadcasted_iota(I32, (c, pair_w), 1) < GLA_DK

    a_pairs = [jnp.zeros((c, 2 * c), F32) for _ in range(2)]

    def add_level(a_pairs, qh, kh, mask2):
        qb = qh.astype(BF16)
        kb = kh.astype(BF16)
        out = []
        for p in range(2):
            qp = qb[:, p * pair_w:(p + 1) * pair_w]
            kp = kb[:, p * pair_w:(p + 1) * pair_w]
            k2 = jnp.concatenate([jnp.where(lo_lanes, kp, 0.0), jnp.where(lo_lanes, 0.0, kp)], axis=0)
            out.append(a_pairs[p] + mask2 * _dot_nt(qp, k2))
        return out

    pre = g
    tot = g
    for lvl in range(GLA_LEVELS):
        s = 1 << lvl
        odd = ((t_kw >> lvl) & 1) == 1
        if not reverse:
            qh = q * jnp.exp(pre)
            kh = k * jnp.exp(tot - pre)
        else:
            qh = q * jnp.exp(tot - pre + g)
            kh = k * jnp.exp(pre - g)
        a_pairs = add_level(a_pairs, qh, kh, mask_ref[lvl])
        sib = jnp.where(odd, pltpu.roll(tot, s, axis=0), pltpu.roll(tot, c - s, axis=0))
        pre = pre + jnp.where(odd, sib, 0.0)
        tot = tot + sib
    if not reverse:
        a_pairs = add_level(a_pairs, q, k, mask_ref[GLA_LEVELS])

    if not reverse:
        q_dec = q * jnp.exp(pre)
        k_dec = k * jnp.exp(tot - pre)
    else:
        q_dec = q * jnp.exp(tot - pre + g)
        k_dec = k * jnp.exp(pre - g)
    chunk_decay = jnp.exp(tot[0:1, :])
    sub_pair = lax.broadcasted_iota(I32, (2 * GLA_DV, pair_w), 0)
    lane_st = lax.broadcasted_iota(I32, (2 * GLA_DV, pair_w), 1)
    block_diag = (sub_pair // GLA_DV) == (lane_st // GLA_DK)
    first_head_v = lax.broadcasted_iota(I32, (c, 2 * GLA_DV), 1) < GLA_DV

    outs = []
    for p in range(2):
        st = st_ref[p]
        qd = q_dec[:, p * pair_w:(p + 1) * pair_w].astype(BF16)
        o_pair = _dot_nt(qd, st.astype(BF16))
        vp = v[:, p * 2 * GLA_DV:(p + 1) * 2 * GLA_DV].astype(BF16)
        kd = k_dec[:, p * pair_w:(p + 1) * pair_w].astype(BF16)
        upd = _dot_tn(vp, kd)
        st_ref[p] = st * chunk_decay[:, p * pair_w:(p + 1) * pair_w] + jnp.where(block_diag, upd, 0.0)
        v_diag = jnp.concatenate([jnp.where(first_head_v, vp, 0.0), jnp.where(first_head_v, 0.0, vp)], axis=0)
        outs.append(o_pair + _dot(a_pairs[p].astype(BF16), v_diag))
    return jnp.concatenate(outs, axis=1)


def _gla_level_masks(reverse):
    c = GLA_CHUNK
    t = np.arange(c)[:, None]
    s = np.arange(c)[None, :]
    masks = []
    for lvl in range(GLA_LEVELS):
        bt, bs = t >> lvl, s >> lvl
        masks.append((bs == bt + 1) & (bt % 2 == 0) if reverse else (bt == bs + 1) & (bs % 2 == 0))
    masks.append(t == s)
    m = np.stack(masks).astype(np.float32)
    return jnp.asarray(np.concatenate([m, m], axis=2))


def _gla_bwd_kernel(q_ref, k_ref, v_ref, g_ref, mask_ref, o_ref, st_ref):
    @pl.when(pl.program_id(1) == 0)
    def _():
        st_ref[...] = jnp.zeros(st_ref.shape, F32)

    for sub in reversed(range(GLA_STEP_CHUNKS)):
        rows = slice(sub * GLA_CHUNK, (sub + 1) * GLA_CHUNK)
        o_ref[0, rows, :] = _gla_chunk(q_ref[0, rows, :], k_ref[0, rows, :], v_ref[0, rows, :], g_ref[0, rows, :],
                                       st_ref, mask_ref, True)


def _gla_fwd_kernel(q_ref, k_ref, v_ref, g_ref, ob_ref, r_ref, gn_ref, mask_ref, o_ref, st_ref):
    @pl.when(pl.program_id(1) == 0)
    def _():
        st_ref[...] = jnp.zeros(st_ref.shape, F32)

    gn = gn_ref[...]
    for sub in range(GLA_STEP_CHUNKS):
        rows = slice(sub * GLA_CHUNK, (sub + 1) * GLA_CHUNK)
        o = _gla_chunk(q_ref[0, rows, :], k_ref[0, rows, :], v_ref[0, rows, :], g_ref[0, rows, :],
                       st_ref, mask_ref, False) + ob_ref[0, rows, :]
        r = r_ref[0, rows, :]
        gate = r * _sigmoid(r)
        outs = []
        for h in range(GLA_HEADS):
            oh = o[:, h * GLA_DV:(h + 1) * GLA_DV]
            outs.append(_rms(oh, gn))
        o_ref[0, rows, :] = (jnp.concatenate(outs, axis=1) * gate).astype(BF16)


def _gla(lq, lk, lv, lr, gf, gb, gla_norm):
    b, t, _ = lq.shape
    c = GLA_CHUNK * GLA_STEP_CHUNKS
    nc = t // c
    fwd = lambda wd: pl.BlockSpec((1, c, wd), lambda bi, ci: (bi, ci, 0))
    rev = lambda wd: pl.BlockSpec((1, c, wd), lambda bi, ci: (bi, nc - 1 - ci, 0))
    st = pltpu.VMEM((2, 2 * GLA_DV, 2 * GLA_DK), F32)
    mask_b = _gla_level_masks(True)
    mask_f = _gla_level_masks(False)
    o_b = pl.pallas_call(
        _gla_bwd_kernel,
        grid=(b, nc),
        in_specs=[rev(GLA_KW), rev(GLA_KW), rev(GLA_VW), rev(GLA_KW), _full_spec(mask_b.shape)],
        out_specs=rev(GLA_VW),
        out_shape=jax.ShapeDtypeStruct((b, t, GLA_VW), F32),
        scratch_shapes=[st],
        compiler_params=_cparams(("parallel", "arbitrary")),
        name="gla_bwd",
    )(lq, lk, lv, gb, mask_b)
    return pl.pallas_call(
        _gla_fwd_kernel,
        grid=(b, nc),
        in_specs=[fwd(GLA_KW), fwd(GLA_KW), fwd(GLA_VW), fwd(GLA_KW), fwd(GLA_VW), fwd(GLA_VW),
                  _full_spec(gla_norm.shape), _full_spec(mask_f.shape)],
        out_specs=fwd(GLA_VW),
        out_shape=jax.ShapeDtypeStruct((b, t, GLA_VW), BF16),
        scratch_shapes=[st],
        compiler_params=_cparams(("parallel", "arbitrary")),
        name="gla_fwd",
    )(lq, lk, lv, gf, o_b, lr, gla_norm, mask_f)


def _post_mix_kernel(x_ref, oa_ref, og_ref, gmix_ref, wga_ref, wgl_ref, wau_ref, wgu_ref, wout_ref, gmoe_ref,
                     wr_hi_ref, wr_lo_ref, x1_ref, hm_ref, aff_ref):
    x = x_ref[...]
    h = _rms(x, gmix_ref[...]).astype(BF16)
    merged = (_sigmoid(_dot(h, wga_ref[...])) * _dot(oa_ref[...], wau_ref[...])
              + _sigmoid(_dot(h, wgl_ref[...])) * _dot(og_ref[...], wgu_ref[...]))
    x1 = x + _dot(merged.astype(BF16), wout_ref[...])
    x1_ref[...] = x1
    hm = _rms(x1, gmoe_ref[...])
    hm_ref[...] = hm
    h_hi, h_lo = _split2(hm)
    wr_hi = wr_hi_ref[...]
    logits = _dot_nt(wr_hi, h_hi) + _dot_nt(wr_hi, h_lo) + _dot_nt(wr_lo_ref[...], h_hi)
    m = jnp.max(logits, axis=0, keepdims=True)
    e = jnp.exp(logits - m)
    aff_ref[...] = e / jnp.sum(e, axis=0, keepdims=True)


def _post_mix(x2d, oa, og, w, tm=512):
    n, d = x2d.shape
    row = lambda wd: pl.BlockSpec((tm, wd), lambda i: (i, 0))
    consts = [w["gmix"], w["wga"], w["wgl"], w["wau"], w["wgu"], w["wout"], w["gmoe"], w["wr_hi"], w["wr_lo"]]
    return pl.pallas_call(
        _post_mix_kernel,
        grid=(n // tm,),
        in_specs=[row(d), row(ATTN_W), row(GLA_VW)] + [_full_spec(c.shape) for c in consts],
        out_specs=[row(d), row(d), pl.BlockSpec((N_EXPERTS, tm), lambda i: (0, i))],
        out_shape=[jax.ShapeDtypeStruct((n, d), F32), jax.ShapeDtypeStruct((n, d), F32),
                   jax.ShapeDtypeStruct((N_EXPERTS, n), F32)],
        compiler_params=_cparams(("parallel",)),
        name="post_mix",
    )(x2d, oa, og, *consts)


ROUTE_SLOTS = 256


def _route_kernel(a_ref, idx_ref, gate_ref, pos_ref, lo_ref, *, cap):
    a = a_ref[0]
    nrow = a.shape[0]
    b_io = lax.broadcasted_iota(I32, (nrow, LANES), 0)
    t_io = lax.broadcasted_iota(I32, (nrow, LANES), 1)
    tok = b_io * LANES + t_io

    def as_float(word):
        return lax.bitcast_convert_type(jnp.full((1, LANES), word, I32), F32)

    def thr_step(i, thr):
        cand = thr | lax.shift_left(jnp.int32(1), jnp.int32(30) - i)
        cnt = jnp.sum((a >= as_float(cand)).astype(I32))
        return jnp.where(cnt >= cap, cand, thr)

    thr = lax.fori_loop(0, 31, thr_step, jnp.int32(0))
    gt = a >= as_float(thr + 1)
    eq = (a >= as_float(thr)) & jnp.logical_not(gt)
    need = cap - jnp.sum(gt.astype(I32))

    def tie_step(i, lim):
        cand = lim + lax.shift_left(jnp.int32(1), jnp.int32(17) - i)
        cnt = jnp.sum((eq & (tok < cand)).astype(I32))
        return jnp.where(cnt < need, cand, lim)

    lim = lax.fori_loop(0, 18, tie_step, jnp.int32(0))
    selm = gt | (eq & (tok <= lim))
    selb = selm.astype(F32).astype(BF16)

    r_io = lax.broadcasted_iota(I32, (LANES, LANES), 0)
    c_io = lax.broadcasted_iota(I32, (LANES, LANES), 1)
    utri = (r_io <= c_io).astype(BF16)
    cum_row = _dot(selb, utri)
    row_cnt = cum_row[:, LANES - 1:LANES]
    br_io = lax.broadcasted_iota(I32, (nrow, nrow), 0)
    bc_io = lax.broadcasted_iota(I32, (nrow, nrow), 1)
    lrow = (bc_io <= br_io).astype(BF16)
    row_incl = _dot(lrow, jnp.broadcast_to(row_cnt, (nrow, LANES)).astype(BF16))
    row_excl = row_incl - row_cnt
    pos_ref[0] = jnp.where(selm, row_excl + cum_row - 1.0, -1.0).astype(I32)
    lo_ref[0] = row_excl.astype(I32)
    row_incl_c = row_incl[:, 0:1]
    row_excl_c = row_excl[:, 0:1]
    cum_t = cum_row.T.astype(BF16)
    a_hi, a_mid, a_lo = _split3(a.T)
    sw = ROUTE_SLOTS
    brow = lax.broadcasted_iota(I32, (nrow, sw), 0).astype(F32)
    tsub = lax.broadcasted_iota(I32, (LANES, sw), 0).astype(F32)
    lane = lax.broadcasted_iota(I32, (1, sw), 1).astype(F32)

    def slot_tile(jt, carry):
        j = lane + jnp.asarray(jt, F32) * float(sw)
        bj = jnp.sum((row_incl_c <= j).astype(F32), axis=0, keepdims=True)
        onehot = (brow == bj).astype(F32)
        base = jnp.sum(onehot * row_excl_c, axis=0, keepdims=True)
        jl = j - base
        ohb = onehot.astype(BF16)
        rt = _dot(cum_t, ohb)
        tj = jnp.sum((rt <= jl).astype(F32), axis=0, keepdims=True)
        acol = _dot(a_hi, ohb) + _dot(a_mid, ohb) + _dot(a_lo, ohb)
        gate = jnp.sum(jnp.where(tsub == tj, acol, 0.0), axis=0, keepdims=True)
        tok_id = (bj * float(LANES) + tj).astype(I32)
        for c in range(sw // LANES):
            row = jt * (sw // LANES) + c
            idx_ref[0, pl.ds(row, 1), :] = tok_id[:, c * LANES:(c + 1) * LANES]
            gate_ref[0, pl.ds(row, 1), :] = gate[:, c * LANES:(c + 1) * LANES]
        return carry

    lax.fori_loop(0, cap // sw, slot_tile, 0)


def _route(aff_t, cap):
    e, n = aff_t.shape
    nrow = n // LANES
    a3 = aff_t.reshape(e, nrow, LANES)
    nt = cap // LANES
    slot_spec = pl.BlockSpec((1, nt, LANES), lambda i: (i, 0, 0))
    tok_spec = pl.BlockSpec((1, nrow, LANES), lambda i: (i, 0, 0))
    idx, gate, pos, lo = pl.pallas_call(
        functools.partial(_route_kernel, cap=cap),
        grid=(e,),
        in_specs=[tok_spec],
        out_specs=[slot_spec, slot_spec, tok_spec, tok_spec],
        out_shape=[jax.ShapeDtypeStruct((e, nt, LANES), I32), jax.ShapeDtypeStruct((e, nt, LANES), F32),
                   jax.ShapeDtypeStruct((e, nrow, LANES), I32), jax.ShapeDtypeStruct((e, nrow, LANES), I32)],
        compiler_params=_cparams(("parallel",)),
        name="route",
    )(a3)
    lo_tab = jnp.concatenate([lo[:, :, 0], jnp.full((e, 1), cap, I32)], axis=1)
    return idx, gate, pos.reshape(e, n), lo_tab


MOE_ISSUE_UNROLL = 8


def _moe_kernel(idx_ref, idxn_ref, gate_ref, hm_hbm, wg_ref, wu_ref, wd_ref, ye_ref, buf_a, buf_b, sem, *, rows):
    nblk = pl.num_programs(1)
    s = pl.program_id(0) * nblk + pl.program_id(1)
    nsteps = pl.num_programs(0) * nblk

    def buffer_wait(buf, k):
        pltpu.make_async_copy(hm_hbm.at[pl.ds(0, rows)], buf, sem.at[k]).wait()

    @pl.when(s == 0)
    def _():
        def body(r0, c):
            for u in range(MOE_ISSUE_UNROLL):
                r = r0 * MOE_ISSUE_UNROLL + u
                n = idx_ref[0, 0, r]
                pltpu.make_async_copy(hm_hbm.at[pl.ds(n, 1)], buf_a.at[pl.ds(r, 1)], sem.at[0]).start()
            return c
        lax.fori_loop(0, rows // MOE_ISSUE_UNROLL, body, 0)

    def step(cur, k_cur, nxt, k_nxt):
        buffer_wait(cur, k_cur)

        @pl.when(s < nsteps)
        def _():
            for r in range(rows):
                n = idxn_ref[0, 0, r]
                pltpu.make_async_copy(hm_hbm.at[pl.ds(n, 1)], nxt.at[pl.ds(r, 1)], sem.at[k_nxt]).start()

        xe = cur[...].astype(BF16)
        hid_g = _dot(xe, wg_ref[0])
        hid = (hid_g * _sigmoid(hid_g)) * _dot(xe, wu_ref[0])
        y = _dot(hid.astype(BF16), wd_ref[0])
        eye = (lax.broadcasted_iota(I32, (LANES, LANES), 0) == lax.broadcasted_iota(I32, (LANES, LANES), 1))
        for c in range(rows // LANES):
            g_row = gate_ref[0, :, c * LANES:(c + 1) * LANES]
            g_col = jnp.sum(jnp.where(eye, g_row, 0.0), axis=1, keepdims=True)
            sl = slice(c * LANES, (c + 1) * LANES)
            ye_ref[sl, :] = (y[sl, :] * g_col).astype(BF16)

        @pl.when(s == nsteps - 1)
        def _():
            buffer_wait(nxt, k_nxt)

    @pl.when(lax.rem(s, 2) == 0)
    def _():
        step(buf_a, 0, buf_b, 1)

    @pl.when(lax.rem(s, 2) == 1)
    def _():
        step(buf_b, 1, buf_a, 0)


def _moe_ffn(idx, gate, hm, wg, wu, wd, rows=512):
    e, nt, _ = idx.shape
    cap = nt * LANES
    n, d = hm.shape
    nblk = cap // rows
    nsteps = e * nblk
    idx3 = idx.reshape(nsteps, 1, rows)
    gate3 = gate.reshape(nsteps, 1, rows)
    ff = wg.shape[-1]
    return pl.pallas_call(
        functools.partial(_moe_kernel, rows=rows),
        grid=(e, nblk),
        in_specs=[pl.BlockSpec((1, 1, rows), lambda ei, j: (ei * nblk + j, 0, 0), memory_space=pltpu.SMEM),
                  pl.BlockSpec((1, 1, rows), lambda ei, j: (jnp.minimum(ei * nblk + j + 1, nsteps - 1), 0, 0),
                               memory_space=pltpu.SMEM),
                  pl.BlockSpec((1, 1, rows), lambda ei, j: (ei * nblk + j, 0, 0)),
                  pl.BlockSpec(memory_space=pl.ANY),
                  pl.BlockSpec((1, d, ff), lambda ei, j: (ei, 0, 0)),
                  pl.BlockSpec((1, d, ff), lambda ei, j: (ei, 0, 0)),
                  pl.BlockSpec((1, ff, d), lambda ei, j: (ei, 0, 0))],
        out_specs=pl.BlockSpec((rows, d), lambda ei, j: (ei * nblk + j, 0)),
        out_shape=jax.ShapeDtypeStruct((e * cap, d), BF16),
        scratch_shapes=[pltpu.VMEM((rows, d), F32), pltpu.VMEM((rows, d), F32), pltpu.SemaphoreType.DMA((2,))],
        compiler_params=_cparams(("arbitrary", "arbitrary")),
        name="moe_ffn",
    )(idx3, idx3, gate3, hm, wg, wu, wd)


COMBINE_WIN = 64


def _ple_final_kernel(lo_ref, x_ref, p_ref, pos_ref, ye_hbm, gple_ref, wpg_ref, wpp_ref, gfin_ref, o_ref,
                      win_a, win_b, acc_ref, sem, *, cap, rows_per_tile):
    i = pl.program_id(0)
    ntiles = pl.num_programs(0)
    win = COMBINE_WIN
    tm = x_ref.shape[0]

    def tile_floors(ti):
        return [jnp.bitwise_and(lo_ref[e, ti * rows_per_tile], -BF16_ROWS) for e in range(N_EXPERTS)]

    def clamp(floors):
        return [jnp.minimum(f, cap - win) for f in floors]

    def start_fetch(floors, buf, k):
        starts = clamp(floors)
        for e in range(N_EXPERTS):
            src = ye_hbm.at[pl.ds(pl.multiple_of(e * cap + starts[e], BF16_ROWS), win)]
            pltpu.make_async_copy(src, buf.at[pl.ds(e * win, win)], sem.at[k]).start()

    def wait_fetch(buf, k):
        pltpu.make_async_copy(ye_hbm.at[pl.ds(0, N_EXPERTS * win)], buf, sem.at[k]).wait()

    def expand(floors, buf):
        starts = clamp(floors)
        pos = pos_ref[...]
        w_io = lax.broadcasted_iota(I32, (win, tm), 0)
        parts = []
        for e in range(N_EXPERTS):
            pe = pos[e:e + 1, :]
            hit = ((pe - starts[e]) == w_io) & (pe >= floors[e])
            parts.append(jnp.where(hit, 1.0, 0.0).astype(BF16))
        onehot_t = jnp.concatenate(parts, axis=0)
        return _dot_tn(onehot_t, buf[...])

    @pl.when(i == 0)
    def _():
        start_fetch(tile_floors(0), win_a, 0)

    def step(cur, k_cur, nxt, k_nxt):
        floors0 = tile_floors(i)
        wait_fetch(cur, k_cur)

        @pl.when(i + 1 < ntiles)
        def _():
            start_fetch(tile_floors(i + 1), nxt, k_nxt)

        acc_ref[...] = x_ref[...] + expand(floors0, cur)

        nrounds = jnp.int32(1)
        for e in range(N_EXPERTS):
            hi = lo_ref[e, (i + 1) * rows_per_tile]
            nrounds = jnp.maximum(nrounds, lax.div(hi - floors0[e] + (win - 1), jnp.int32(win)))

        def extra_round(r, c):
            floors = [f + r * win for f in floors0]
            start_fetch(floors, cur, k_cur)
            wait_fetch(cur, k_cur)
            acc_ref[...] += expand(floors, cur)
            return c

        lax.fori_loop(1, nrounds, extra_round, 0)

        x = acc_ref[...]
        hp = _rms(x, gple_ref[...]).astype(BF16)
        gate = _sigmoid(_dot(hp, wpg_ref[...]))
        y = x + gate * _dot(p_ref[...].astype(BF16), wpp_ref[...])
        o_ref[...] = _rms(y, gfin_ref[...])

    @pl.when(lax.rem(i, 2) == 0)
    def _():
        step(win_a, 0, win_b, 1)

    @pl.when(lax.rem(i, 2) == 1)
    def _():
        step(win_b, 1, win_a, 0)


def _ple_final(x1, p2d, pos, lo_tab, ye, cap, w, tm=256):
    n, d = x1.shape
    pd = p2d.shape[-1]
    row = lambda wd: pl.BlockSpec((tm, wd), lambda i, lo: (i, 0))
    consts = [w["gple"], w["wpg"], w["wpp"], w["gfin"]]
    const_spec = lambda c: pl.BlockSpec(c.shape, lambda i, lo: (0,) * c.ndim)
    grid_spec = pltpu.PrefetchScalarGridSpec(
        num_scalar_prefetch=1,
        grid=(n // tm,),
        in_specs=[row(d), row(pd), pl.BlockSpec((N_EXPERTS, tm), lambda i, lo: (0, i)),
                  pl.BlockSpec(memory_space=pl.ANY)] + [const_spec(c) for c in consts],
        out_specs=row(d),
        scratch_shapes=[pltpu.VMEM((N_EXPERTS * COMBINE_WIN, d), BF16),
                        pltpu.VMEM((N_EXPERTS * COMBINE_WIN, d), BF16), pltpu.VMEM((tm, d), F32),
                        pltpu.SemaphoreType.DMA((2,))],
    )
    return pl.pallas_call(
        functools.partial(_ple_final_kernel, cap=cap, rows_per_tile=tm // LANES),
        grid_spec=grid_spec,
        out_shape=jax.ShapeDtypeStruct((n, d), F32),
        compiler_params=_cparams(("arbitrary",)),
        name="ple_final",
    )(lo_tab, x1, p2d, pos, ye, *consts)


def _rope_tables(T):
    half = HEAD_DIM // 2
    inv = ROPE_THETA ** (-jnp.arange(0, half, 2, dtype=F32) / half)
    t = jnp.arange(T)
    ang_r = (t // GRID_W).astype(F32)[:, None] * inv
    ang_c = (t % GRID_W).astype(F32)[:, None] * inv
    cos = jnp.concatenate([jnp.cos(ang_r)] * 2 + [jnp.cos(ang_c)] * 2, axis=1)
    sin = jnp.concatenate([-jnp.sin(ang_r), jnp.sin(ang_r), -jnp.sin(ang_c), jnp.sin(ang_c)], axis=1)
    return jnp.tile(cos, (1, N_Q_HEADS)), jnp.tile(sin, (1, N_Q_HEADS))


def _block_diag_ones(width):
    i = np.arange(width)
    return jnp.asarray((i[:, None] // HEAD_DIM) == (i[None, :] // HEAD_DIM), BF16)


def _prep_weights(g_mix, w_in, q_norm, k_norm, w_af, b_af, w_ab, b_ab, gla_norm, w_attn_up, w_gla_up,
                  w_out, g_moe, w_router, w_gate_e, w_up_e, w_down_e, g_ple, w_ple_gate, w_ple_proj,
                  g_final):
    d = w_in.shape[0]
    o_gla = ATTN_W + 2 * KV_W
    o_lr = o_gla + 2 * GLA_KW + 2 * GLA_VW
    o_ga = o_lr + 2 * GLA_LOWRANK
    wlr = jnp.zeros((d, LANES), F32).at[:, :2 * GLA_LOWRANK].set(w_in[:, o_lr:o_ga])
    wgate = jnp.zeros((LANES, 2 * GLA_KW), F32)
    wgate = wgate.at[:GLA_LOWRANK, :GLA_KW].set(w_af)
    wgate = wgate.at[GLA_LOWRANK:2 * GLA_LOWRANK, GLA_KW:].set(w_ab)
    wr_t = w_router.T
    wr_hi = wr_t.astype(BF16)
    wr_lo = (wr_t - wr_hi.astype(F32)).astype(BF16)
    return dict(
        gmix=g_mix.reshape(1, d),
        wqk=w_in[:, :ATTN_W + KV_W].astype(BF16),
        wvt=w_in[:, ATTN_W + KV_W:o_gla].T.astype(BF16),
        wgla=w_in[:, o_gla:o_lr].astype(BF16),
        wlr=wlr.astype(BF16),
        wga=w_in[:, o_ga:o_ga + d].astype(BF16),
        wgl=w_in[:, o_ga + d:].astype(BF16),
        wgate=wgate.astype(BF16),
        bgate=jnp.concatenate([b_af, b_ab]).reshape(1, 2 * GLA_KW),
        qn=jnp.tile(q_norm, N_Q_HEADS).reshape(1, ATTN_W),
        kn=jnp.tile(k_norm, N_KV_HEADS).reshape(1, KV_W),
        bdq=_block_diag_ones(ATTN_W),
        bdk=_block_diag_ones(KV_W),
        gnorm=gla_norm.reshape(1, GLA_DV),
        wau=w_attn_up.astype(BF16),
        wgu=w_gla_up.astype(BF16),
        wout=w_out.astype(BF16),
        gmoe=g_moe.reshape(1, d),
        wr_hi=wr_hi,
        wr_lo=wr_lo,
        wge=w_gate_e.astype(BF16),
        wue=w_up_e.astype(BF16),
        wde=w_down_e.astype(BF16),
        gple=g_ple.reshape(1, d),
        wpg=w_ple_gate.astype(BF16),
        wpp=w_ple_proj.astype(BF16),
        gfin=g_final.reshape(1, d),
    )


def _trunk(x, p, w):
    b, t, d = x.shape
    n = b * t
    x2d = x.reshape(n, d)
    q, k, vt, lq, lk, lv, lr, gf, gb = _in_proj(x2d, t, _rope_tables(t), w)
    r3 = lambda a: a.reshape(b, t, a.shape[-1])
    o_attn = _attention(r3(q), r3(k), vt).reshape(n, ATTN_W)
    o_gla = _gla(r3(lq), r3(lk), r3(lv), r3(lr), r3(gf), r3(gb), w["gnorm"]).reshape(n, GLA_VW)
    x1, hm, aff_t = _post_mix(x2d, o_attn, o_gla, w)
    cap = EC_CAPACITY * n // N_EXPERTS
    idx, gate, pos, lo_tab = _route(aff_t, cap)
    ye = _moe_ffn(idx, gate, hm, w["wge"], w["wue"], w["wde"])
    y = _ple_final(x1, p.reshape(n, p.shape[-1]), pos, lo_tab, ye, cap, w)
    return y.reshape(b, t, d)


def kernel(x_prompt, x_sample, p_prompt, p_sample, g_mix, w_in, q_norm, k_norm, w_af, b_af, w_ab, b_ab,
           gla_norm, w_attn_up, w_gla_up, w_out, g_moe, w_router, w_gate_e, w_up_e, w_down_e, g_ple,
           w_ple_gate, w_ple_proj, g_final):
    assert g_mix.shape[0] == 1, "single layer"
    w = _prep_weights(g_mix[0], w_in[0], q_norm[0], k_norm[0], w_af[0], b_af[0], w_ab[0], b_ab[0],
                      gla_norm[0], w_attn_up[0], w_gla_up[0], w_out[0], g_moe[0], w_router[0],
                      w_gate_e[0], w_up_e[0], w_down_e[0], g_ple[0], w_ple_gate[0], w_ple_proj[0], g_final)
    return (_trunk(x_prompt, p_prompt[0], w), _trunk(x_sample, p_sample[0], w))
```

```python
import functools

import numpy as np
import jax
import jax.numpy as jnp
from jax import lax
from jax.experimental import pallas as pl
from jax.experimental.pallas import tpu as pltpu

F32 = jnp.float32
BF16 = jnp.bfloat16
I32 = jnp.int32

HEAD_DIM = 64
N_Q_HEADS = 8
N_KV_HEADS = 2
GRID_W = 64
ROPE_THETA = 10000.0
GLA_HEADS = 4
GLA_DK = 64
GLA_DV = 128
GLA_LOWRANK = 16
GLA_GATE_NORM = 16.0
N_EXPERTS = 16
EC_CAPACITY = 2
NORM_EPS = 1e-6
LOG2E = 1.4426950408889634

ATTN_W = N_Q_HEADS * HEAD_DIM
KV_W = N_KV_HEADS * HEAD_DIM
GLA_KW = GLA_HEADS * GLA_DK
GLA_VW = GLA_HEADS * GLA_DV

LANES = 128
SUBLANES = 8
BF16_ROWS = 16
VMEM_LIMIT = 56 * 1024 * 1024

GLA_CHUNK = 128
GLA_LEVELS = 7
GLA_STEP_CHUNKS = 4


def _cparams(sem):
    return pltpu.CompilerParams(dimension_semantics=sem, vmem_limit_bytes=VMEM_LIMIT)


def _dot(a, b):
    return jnp.dot(a, b, preferred_element_type=F32)


def _dot_nt(a, b):
    return lax.dot_general(a, b, (((1,), (1,)), ((), ())), preferred_element_type=F32)


def _dot_tn(a, b):
    return lax.dot_general(a, b, (((0,), (0,)), ((), ())), preferred_element_type=F32)


def _split2(x):
    hi = x.astype(BF16)
    lo = (x - hi.astype(F32)).astype(BF16)
    return hi, lo


def _split3(x):
    hi = x.astype(BF16)
    r = x - hi.astype(F32)
    mid = r.astype(BF16)
    lo = (r - mid.astype(F32)).astype(BF16)
    return hi, mid, lo


def _rms(x, gain):
    ms = jnp.mean(x * x, axis=-1, keepdims=True)
    return x * lax.rsqrt(ms + NORM_EPS) * gain


def _sigmoid(x):
    return 1.0 / (1.0 + jnp.exp(-x))


def _full_spec(shape):
    nd = len(shape)
    return pl.BlockSpec(shape, lambda *_: (0,) * nd)


def _head_ms(x, bd):
    hi, lo = _split2(x * x)
    return (_dot(hi, bd) + _dot(lo, bd)) * (1.0 / HEAD_DIM)


def _rope(x, cos, sin):
    w = x.shape[-1]
    lane = lax.broadcasted_iota(I32, x.shape, 1)
    fwd = pltpu.roll(x, w - 16, axis=1)
    bwd = pltpu.roll(x, 16, axis=1)
    partner = jnp.where((lane % 32) < 16, fwd, bwd)
    return x * cos + partner * sin


def _in_proj_kernel(x_ref, cos_ref, sin_ref, gmix_ref, wqk_ref, wvt_ref, wgla_ref, wlr_ref,
                    wgate_ref, bgate_ref, qn_ref, kn_ref, bdq_ref, bdk_ref,
                    q_ref, k_ref, vt_ref, lq_ref, lk_ref, lv_ref, lr_ref, gf_ref, gb_ref):
    x = x_ref[...]
    h = _rms(x, gmix_ref[...]).astype(BF16)

    qk = _dot(h, wqk_ref[...])
    cos = cos_ref[...]
    sin = sin_ref[...]
    q = qk[:, :ATTN_W]
    q = q * lax.rsqrt(_head_ms(q, bdq_ref[...]) + NORM_EPS) * qn_ref[...]
    q = _rope(q, cos, sin) * (HEAD_DIM ** -0.5 * LOG2E)
    q_ref[...] = q.astype(BF16)
    k = qk[:, ATTN_W:]
    k = k * lax.rsqrt(_head_ms(k, bdk_ref[...]) + NORM_EPS) * kn_ref[...]
    k = _rope(k, cos[:, :KV_W], sin[:, :KV_W])
    k_ref[...] = k.astype(BF16)
    vt_ref[0] = _dot_nt(wvt_ref[...], h).astype(BF16)

    gla = _dot(h, wgla_ref[...])
    lq_ref[...] = gla[:, :GLA_KW] * (GLA_DK ** -0.5)
    lk_ref[...] = gla[:, GLA_KW:2 * GLA_KW]
    lv_ref[...] = gla[:, 2 * GLA_KW:2 * GLA_KW + GLA_VW].astype(BF16)
    lr_ref[...] = gla[:, 2 * GLA_KW + GLA_VW:]

    a = _dot(h, wlr_ref[...]).astype(BF16)
    z = _dot(a, wgate_ref[...]) + bgate_ref[...]
    g = (jnp.minimum(z, 0.0) - jnp.log(1.0 + jnp.exp(-jnp.abs(z)))) * (1.0 / GLA_GATE_NORM)
    gf_ref[...] = g[:, :GLA_KW]
    gb_ref[...] = g[:, GLA_KW:]


def _in_proj(x2d, T, tabs, w, tm=512):
    n, d = x2d.shape
    nt = T // tm
    row = lambda wd: pl.BlockSpec((tm, wd), lambda i: (i, 0))
    tab = pl.BlockSpec((tm, ATTN_W), lambda i: (i % nt, 0))
    consts = [w["gmix"], w["wqk"], w["wvt"], w["wgla"], w["wlr"], w["wgate"], w["bgate"],
              w["qn"], w["kn"], w["bdq"], w["bdk"]]
    outs = [(ATTN_W, BF16), (KV_W, BF16), None, (GLA_KW, F32), (GLA_KW, F32), (GLA_VW, BF16),
            (GLA_VW, F32), (GLA_KW, F32), (GLA_KW, F32)]
    vt_spec = pl.BlockSpec((1, KV_W, tm), lambda i: (i // nt, 0, i % nt))
    vt_shape = jax.ShapeDtypeStruct((n // T, KV_W, T), BF16)
    return pl.pallas_call(
        _in_proj_kernel,
        grid=(n // tm,),
        in_specs=[row(d), tab, tab] + [_full_spec(c.shape) for c in consts],
        out_specs=[vt_spec if o is None else row(o[0]) for o in outs],
        out_shape=[vt_shape if o is None else jax.ShapeDtypeStruct((n, o[0]), o[1]) for o in outs],
        compiler_params=_cparams(("parallel",)),
        name="in_proj",
    )(x2d, tabs[0], tabs[1], *consts)


def _attn_kernel(q_ref, k_ref, vt_ref, o_ref, qpad_ref, m_ref, acc_ref, s_even, s_odd, mx_even, mx_odd, *, nk):
    j = pl.program_id(2)
    group = N_Q_HEADS // N_KV_HEADS
    even = (s_even, mx_even)
    odd = (s_odd, mx_odd)

    def head_scores(dst, hq):
        s = _dot(k_ref[0], qpad_ref[hq])
        dst[0][hq] = s
        dst[1][hq] = jnp.max(s, axis=0, keepdims=True)

    def head_softmax(src, hq):
        m_old = m_ref[hq]
        m_new = jnp.maximum(m_old, src[1][hq])
        alpha = jnp.exp2(m_old - m_new)
        p = jnp.exp2(src[0][hq] - m_new)
        m_ref[hq] = m_new
        return p.astype(BF16), alpha

    def head_values(hq, p, alpha):
        hk = hq // group
        ones = jnp.ones((BF16_ROWS, vt_ref.shape[2]), BF16)
        vt = vt_ref[0, hk * HEAD_DIM:(hk + 1) * HEAD_DIM, :]
        vt1 = jnp.concatenate([vt, ones], axis=0)
        acc_ref[hq] = alpha * acc_ref[hq] + _dot(vt1, p)

    def scores_into(dst):
        for hq in range(N_Q_HEADS):
            head_scores(dst, hq)

    def consume(src, dst=None):
        pending = None
        for hq in range(N_Q_HEADS):
            if dst is not None:
                head_scores(dst, hq)
            p, alpha = head_softmax(src, hq)
            if pending is not None:
                head_values(*pending)
            pending = (hq, p, alpha)
        head_values(*pending)

    @pl.when(j == 0)
    def _():
        m_ref[...] = jnp.full(m_ref.shape, -jnp.inf, F32)
        acc_ref[...] = jnp.zeros(acc_ref.shape, F32)
        lane = lax.broadcasted_iota(I32, (q_ref.shape[1], LANES), 1)
        for hq in range(N_Q_HEADS):
            hk = hq // group
            chunk = q_ref[0, :, (hq // 2) * LANES:(hq // 2 + 1) * LANES].astype(F32)
            if hq % 2 != hk:
                chunk = pltpu.roll(chunk, HEAD_DIM, axis=1)
            keep = (lane // HEAD_DIM) == hk
            qpad_ref[hq] = jnp.where(keep, chunk, 0.0).T.astype(BF16)
        scores_into(even)

    @pl.when((j > 0) & (j < nk) & (lax.rem(j, 2) == 1))
    def _():
        consume(even, odd)

    @pl.when((j > 0) & (j < nk) & (lax.rem(j, 2) == 0))
    def _():
        consume(odd, even)

    @pl.when(j == nk)
    def _():
        consume(odd if nk % 2 == 0 else even)
        outs = [acc_ref[hq, :HEAD_DIM, :] / acc_ref[hq, HEAD_DIM:HEAD_DIM + 1, :] for hq in range(N_Q_HEADS)]
        o_ref[0] = jnp.concatenate(outs, axis=0).T.astype(BF16)


def _attention(q, k, vt, tq=1024, tk=512):
    b, t, _ = q.shape
    nk = t // tk
    s_buf = pltpu.VMEM((N_Q_HEADS, tk, tq), F32)
    mx_buf = pltpu.VMEM((N_Q_HEADS, 1, tq), F32)
    return pl.pallas_call(
        functools.partial(_attn_kernel, nk=nk),
        grid=(b, t // tq, nk + 1),
        in_specs=[pl.BlockSpec((1, tq, ATTN_W), lambda bi, i, j: (bi, i, 0)),
                  pl.BlockSpec((1, tk, KV_W), lambda bi, i, j: (bi, jnp.minimum(j, nk - 1), 0)),
                  pl.BlockSpec((1, KV_W, tk), lambda bi, i, j: (bi, 0, jnp.maximum(j - 1, 0)))],
        out_specs=pl.BlockSpec((1, tq, ATTN_W), lambda bi, i, j: (bi, i, 0)),
        out_shape=jax.ShapeDtypeStruct((b, t, ATTN_W), BF16),
        scratch_shapes=[pltpu.VMEM((N_Q_HEADS, LANES, tq), BF16),
                        pltpu.VMEM((N_Q_HEADS, 1, tq), F32),
                        pltpu.VMEM((N_Q_HEADS, HEAD_DIM + BF16_ROWS, tq), F32),
                        s_buf, s_buf, mx_buf, mx_buf],
        compiler_params=_cparams(("parallel", "parallel", "arbitrary")),
        name="attention",
    )(q, k, vt)


def _gla_chunk(q, k, v, g, st_ref, mask_ref, reverse):
    c = GLA_CHUNK
    pair_w = 2 * GLA_DK
    t_kw = lax.broadcasted_iota(I32, (c, GLA_KW), 0)
    lo_lanes = lax.broadcasted_iota(I32, (c, pair_w), 1) < GLA_DK

    a_pairs = [jnp.zeros((c, 2 * c), F32) for _ in range(2)]

    def add_level(a_pairs, qh, kh, mask2):
        qb = qh.astype(BF16)
        kb = kh.astype(BF16)
        out = []
        for p in range(2):
            qp = qb[:, p * pair_w:(p + 1) * pair_w]
            kp = kb[:, p * pair_w:(p + 1) * pair_w]
            k2 = jnp.concatenate([jnp.where(lo_lanes, kp, 0.0), jnp.where(lo_lanes, 0.0, kp)], axis=0)
            out.append(a_pairs[p] + mask2 * _dot_nt(qp, k2))
        return out

    pre = g
    tot = g
    for lvl in range(GLA_LEVELS):
        s = 1 << lvl
        odd = ((t_kw >> lvl) & 1) == 1
        if not reverse:
            qh = q * jnp.exp(pre)
            kh = k * jnp.exp(tot - pre)
        else:
            qh = q * jnp.exp(tot - pre + g)
            kh = k * jnp.exp(pre - g)
        a_pairs = add_level(a_pairs, qh, kh, mask_ref[lvl])
        sib = jnp.where(odd, pltpu.roll(tot, s, axis=0), pltpu.roll(tot, c - s, axis=0))
        pre = pre + jnp.where(odd, sib, 0.0)
        tot = tot + sib
    if not reverse:
        a_pairs = add_level(a_pairs, q, k, mask_ref[GLA_LEVELS])

    if not reverse:
        q_dec = q * jnp.exp(pre)
        k_dec = k * jnp.exp(tot - pre)
    else:
        q_dec = q * jnp.exp(tot - pre + g)
        k_dec = k * jnp.exp(pre - g)
    chunk_decay = jnp.exp(tot[0:1, :])
    sub_pair = lax.broadcasted_iota(I32, (2 * GLA_DV, pair_w), 0)
    lane_st = lax.broadcasted_iota(I32, (2 * GLA_DV, pair_w), 1)
    block_diag = (sub_pair // GLA_DV) == (lane_st // GLA_DK)
    first_head_v = lax.broadcasted_iota(I32, (c, 2 * GLA_DV), 1) < GLA_DV

    outs = []
    for p in range(2):
        st = st_ref[p]
        qd = q_dec[:, p * pair_w:(p + 1) * pair_w].astype(BF16)
        o_pair = _dot_nt(qd, st.astype(BF16))
        vp = v[:, p * 2 * GLA_DV:(p + 1) * 2 * GLA_DV].astype(BF16)
        kd = k_dec[:, p * pair_w:(p + 1) * pair_w].astype(BF16)
        upd = _dot_tn(vp, kd)
        st_ref[p] = st * chunk_decay[:, p * pair_w:(p + 1) * pair_w] + jnp.where(block_diag, upd, 0.0)
        v_diag = jnp.concatenate([jnp.where(first_head_v, vp, 0.0), jnp.where(first_head_v, 0.0, vp)], axis=0)
        outs.append(o_pair + _dot(a_pairs[p].astype(BF16), v_diag))
    return jnp.concatenate(outs, axis=1)


def _gla_level_masks(reverse):
    c = GLA_CHUNK
    t = np.arange(c)[:, None]
    s = np.arange(c)[None, :]
    masks = []
    for lvl in range(GLA_LEVELS):
        bt, bs = t >> lvl, s >> lvl
        masks.append((bs == bt + 1) & (bt % 2 == 0) if reverse else (bt == bs + 1) & (bs % 2 == 0))
    masks.append(t == s)
    m = np.stack(masks).astype(np.float32)
    return jnp.asarray(np.concatenate([m, m], axis=2))


def _gla_bwd_kernel(q_ref, k_ref, v_ref, g_ref, mask_ref, o_ref, st_ref):
    @pl.when(pl.program_id(1) == 0)
    def _():
        st_ref[...] = jnp.zeros(st_ref.shape, F32)

    for sub in reversed(range(GLA_STEP_CHUNKS)):
        rows = slice(sub * GLA_CHUNK, (sub + 1) * GLA_CHUNK)
        o_ref[0, rows, :] = _gla_chunk(q_ref[0, rows, :], k_ref[0, rows, :], v_ref[0, rows, :], g_ref[0, rows, :],
                                       st_ref, mask_ref, True)


def _gla_fwd_kernel(q_ref, k_ref, v_ref, g_ref, ob_ref, r_ref, gn_ref, mask_ref, o_ref, st_ref):
    @pl.when(pl.program_id(1) == 0)
    def _():
        st_ref[...] = jnp.zeros(st_ref.shape, F32)

    gn = gn_ref[...]
    for sub in range(GLA_STEP_CHUNKS):
        rows = slice(sub * GLA_CHUNK, (sub + 1) * GLA_CHUNK)
        o = _gla_chunk(q_ref[0, rows, :], k_ref[0, rows, :], v_ref[0, rows, :], g_ref[0, rows, :],
                       st_ref, mask_ref, False) + ob_ref[0, rows, :]
        r = r_ref[0, rows, :]
        gate = r * _sigmoid(r)
        outs = []
        for h in range(GLA_HEADS):
            oh = o[:, h * GLA_DV:(h + 1) * GLA_DV]
            outs.append(_rms(oh, gn))
        o_ref[0, rows, :] = (jnp.concatenate(outs, axis=1) * gate).astype(BF16)


def _gla(lq, lk, lv, lr, gf, gb, gla_norm):
    b, t, _ = lq.shape
    c = GLA_CHUNK * GLA_STEP_CHUNKS
    nc = t // c
    fwd = lambda wd: pl.BlockSpec((1, c, wd), lambda bi, ci: (bi, ci, 0))
    rev = lambda wd: pl.BlockSpec((1, c, wd), lambda bi, ci: (bi, nc - 1 - ci, 0))
    st = pltpu.VMEM((2, 2 * GLA_DV, 2 * GLA_DK), F32)
    mask_b = _gla_level_masks(True)
    mask_f = _gla_level_masks(False)
    o_b = pl.pallas_call(
        _gla_bwd_kernel,
        grid=(b, nc),
        in_specs=[rev(GLA_KW), rev(GLA_KW), rev(GLA_VW), rev(GLA_KW), _full_spec(mask_b.shape)],
        out_specs=rev(GLA_VW),
        out_shape=jax.ShapeDtypeStruct((b, t, GLA_VW), F32),
        scratch_shapes=[st],
        compiler_params=_cparams(("parallel", "arbitrary")),
        name="gla_bwd",
    )(lq, lk, lv, gb, mask_b)
    return pl.pallas_call(
        _gla_fwd_kernel,
        grid=(b, nc),
        in_specs=[fwd(GLA_KW), fwd(GLA_KW), fwd(GLA_VW), fwd(GLA_KW), fwd(GLA_VW), fwd(GLA_VW),
                  _full_spec(gla_norm.shape), _full_spec(mask_f.shape)],
        out_specs=fwd(GLA_VW),
        out_shape=jax.ShapeDtypeStruct((b, t, GLA_VW), BF16),
        scratch_shapes=[st],
        compiler_params=_cparams(("parallel", "arbitrary")),
        name="gla_fwd",
    )(lq, lk, lv, gf, o_b, lr, gla_norm, mask_f)


def _post_mix_kernel(x_ref, oa_ref, og_ref, gmix_ref, wga_ref, wgl_ref, wau_ref, wgu_ref, wout_ref, gmoe_ref,
                     wr_hi_ref, wr_lo_ref, x1_ref, hm_ref, aff_ref):
    x = x_ref[...]
    h = _rms(x, gmix_ref[...]).astype(BF16)
    merged = (_sigmoid(_dot(h, wga_ref[...])) * _dot(oa_ref[...], wau_ref[...])
              + _sigmoid(_dot(h, wgl_ref[...])) * _dot(og_ref[...], wgu_ref[...]))
    x1 = x + _dot(merged.astype(BF16), wout_ref[...])
    x1_ref[...] = x1
    hm = _rms(x1, gmoe_ref[...])
    hm_ref[...] = hm
    h_hi, h_lo = _split2(hm)
    wr_hi = wr_hi_ref[...]
    both = _dot_nt(jnp.concatenate([wr_hi, wr_lo_ref[...]], axis=0), h_hi)
    logits = both[:N_EXPERTS] + both[N_EXPERTS:] + _dot_nt(wr_hi, h_lo)
    m = jnp.max(logits, axis=0, keepdims=True)
    e = jnp.exp(logits - m)
    aff_ref[...] = e / jnp.sum(e, axis=0, keepdims=True)


def _post_mix(x2d, oa, og, w, tm=512):
    n, d = x2d.shape
    row = lambda wd: pl.BlockSpec((tm, wd), lambda i: (i, 0))
    consts = [w["gmix"], w["wga"], w["wgl"], w["wau"], w["wgu"], w["wout"], w["gmoe"], w["wr_hi"], w["wr_lo"]]
    return pl.pallas_call(
        _post_mix_kernel,
        grid=(n // tm,),
        in_specs=[row(d), row(ATTN_W), row(GLA_VW)] + [_full_spec(c.shape) for c in consts],
        out_specs=[row(d), row(d), pl.BlockSpec((N_EXPERTS, tm), lambda i: (0, i))],
        out_shape=[jax.ShapeDtypeStruct((n, d), F32), jax.ShapeDtypeStruct((n, d), F32),
                   jax.ShapeDtypeStruct((N_EXPERTS, n), F32)],
        compiler_params=_cparams(("parallel",)),
        name="post_mix",
    )(x2d, oa, og, *consts)


ROUTE_SLOTS = 256


def _route_kernel(a_ref, idx_ref, gate_ref, pos_ref, lo_ref, *, cap):
    a = a_ref[0]
    nrow = a.shape[0]
    b_io = lax.broadcasted_iota(I32, (nrow, LANES), 0)
    t_io = lax.broadcasted_iota(I32, (nrow, LANES), 1)
    tok = b_io * LANES + t_io

    def as_float(word):
        return lax.bitcast_convert_type(jnp.full((1, LANES), word, I32), F32)

    def thr_step(i, thr):
        cand = thr | lax.shift_left(jnp.int32(1), jnp.int32(30) - i)
        cnt = jnp.sum((a >= as_float(cand)).astype(I32))
        return jnp.where(cnt >= cap, cand, thr)

    thr = lax.fori_loop(0, 31, thr_step, jnp.int32(0))
    gt = a >= as_float(thr + 1)
    eq = (a >= as_float(thr)) & jnp.logical_not(gt)
    need = cap - jnp.sum(gt.astype(I32))

    def tie_step(i, lim):
        cand = lim + lax.shift_left(jnp.int32(1), jnp.int32(17) - i)
        cnt = jnp.sum((eq & (tok < cand)).astype(I32))
        return jnp.where(cnt < need, cand, lim)

    lim = lax.fori_loop(0, 18, tie_step, jnp.int32(0))
    selm = gt | (eq & (tok <= lim))
    selb = selm.astype(F32).astype(BF16)

    r_io = lax.broadcasted_iota(I32, (LANES, LANES), 0)
    c_io = lax.broadcasted_iota(I32, (LANES, LANES), 1)
    utri = (r_io <= c_io).astype(BF16)
    cum_row = _dot(selb, utri)
    row_cnt = cum_row[:, LANES - 1:LANES]
    br_io = lax.broadcasted_iota(I32, (nrow, nrow), 0)
    bc_io = lax.broadcasted_iota(I32, (nrow, nrow), 1)
    lrow = (bc_io <= br_io).astype(BF16)
    row_incl = _dot(lrow, jnp.broadcast_to(row_cnt, (nrow, LANES)).astype(BF16))
    row_excl = row_incl - row_cnt
    pos_ref[0] = jnp.where(selm, row_excl + cum_row - 1.0, -1.0).astype(I32)
    lo_ref[0] = row_excl.astype(I32)
    row_incl_c = row_incl[:, 0:1]
    row_excl_c = row_excl[:, 0:1]
    cum_t = cum_row.T.astype(BF16)
    a_hi, a_mid, a_lo = _split3(a.T)
    sw = ROUTE_SLOTS
    brow = lax.broadcasted_iota(I32, (nrow, sw), 0).astype(F32)
    tsub = lax.broadcasted_iota(I32, (LANES, sw), 0).astype(F32)
    lane = lax.broadcasted_iota(I32, (1, sw), 1).astype(F32)

    def slot_tile(jt, carry):
        j = lane + jnp.asarray(jt, F32) * float(sw)
        bj = jnp.sum((row_incl_c <= j).astype(F32), axis=0, keepdims=True)
        onehot = (brow == bj).astype(F32)
        base = jnp.sum(onehot * row_excl_c, axis=0, keepdims=True)
        jl = j - base
        ohb = onehot.astype(BF16)
        rt = _dot(cum_t, ohb)
        tj = jnp.sum((rt <= jl).astype(F32), axis=0, keepdims=True)
        acol = _dot(a_hi, ohb) + _dot(a_mid, ohb) + _dot(a_lo, ohb)
        gate = jnp.sum(jnp.where(tsub == tj, acol, 0.0), axis=0, keepdims=True)
        tok_id = (bj * float(LANES) + tj).astype(I32)
        for c in range(sw // LANES):
            row = jt * (sw // LANES) + c
            idx_ref[0, pl.ds(row, 1), :] = tok_id[:, c * LANES:(c + 1) * LANES]
            gate_ref[0, pl.ds(row, 1), :] = gate[:, c * LANES:(c + 1) * LANES]
        return carry

    lax.fori_loop(0, cap // sw, slot_tile, 0)


def _route(aff_t, cap):
    e, n = aff_t.shape
    nrow = n // LANES
    a3 = aff_t.reshape(e, nrow, LANES)
    nt = cap // LANES
    slot_spec = pl.BlockSpec((1, nt, LANES), lambda i: (i, 0, 0))
    tok_spec = pl.BlockSpec((1, nrow, LANES), lambda i: (i, 0, 0))
    idx, gate, pos, lo = pl.pallas_call(
        functools.partial(_route_kernel, cap=cap),
        grid=(e,),
        in_specs=[tok_spec],
        out_specs=[slot_spec, slot_spec, tok_spec, tok_spec],
        out_shape=[jax.ShapeDtypeStruct((e, nt, LANES), I32), jax.ShapeDtypeStruct((e, nt, LANES), F32),
                   jax.ShapeDtypeStruct((e, nrow, LANES), I32), jax.ShapeDtypeStruct((e, nrow, LANES), I32)],
        compiler_params=_cparams(("parallel",)),
        name="route",
    )(a3)
    lo_tab = jnp.concatenate([lo[:, :, 0], jnp.full((e, 1), cap, I32)], axis=1)
    return idx, gate, pos.reshape(e, n), lo_tab


MOE_ISSUE_UNROLL = 8


def _moe_kernel(idx_ref, idxn_ref, gate_ref, hm_hbm, wg_ref, wu_ref, wd_ref, ye_ref, buf_a, buf_b, sem, *, rows):
    nblk = pl.num_programs(1)
    s = pl.program_id(0) * nblk + pl.program_id(1)
    nsteps = pl.num_programs(0) * nblk

    def buffer_wait(buf, k):
        pltpu.make_async_copy(hm_hbm.at[pl.ds(0, rows)], buf, sem.at[k]).wait()

    @pl.when(s == 0)
    def _():
        def body(r0, c):
            for u in range(MOE_ISSUE_UNROLL):
                r = r0 * MOE_ISSUE_UNROLL + u
                n = idx_ref[0, 0, r]
                pltpu.make_async_copy(hm_hbm.at[pl.ds(n, 1)], buf_a.at[pl.ds(r, 1)], sem.at[0]).start()
            return c
        lax.fori_loop(0, rows // MOE_ISSUE_UNROLL, body, 0)

    def step(cur, k_cur, nxt, k_nxt):
        buffer_wait(cur, k_cur)

        @pl.when(s < nsteps)
        def _():
            for r in range(rows):
                n = idxn_ref[0, 0, r]
                pltpu.make_async_copy(hm_hbm.at[pl.ds(n, 1)], nxt.at[pl.ds(r, 1)], sem.at[k_nxt]).start()

        xe = cur[...].astype(BF16)
        hid_g = _dot(xe, wg_ref[0])
        hid = (hid_g * _sigmoid(hid_g)) * _dot(xe, wu_ref[0])
        y = _dot(hid.astype(BF16), wd_ref[0])
        eye = (lax.broadcasted_iota(I32, (LANES, LANES), 0) == lax.broadcasted_iota(I32, (LANES, LANES), 1))
        for c in range(rows // LANES):
            g_row = gate_ref[0, :, c * LANES:(c + 1) * LANES]
            g_col = jnp.sum(jnp.where(eye, g_row, 0.0), axis=1, keepdims=True)
            sl = slice(c * LANES, (c + 1) * LANES)
            ye_ref[sl, :] = (y[sl, :] * g_col).astype(BF16)

        @pl.when(s == nsteps - 1)
        def _():
            buffer_wait(nxt, k_nxt)

    @pl.when(lax.rem(s, 2) == 0)
    def _():
        step(buf_a, 0, buf_b, 1)

    @pl.when(lax.rem(s, 2) == 1)
    def _():
        step(buf_b, 1, buf_a, 0)


def _moe_ffn(idx, gate, hm, wg, wu, wd, rows=512):
    e, nt, _ = idx.shape
    cap = nt * LANES
    n, d = hm.shape
    nblk = cap // rows
    nsteps = e * nblk
    idx3 = idx.reshape(nsteps, 1, rows)
    gate3 = gate.reshape(nsteps, 1, rows)
    ff = wg.shape[-1]
    return pl.pallas_call(
        functools.partial(_moe_kernel, rows=rows),
        grid=(e, nblk),
        in_specs=[pl.BlockSpec((1, 1, rows), lambda ei, j: (ei * nblk + j, 0, 0), memory_space=pltpu.SMEM),
                  pl.BlockSpec((1, 1, rows), lambda ei, j: (jnp.minimum(ei * nblk + j + 1, nsteps - 1), 0, 0),
                               memory_space=pltpu.SMEM),
                  pl.BlockSpec((1, 1, rows), lambda ei, j: (ei * nblk + j, 0, 0)),
                  pl.BlockSpec(memory_space=pl.ANY),
                  pl.BlockSpec((1, d, ff), lambda ei, j: (ei, 0, 0)),
                  pl.BlockSpec((1, d, ff), lambda ei, j: (ei, 0, 0)),
                  pl.BlockSpec((1, ff, d), lambda ei, j: (ei, 0, 0))],
        out_specs=pl.BlockSpec((rows, d), lambda ei, j: (ei * nblk + j, 0)),
        out_shape=jax.ShapeDtypeStruct((e * cap, d), BF16),
        scratch_shapes=[pltpu.VMEM((rows, d), F32), pltpu.VMEM((rows, d), F32), pltpu.SemaphoreType.DMA((2,))],
        compiler_params=_cparams(("arbitrary", "arbitrary")),
        name="moe_ffn",
    )(idx3, idx3, gate3, hm, wg, wu, wd)


COMBINE_WIN = 64


def _ple_final_kernel(lo_ref, x_ref, p_ref, pos_ref, ye_hbm, gple_ref, wpg_ref, wpp_ref, gfin_ref, o_ref,
                      win_a, win_b, acc_ref, sem, *, cap, rows_per_tile):
    i = pl.program_id(0)
    ntiles = pl.num_programs(0)
    win = COMBINE_WIN
    tm = x_ref.shape[0]

    def tile_floors(ti):
        return [jnp.bitwise_and(lo_ref[e, ti * rows_per_tile], -BF16_ROWS) for e in range(N_EXPERTS)]

    def clamp(floors):
        return [jnp.minimum(f, cap - win) for f in floors]

    def start_fetch(floors, buf, k):
        starts = clamp(floors)
        for e in range(N_EXPERTS):
            src = ye_hbm.at[pl.ds(pl.multiple_of(e * cap + starts[e], BF16_ROWS), win)]
            pltpu.make_async_copy(src, buf.at[pl.ds(e * win, win)], sem.at[k]).start()

    def wait_fetch(buf, k):
        pltpu.make_async_copy(ye_hbm.at[pl.ds(0, N_EXPERTS * win)], buf, sem.at[k]).wait()

    def expand(floors, buf):
        starts = clamp(floors)
        pos = pos_ref[...]
        w_io = lax.broadcasted_iota(I32, (win, tm), 0)
        parts = []
        for e in range(N_EXPERTS):
            pe = pos[e:e + 1, :]
            hit = ((pe - starts[e]) == w_io) & (pe >= floors[e])
            parts.append(jnp.where(hit, 1.0, 0.0).astype(BF16))
        onehot_t = jnp.concatenate(parts, axis=0)
        return _dot_tn(onehot_t, buf[...])

    @pl.when(i == 0)
    def _():
        start_fetch(tile_floors(0), win_a, 0)

    def step(cur, k_cur, nxt, k_nxt):
        floors0 = tile_floors(i)
        wait_fetch(cur, k_cur)

        @pl.when(i + 1 < ntiles)
        def _():
            start_fetch(tile_floors(i + 1), nxt, k_nxt)

        acc_ref[...] = x_ref[...] + expand(floors0, cur)

        nrounds = jnp.int32(1)
        for e in range(N_EXPERTS):
            hi = lo_ref[e, (i + 1) * rows_per_tile]
            nrounds = jnp.maximum(nrounds, lax.div(hi - floors0[e] + (win - 1), jnp.int32(win)))

        def extra_round(r, c):
            floors = [f + r * win for f in floors0]
            start_fetch(floors, cur, k_cur)
            wait_fetch(cur, k_cur)
            acc_ref[...] += expand(floors, cur)
            return c

        lax.fori_loop(1, nrounds, extra_round, 0)

        x = acc_ref[...]
        hp = _rms(x, gple_ref[...]).astype(BF16)
        gate = _sigmoid(_dot(hp, wpg_ref[...]))
        y = x + gate * _dot(p_ref[...].astype(BF16), wpp_ref[...])
        o_ref[...] = _rms(y, gfin_ref[...])

    @pl.when(lax.rem(i, 2) == 0)
    def _():
        step(win_a, 0, win_b, 1)

    @pl.when(lax.rem(i, 2) == 1)
    def _():
        step(win_b, 1, win_a, 0)


def _ple_final(x1, p2d, pos, lo_tab, ye, cap, w, tm=256):
    n, d = x1.shape
    pd = p2d.shape[-1]
    row = lambda wd: pl.BlockSpec((tm, wd), lambda i, lo: (i, 0))
    consts = [w["gple"], w["wpg"], w["wpp"], w["gfin"]]
    const_spec = lambda c: pl.BlockSpec(c.shape, lambda i, lo: (0,) * c.ndim)
    grid_spec = pltpu.PrefetchScalarGridSpec(
        num_scalar_prefetch=1,
        grid=(n // tm,),
        in_specs=[row(d), row(pd), pl.BlockSpec((N_EXPERTS, tm), lambda i, lo: (0, i)),
                  pl.BlockSpec(memory_space=pl.ANY)] + [const_spec(c) for c in consts],
        out_specs=row(d),
        scratch_shapes=[pltpu.VMEM((N_EXPERTS * COMBINE_WIN, d), BF16),
                        pltpu.VMEM((N_EXPERTS * COMBINE_WIN, d), BF16), pltpu.VMEM((tm, d), F32),
                        pltpu.SemaphoreType.DMA((2,))],
    )
    return pl.pallas_call(
        functools.partial(_ple_final_kernel, cap=cap, rows_per_tile=tm // LANES),
        grid_spec=grid_spec,
        out_shape=jax.ShapeDtypeStruct((n, d), F32),
        compiler_params=_cparams(("arbitrary",)),
        name="ple_final",
    )(lo_tab, x1, p2d, pos, ye, *consts)


def _rope_tables(T):
    half = HEAD_DIM // 2
    inv = ROPE_THETA ** (-jnp.arange(0, half, 2, dtype=F32) / half)
    t = jnp.arange(T)
    ang_r = (t // GRID_W).astype(F32)[:, None] * inv
    ang_c = (t % GRID_W).astype(F32)[:, None] * inv
    cos = jnp.concatenate([jnp.cos(ang_r)] * 2 + [jnp.cos(ang_c)] * 2, axis=1)
    sin = jnp.concatenate([-jnp.sin(ang_r), jnp.sin(ang_r), -jnp.sin(ang_c), jnp.sin(ang_c)], axis=1)
    return jnp.tile(cos, (1, N_Q_HEADS)), jnp.tile(sin, (1, N_Q_HEADS))


def _block_diag_ones(width):
    i = np.arange(width)
    return jnp.asarray((i[:, None] // HEAD_DIM) == (i[None, :] // HEAD_DIM), BF16)


def _prep_weights(g_mix, w_in, q_norm, k_norm, w_af, b_af, w_ab, b_ab, gla_norm, w_attn_up, w_gla_up,
                  w_out, g_moe, w_router, w_gate_e, w_up_e, w_down_e, g_ple, w_ple_gate, w_ple_proj,
                  g_final):
    d = w_in.shape[0]
    o_gla = ATTN_W + 2 * KV_W
    o_lr = o_gla + 2 * GLA_KW + 2 * GLA_VW
    o_ga = o_lr + 2 * GLA_LOWRANK
    wlr = jnp.zeros((d, LANES), F32).at[:, :2 * GLA_LOWRANK].set(w_in[:, o_lr:o_ga])
    wgate = jnp.zeros((LANES, 2 * GLA_KW), F32)
    wgate = wgate.at[:GLA_LOWRANK, :GLA_KW].set(w_af)
    wgate = wgate.at[GLA_LOWRANK:2 * GLA_LOWRANK, GLA_KW:].set(w_ab)
    wr_t = w_router.T
    wr_hi = wr_t.astype(BF16)
    wr_lo = (wr_t - wr_hi.astype(F32)).astype(BF16)
    return dict(
        gmix=g_mix.reshape(1, d),
        wqk=w_in[:, :ATTN_W + KV_W].astype(BF16),
        wvt=w_in[:, ATTN_W + KV_W:o_gla].T.astype(BF16),
        wgla=w_in[:, o_gla:o_lr].astype(BF16),
        wlr=wlr.astype(BF16),
        wga=w_in[:, o_ga:o_ga + d].astype(BF16),
        wgl=w_in[:, o_ga + d:].astype(BF16),
        wgate=wgate.astype(BF16),
        bgate=jnp.concatenate([b_af, b_ab]).reshape(1, 2 * GLA_KW),
        qn=jnp.tile(q_norm, N_Q_HEADS).reshape(1, ATTN_W),
        kn=jnp.tile(k_norm, N_KV_HEADS).reshape(1, KV_W),
        bdq=_block_diag_ones(ATTN_W),
        bdk=_block_diag_ones(KV_W),
        gnorm=gla_norm.reshape(1, GLA_DV),
        wau=w_attn_up.astype(BF16),
        wgu=w_gla_up.astype(BF16),
        wout=w_out.astype(BF16),
        gmoe=g_moe.reshape(1, d),
        wr_hi=wr_hi,
        wr_lo=wr_lo,
        wge=w_gate_e.astype(BF16),
        wue=w_up_e.astype(BF16),
        wde=w_down_e.astype(BF16),
        gple=g_ple.reshape(1, d),
        wpg=w_ple_gate.astype(BF16),
        wpp=w_ple_proj.astype(BF16),
        gfin=g_final.reshape(1, d),
    )


def _trunk(x, p, w):
    b, t, d = x.shape
    n = b * t
    x2d = x.reshape(n, d)
    q, k, vt, lq, lk, lv, lr, gf, gb = _in_proj(x2d, t, _rope_tables(t), w)
    r3 = lambda a: a.reshape(b, t, a.shape[-1])
    o_attn = _attention(r3(q), r3(k), vt).reshape(n, ATTN_W)
    o_gla = _gla(r3(lq), r3(lk), r3(lv), r3(lr), r3(gf), r3(gb), w["gnorm"]).reshape(n, GLA_VW)
    x1, hm, aff_t = _post_mix(x2d, o_attn, o_gla, w)
    cap = EC_CAPACITY * n // N_EXPERTS
    idx, gate, pos, lo_tab = _route(aff_t, cap)
    ye = _moe_ffn(idx, gate, hm, w["wge"], w["wue"], w["wde"])
    y = _ple_final(x1, p.reshape(n, p.shape[-1]), pos, lo_tab, ye, cap, w)
    return y.reshape(b, t, d)


def kernel(x_prompt, x_sample, p_prompt, p_sample, g_mix, w_in, q_norm, k_norm, w_af, b_af, w_ab, b_ab,
           gla_norm, w_attn_up, w_gla_up, w_out, g_moe, w_router, w_gate_e, w_up_e, w_down_e, g_ple,
           w_ple_gate, w_ple_proj, g_final):
    assert g_mix.shape[0] == 1, "single layer"
    w = _prep_weights(g_mix[0], w_in[0], q_norm[0], k_norm[0], w_af[0], b_af[0], w_ab[0], b_ab[0],
                      gla_norm[0], w_attn_up[0], w_gla_up[0], w_out[0], g_moe[0], w_router[0],
                      w_gate_e[0], w_up_e[0], w_down_e[0], g_ple[0], w_ple_gate[0], w_ple_proj[0], g_final)
    return (_trunk(x_prompt, p_prompt[0], w), _trunk(x_sample, p_sample[0], w))
```

```python
import functools

import numpy as np
import jax
import jax.numpy as jnp
from jax import lax
from jax.experimental import pallas as pl
from jax.experimental.pallas import tpu as pltpu

F32 = jnp.float32
BF16 = jnp.bfloat16
I32 = jnp.int32

HEAD_DIM = 64
N_Q_HEADS = 8
N_KV_HEADS = 2
GRID_W = 64
ROPE_THETA = 10000.0
GLA_HEADS = 4
GLA_DK = 64
GLA_DV = 128
GLA_LOWRANK = 16
GLA_GATE_NORM = 16.0
N_EXPERTS = 16
EC_CAPACITY = 2
NORM_EPS = 1e-6
LOG2E = 1.4426950408889634

ATTN_W = N_Q_HEADS * HEAD_DIM
KV_W = N_KV_HEADS * HEAD_DIM
GLA_KW = GLA_HEADS * GLA_DK
GLA_VW = GLA_HEADS * GLA_DV

LANES = 128
BF16_ROWS = 16
VMEM_LIMIT = 56 * 1024 * 1024

GLA_CHUNK = 128
GLA_LEVELS = 7
GLA_STEP_CHUNKS = 8


def _cparams(sem):
    return pltpu.CompilerParams(dimension_semantics=sem, vmem_limit_bytes=VMEM_LIMIT)


def _dot(a, b):
    return jnp.dot(a, b, preferred_element_type=F32)


def _dot_nt(a, b):
    return lax.dot_general(a, b, (((1,), (1,)), ((), ())), preferred_element_type=F32)


def _dot_tn(a, b):
    return lax.dot_general(a, b, (((0,), (0,)), ((), ())), preferred_element_type=F32)


def _split2(x):
    hi = x.astype(BF16)
    lo = (x - hi.astype(F32)).astype(BF16)
    return hi, lo


def _split3(x):
    hi = x.astype(BF16)
    r = x - hi.astype(F32)
    mid = r.astype(BF16)
    lo = (r - mid.astype(F32)).astype(BF16)
    return hi, mid, lo


def _rms(x, gain):
    ms = jnp.mean(x * x, axis=-1, keepdims=True)
    return x * lax.rsqrt(ms + NORM_EPS) * gain


def _sigmoid(x):
    return 1.0 / (1.0 + jnp.exp(-x))


def _full_spec(shape):
    nd = len(shape)
    return pl.BlockSpec(shape, lambda *_: (0,) * nd)


def _head_ms(x, bd):
    hi, lo = _split2(x * x)
    return (_dot(hi, bd) + _dot(lo, bd)) * (1.0 / HEAD_DIM)


def _rope(x, cos, sin):
    w = x.shape[-1]
    lane = lax.broadcasted_iota(I32, x.shape, 1)
    fwd = pltpu.roll(x, w - 16, axis=1)
    bwd = pltpu.roll(x, 16, axis=1)
    partner = jnp.where((lane % 32) < 16, fwd, bwd)
    return x * cos + partner * sin


def _in_proj_kernel(x_ref, cos_ref, sin_ref, gmix_ref, wqk_ref, wvt_ref, wgla_ref, wlr_ref,
                    wgate_ref, bgate_ref, qn_ref, kn_ref, bdq_ref, bdk_ref,
                    q_ref, k_ref, vt_ref, lq_ref, lk_ref, lv_ref, lr_ref, gf_ref, gb_ref):
    x = x_ref[...]
    h = _rms(x, gmix_ref[...]).astype(BF16)

    qk = _dot(h, wqk_ref[...])
    cos = cos_ref[...]
    sin = sin_ref[...]
    q = qk[:, :ATTN_W]
    q = q * lax.rsqrt(_head_ms(q, bdq_ref[...]) + NORM_EPS) * qn_ref[...]
    q = _rope(q, cos, sin) * (HEAD_DIM ** -0.5 * LOG2E)
    q_ref[...] = q.astype(BF16)
    k = qk[:, ATTN_W:]
    k = k * lax.rsqrt(_head_ms(k, bdk_ref[...]) + NORM_EPS) * kn_ref[...]
    k = _rope(k, cos[:, :KV_W], sin[:, :KV_W])
    k_ref[...] = k.astype(BF16)
    vt_ref[0] = _dot_nt(wvt_ref[...], h).astype(BF16)

    gla = _dot(h, wgla_ref[...])
    lq_ref[...] = gla[:, :GLA_KW] * (GLA_DK ** -0.5)
    lk_ref[...] = gla[:, GLA_KW:2 * GLA_KW]
    lv_ref[...] = gla[:, 2 * GLA_KW:2 * GLA_KW + GLA_VW].astype(BF16)
    lr_ref[...] = gla[:, 2 * GLA_KW + GLA_VW:]

    a = _dot(h, wlr_ref[...]).astype(BF16)
    z = _dot(a, wgate_ref[...]) + bgate_ref[...]
    g = (jnp.minimum(z, 0.0) - jnp.log(1.0 + jnp.exp(-jnp.abs(z)))) * (1.0 / GLA_GATE_NORM)
    gf_ref[...] = g[:, :GLA_KW]
    gb_ref[...] = g[:, GLA_KW:]


def _in_proj(x2d, T, tabs, w, tm=512):
    n, d = x2d.shape
    nt = T // tm
    row = lambda wd: pl.BlockSpec((tm, wd), lambda i: (i, 0))
    tab = pl.BlockSpec((tm, ATTN_W), lambda i: (i % nt, 0))
    consts = [w["gmix"], w["wqk"], w["wvt"], w["wgla"], w["wlr"], w["wgate"], w["bgate"],
              w["qn"], w["kn"], w["bdq"], w["bdk"]]
    outs = [(ATTN_W, BF16), (KV_W, BF16), None, (GLA_KW, F32), (GLA_KW, F32), (GLA_VW, BF16),
            (GLA_VW, F32), (GLA_KW, F32), (GLA_KW, F32)]
    vt_spec = pl.BlockSpec((1, KV_W, tm), lambda i: (i // nt, 0, i % nt))
    vt_shape = jax.ShapeDtypeStruct((n // T, KV_W, T), BF16)
    return pl.pallas_call(
        _in_proj_kernel,
        grid=(n // tm,),
        in_specs=[row(d), tab, tab] + [_full_spec(c.shape) for c in consts],
        out_specs=[vt_spec if o is None else row(o[0]) for o in outs],
        out_shape=[vt_shape if o is None else jax.ShapeDtypeStruct((n, o[0]), o[1]) for o in outs],
        compiler_params=_cparams(("parallel",)),
        name="in_proj",
    )(x2d, tabs[0], tabs[1], *consts)


def _attn_kernel(q_ref, k_ref, vt_ref, o_ref, qpad_ref, m_ref, acc_ref, s_even, s_odd, mx_even, mx_odd, *, nk):
    j = pl.program_id(2)
    group = N_Q_HEADS // N_KV_HEADS
    even = (s_even, mx_even)
    odd = (s_odd, mx_odd)

    def head_scores(dst, hq):
        s = _dot(k_ref[0], qpad_ref[hq])
        dst[0][hq] = s
        dst[1][hq] = jnp.max(s, axis=0, keepdims=True)

    def head_softmax(src, hq):
        m_old = m_ref[hq]
        m_new = jnp.maximum(m_old, src[1][hq])
        alpha = jnp.exp2(m_old - m_new)
        p = jnp.exp2(src[0][hq] - m_new)
        m_ref[hq] = m_new
        return p.astype(BF16), alpha

    def head_values(hq, p, alpha):
        hk = hq // group
        ones = jnp.ones((BF16_ROWS, vt_ref.shape[2]), BF16)
        vt = vt_ref[0, hk * HEAD_DIM:(hk + 1) * HEAD_DIM, :]
        vt1 = jnp.concatenate([vt, ones], axis=0)
        acc_ref[hq] = alpha * acc_ref[hq] + _dot(vt1, p)

    def scores_into(dst):
        for hq in range(N_Q_HEADS):
            head_scores(dst, hq)

    def consume(src, dst=None):
        pending = None
        for hq in range(N_Q_HEADS):
            if dst is not None:
                head_scores(dst, hq)
            p, alpha = head_softmax(src, hq)
            if pending is not None:
                head_values(*pending)
            pending = (hq, p, alpha)
        head_values(*pending)

    @pl.when(j == 0)
    def _():
        m_ref[...] = jnp.full(m_ref.shape, -jnp.inf, F32)
        acc_ref[...] = jnp.zeros(acc_ref.shape, F32)
        lane = lax.broadcasted_iota(I32, (q_ref.shape[1], LANES), 1)
        for hq in range(N_Q_HEADS):
            hk = hq // group
            chunk = q_ref[0, :, (hq // 2) * LANES:(hq // 2 + 1) * LANES].astype(F32)
            if hq % 2 != hk:
                chunk = pltpu.roll(chunk, HEAD_DIM, axis=1)
            keep = (lane // HEAD_DIM) == hk
            qpad_ref[hq] = jnp.where(keep, chunk, 0.0).T.astype(BF16)
        scores_into(even)

    @pl.when((j > 0) & (j < nk) & (lax.rem(j, 2) == 1))
    def _():
        consume(even, odd)

    @pl.when((j > 0) & (j < nk) & (lax.rem(j, 2) == 0))
    def _():
        consume(odd, even)

    @pl.when(j == nk)
    def _():
        consume(odd if nk % 2 == 0 else even)
        outs = [acc_ref[hq, :HEAD_DIM, :] / acc_ref[hq, HEAD_DIM:HEAD_DIM + 1, :] for hq in range(N_Q_HEADS)]
        o_ref[0] = jnp.concatenate(outs, axis=0).T.astype(BF16)


def _attention(q, k, vt, tq=1024, tk=512):
    b, t, _ = q.shape
    nk = t // tk
    s_buf = pltpu.VMEM((N_Q_HEADS, tk, tq), F32)
    mx_buf = pltpu.VMEM((N_Q_HEADS, 1, tq), F32)
    return pl.pallas_call(
        functools.partial(_attn_kernel, nk=nk),
        grid=(b, t // tq, nk + 1),
        in_specs=[pl.BlockSpec((1, tq, ATTN_W), lambda bi, i, j: (bi, i, 0)),
                  pl.BlockSpec((1, tk, KV_W), lambda bi, i, j: (bi, jnp.minimum(j, nk - 1), 0)),
                  pl.BlockSpec((1, KV_W, tk), lambda bi, i, j: (bi, 0, jnp.maximum(j - 1, 0)))],
        out_specs=pl.BlockSpec((1, tq, ATTN_W), lambda bi, i, j: (bi, i, 0)),
        out_shape=jax.ShapeDtypeStruct((b, t, ATTN_W), BF16),
        scratch_shapes=[pltpu.VMEM((N_Q_HEADS, LANES, tq), BF16),
                        pltpu.VMEM((N_Q_HEADS, 1, tq), F32),
                        pltpu.VMEM((N_Q_HEADS, HEAD_DIM + BF16_ROWS, tq), F32),
                        s_buf, s_buf, mx_buf, mx_buf],
        compiler_params=_cparams(("parallel", "parallel", "arbitrary")),
        name="attention",
    )(q, k, vt)


def _gla_chunk(q, k, v, g, st_ref, mask_ref, reverse):
    c = GLA_CHUNK
    pair_w = 2 * GLA_DK
    t_kw = lax.broadcasted_iota(I32, (c, GLA_KW), 0)
    lo_lanes = lax.broadcasted_iota(I32, (c, pair_w), 1) < GLA_DK

    a_pairs = [jnp.zeros((c, 2 * c), F32) for _ in range(2)]

    def add_level(a_pairs, qh, kh, mask2):
        qb = qh.astype(BF16)
        kb = kh.astype(BF16)
        out = []
        for p in range(2):
            qp = qb[:, p * pair_w:(p + 1) * pair_w]
            kp = kb[:, p * pair_w:(p + 1) * pair_w]
            k2 = jnp.concatenate([jnp.where(lo_lanes, kp, 0.0), jnp.where(lo_lanes, 0.0, kp)], axis=0)
            out.append(a_pairs[p] + mask2 * _dot_nt(qp, k2))
        return out

    pre = g
    tot = g
    for lvl in range(GLA_LEVELS):
        s = 1 << lvl
        odd = ((t_kw >> lvl) & 1) == 1
        if not reverse:
            qh = q * jnp.exp(pre)
            kh = k * jnp.exp(tot - pre)
        else:
            qh = q * jnp.exp(tot - pre + g)
            kh = k * jnp.exp(pre - g)
        a_pairs = add_level(a_pairs, qh, kh, mask_ref[lvl])
        sib = jnp.where(odd, pltpu.roll(tot, s, axis=0), pltpu.roll(tot, c - s, axis=0))
        pre = pre + jnp.where(odd, sib, 0.0)
        tot = tot + sib
    if not reverse:
        a_pairs = add_level(a_pairs, q, k, mask_ref[GLA_LEVELS])

    if not reverse:
        q_dec = q * jnp.exp(pre)
        k_dec = k * jnp.exp(tot - pre)
    else:
        q_dec = q * jnp.exp(tot - pre + g)
        k_dec = k * jnp.exp(pre - g)
    chunk_decay = jnp.exp(tot[0:1, :])
    sub_pair = lax.broadcasted_iota(I32, (2 * GLA_DV, pair_w), 0)
    lane_st = lax.broadcasted_iota(I32, (2 * GLA_DV, pair_w), 1)
    block_diag = (sub_pair // GLA_DV) == (lane_st // GLA_DK)
    first_head_v = lax.broadcasted_iota(I32, (c, 2 * GLA_DV), 1) < GLA_DV

    outs = []
    for p in range(2):
        st = st_ref[p]
        qd = q_dec[:, p * pair_w:(p + 1) * pair_w].astype(BF16)
        o_pair = _dot_nt(qd, st.astype(BF16))
        vp = v[:, p * 2 * GLA_DV:(p + 1) * 2 * GLA_DV].astype(BF16)
        kd = k_dec[:, p * pair_w:(p + 1) * pair_w].astype(BF16)
        upd = _dot_tn(vp, kd)
        st_ref[p] = st * chunk_decay[:, p * pair_w:(p + 1) * pair_w] + jnp.where(block_diag, upd, 0.0)
        v_diag = jnp.concatenate([jnp.where(first_head_v, vp, 0.0), jnp.where(first_head_v, 0.0, vp)], axis=0)
        outs.append(o_pair + _dot(a_pairs[p].astype(BF16), v_diag))
    return jnp.concatenate(outs, axis=1)


def _gla_level_masks(reverse):
    c = GLA_CHUNK
    t = np.arange(c)[:, None]
    s = np.arange(c)[None, :]
    masks = []
    for lvl in range(GLA_LEVELS):
        bt, bs = t >> lvl, s >> lvl
        masks.append((bs == bt + 1) & (bt % 2 == 0) if reverse else (bt == bs + 1) & (bs % 2 == 0))
    masks.append(t == s)
    m = np.stack(masks).astype(np.float32)
    return jnp.asarray(np.concatenate([m, m], axis=2))


def _gla_bwd_kernel(q_ref, k_ref, v_ref, g_ref, mask_ref, o_ref, st_ref):
    @pl.when(pl.program_id(1) == 0)
    def _():
        st_ref[...] = jnp.zeros(st_ref.shape, F32)

    for sub in reversed(range(GLA_STEP_CHUNKS)):
        rows = slice(sub * GLA_CHUNK, (sub + 1) * GLA_CHUNK)
        o_ref[0, rows, :] = _gla_chunk(q_ref[0, rows, :], k_ref[0, rows, :], v_ref[0, rows, :], g_ref[0, rows, :],
                                       st_ref, mask_ref, True)


def _gla_fwd_kernel(q_ref, k_ref, v_ref, g_ref, ob_ref, r_ref, gn_ref, mask_ref, o_ref, st_ref):
    @pl.when(pl.program_id(1) == 0)
    def _():
        st_ref[...] = jnp.zeros(st_ref.shape, F32)

    gn = gn_ref[...]
    for sub in range(GLA_STEP_CHUNKS):
        rows = slice(sub * GLA_CHUNK, (sub + 1) * GLA_CHUNK)
        o = _gla_chunk(q_ref[0, rows, :], k_ref[0, rows, :], v_ref[0, rows, :], g_ref[0, rows, :],
                       st_ref, mask_ref, False) + ob_ref[0, rows, :]
        r = r_ref[0, rows, :]
        gate = r * _sigmoid(r)
        outs = []
        for h in range(GLA_HEADS):
            oh = o[:, h * GLA_DV:(h + 1) * GLA_DV]
            outs.append(_rms(oh, gn))
        o_ref[0, rows, :] = (jnp.concatenate(outs, axis=1) * gate).astype(BF16)


def _gla(lq, lk, lv, lr, gf, gb, gla_norm):
    b, t, _ = lq.shape
    c = GLA_CHUNK * GLA_STEP_CHUNKS
    nc = t // c
    fwd = lambda wd: pl.BlockSpec((1, c, wd), lambda bi, ci: (bi, ci, 0))
    rev = lambda wd: pl.BlockSpec((1, c, wd), lambda bi, ci: (bi, nc - 1 - ci, 0))
    st = pltpu.VMEM((2, 2 * GLA_DV, 2 * GLA_DK), F32)
    mask_b = _gla_level_masks(True)
    mask_f = _gla_level_masks(False)
    o_b = pl.pallas_call(
        _gla_bwd_kernel,
        grid=(b, nc),
        in_specs=[rev(GLA_KW), rev(GLA_KW), rev(GLA_VW), rev(GLA_KW), _full_spec(mask_b.shape)],
        out_specs=rev(GLA_VW),
        out_shape=jax.ShapeDtypeStruct((b, t, GLA_VW), F32),
        scratch_shapes=[st],
        compiler_params=_cparams(("parallel", "arbitrary")),
        name="gla_bwd",
    )(lq, lk, lv, gb, mask_b)
    return pl.pallas_call(
        _gla_fwd_kernel,
        grid=(b, nc),
        in_specs=[fwd(GLA_KW), fwd(GLA_KW), fwd(GLA_VW), fwd(GLA_KW), fwd(GLA_VW), fwd(GLA_VW),
                  _full_spec(gla_norm.shape), _full_spec(mask_f.shape)],
        out_specs=fwd(GLA_VW),
        out_shape=jax.ShapeDtypeStruct((b, t, GLA_VW), BF16),
        scratch_shapes=[st],
        compiler_params=_cparams(("parallel", "arbitrary")),
        name="gla_fwd",
    )(lq, lk, lv, gf, o_b, lr, gla_norm, mask_f)


def _post_mix_kernel(x_ref, oa_ref, og_ref, gmix_ref, wga_ref, wgl_ref, wau_ref, wgu_ref, wout_ref, gmoe_ref,
                     wr_hi_ref, wr_lo_ref, x1_ref, hm_ref, aff_ref):
    x = x_ref[...]
    h = _rms(x, gmix_ref[...]).astype(BF16)
    merged = (_sigmoid(_dot(h, wga_ref[...])) * _dot(oa_ref[...], wau_ref[...])
              + _sigmoid(_dot(h, wgl_ref[...])) * _dot(og_ref[...], wgu_ref[...]))
    x1 = x + _dot(merged.astype(BF16), wout_ref[...])
    x1_ref[...] = x1
    hm = _rms(x1, gmoe_ref[...])
    hm_ref[...] = hm
    h_hi, h_lo = _split2(hm)
    wr_hi = wr_hi_ref[...]
    both = _dot_nt(jnp.concatenate([wr_hi, wr_lo_ref[...]], axis=0), h_hi)
    logits = both[:N_EXPERTS] + both[N_EXPERTS:] + _dot_nt(wr_hi, h_lo)
    m = jnp.max(logits, axis=0, keepdims=True)
    e = jnp.exp(logits - m)
    aff_ref[...] = e / jnp.sum(e, axis=0, keepdims=True)


def _post_mix(x2d, oa, og, w, tm=512):
    n, d = x2d.shape
    row = lambda wd: pl.BlockSpec((tm, wd), lambda i: (i, 0))
    consts = [w["gmix"], w["wga"], w["wgl"], w["wau"], w["wgu"], w["wout"], w["gmoe"], w["wr_hi"], w["wr_lo"]]
    return pl.pallas_call(
        _post_mix_kernel,
        grid=(n // tm,),
        in_specs=[row(d), row(ATTN_W), row(GLA_VW)] + [_full_spec(c.shape) for c in consts],
        out_specs=[row(d), row(d), pl.BlockSpec((N_EXPERTS, tm), lambda i: (0, i))],
        out_shape=[jax.ShapeDtypeStruct((n, d), F32), jax.ShapeDtypeStruct((n, d), F32),
                   jax.ShapeDtypeStruct((N_EXPERTS, n), F32)],
        compiler_params=_cparams(("parallel",)),
        name="post_mix",
    )(x2d, oa, og, *consts)


ROUTE_SLOTS = 256


def _route_kernel(a_ref, idx_ref, gate_ref, pos_ref, lo_ref, *, cap):
    a = a_ref[0]
    nrow = a.shape[0]
    b_io = lax.broadcasted_iota(I32, (nrow, LANES), 0)
    t_io = lax.broadcasted_iota(I32, (nrow, LANES), 1)
    tok = b_io * LANES + t_io

    def as_float(word):
        return lax.bitcast_convert_type(jnp.full((1, LANES), word, I32), F32)

    def thr_step(i, thr):
        cand = thr | lax.shift_left(jnp.int32(1), jnp.int32(30) - i)
        cnt = jnp.sum((a >= as_float(cand)).astype(I32))
        return jnp.where(cnt >= cap, cand, thr)

    thr = lax.fori_loop(0, 31, thr_step, jnp.int32(0))
    gt = a >= as_float(thr + 1)
    eq = (a >= as_float(thr)) & jnp.logical_not(gt)
    need = cap - jnp.sum(gt.astype(I32))

    def tie_step(i, lim):
        cand = lim + lax.shift_left(jnp.int32(1), jnp.int32(17) - i)
        cnt = jnp.sum((eq & (tok < cand)).astype(I32))
        return jnp.where(cnt < need, cand, lim)

    lim = lax.fori_loop(0, 18, tie_step, jnp.int32(0))
    selm = gt | (eq & (tok <= lim))
    selb = selm.astype(F32).astype(BF16)

    r_io = lax.broadcasted_iota(I32, (LANES, LANES), 0)
    c_io = lax.broadcasted_iota(I32, (LANES, LANES), 1)
    utri = (r_io <= c_io).astype(BF16)
    cum_row = _dot(selb, utri)
    row_cnt = cum_row[:, LANES - 1:LANES]
    br_io = lax.broadcasted_iota(I32, (nrow, nrow), 0)
    bc_io = lax.broadcasted_iota(I32, (nrow, nrow), 1)
    lrow = (bc_io <= br_io).astype(BF16)
    row_incl = _dot(lrow, jnp.broadcast_to(row_cnt, (nrow, LANES)).astype(BF16))
    row_excl = row_incl - row_cnt
    pos_ref[0] = jnp.where(selm, row_excl + cum_row - 1.0, -1.0).astype(I32)
    lo_ref[0] = row_excl.astype(I32)
    row_incl_c = row_incl[:, 0:1]
    row_excl_c = row_excl[:, 0:1]
    cum_t = cum_row.T.astype(BF16)
    a_hi, a_mid, a_lo = _split3(a.T)
    sw = ROUTE_SLOTS
    brow = lax.broadcasted_iota(I32, (nrow, sw), 0).astype(F32)
    tsub = lax.broadcasted_iota(I32, (LANES, sw), 0).astype(F32)
    lane = lax.broadcasted_iota(I32, (1, sw), 1).astype(F32)

    def slot_tile(jt, carry):
        j = lane + jnp.asarray(jt, F32) * float(sw)
        bj = jnp.sum((row_incl_c <= j).astype(F32), axis=0, keepdims=True)
        onehot = (brow == bj).astype(F32)
        base = jnp.sum(onehot * row_excl_c, axis=0, keepdims=True)
        jl = j - base
        ohb = onehot.astype(BF16)
        rt = _dot(cum_t, ohb)
        tj = jnp.sum((rt <= jl).astype(F32), axis=0, keepdims=True)
        acol = _dot(a_hi, ohb) + _dot(a_mid, ohb) + _dot(a_lo, ohb)
        gate = jnp.sum(jnp.where(tsub == tj, acol, 0.0), axis=0, keepdims=True)
        tok_id = (bj * float(LANES) + tj).astype(I32)
        for c in range(sw // LANES):
            row = jt * (sw // LANES) + c
            idx_ref[0, pl.ds(row, 1), :] = tok_id[:, c * LANES:(c + 1) * LANES]
            gate_ref[0, pl.ds(row, 1), :] = gate[:, c * LANES:(c + 1) * LANES]
        return carry

    lax.fori_loop(0, cap // sw, slot_tile, 0)


def _route(aff_t, cap):
    e, n = aff_t.shape
    nrow = n // LANES
    a3 = aff_t.reshape(e, nrow, LANES)
    nt = cap // LANES
    slot_spec = pl.BlockSpec((1, nt, LANES), lambda i: (i, 0, 0))
    tok_spec = pl.BlockSpec((1, nrow, LANES), lambda i: (i, 0, 0))
    idx, gate, pos, lo = pl.pallas_call(
        functools.partial(_route_kernel, cap=cap),
        grid=(e,),
        in_specs=[tok_spec],
        out_specs=[slot_spec, slot_spec, tok_spec, tok_spec],
        out_shape=[jax.ShapeDtypeStruct((e, nt, LANES), I32), jax.ShapeDtypeStruct((e, nt, LANES), F32),
                   jax.ShapeDtypeStruct((e, nrow, LANES), I32), jax.ShapeDtypeStruct((e, nrow, LANES), I32)],
        compiler_params=_cparams(("parallel",)),
        name="route",
    )(a3)
    lo_tab = jnp.concatenate([lo[:, :, 0], jnp.full((e, 1), cap, I32)], axis=1)
    return idx, gate, pos.reshape(e, n), lo_tab


MOE_ISSUE_UNROLL = 8


def _moe_kernel(idx_ref, idxn_ref, gate_ref, hm_hbm, wg_ref, wu_ref, wd_ref, ye_ref, buf_a, buf_b, sem, *, rows):
    nblk = pl.num_programs(1)
    s = pl.program_id(0) * nblk + pl.program_id(1)
    nsteps = pl.num_programs(0) * nblk

    def buffer_wait(buf, k):
        pltpu.make_async_copy(hm_hbm.at[pl.ds(0, rows)], buf, sem.at[k]).wait()

    @pl.when(s == 0)
    def _():
        def body(r0, c):
            for u in range(MOE_ISSUE_UNROLL):
                r = r0 * MOE_ISSUE_UNROLL + u
                n = idx_ref[0, 0, r]
                pltpu.make_async_copy(hm_hbm.at[pl.ds(n, 1)], buf_a.at[pl.ds(r, 1)], sem.at[0]).start()
            return c
        lax.fori_loop(0, rows // MOE_ISSUE_UNROLL, body, 0)

    def step(cur, k_cur, nxt, k_nxt):
        buffer_wait(cur, k_cur)

        @pl.when(s < nsteps)
        def _():
            for r in range(rows):
                n = idxn_ref[0, 0, r]
                pltpu.make_async_copy(hm_hbm.at[pl.ds(n, 1)], nxt.at[pl.ds(r, 1)], sem.at[k_nxt]).start()

        xe = cur[...].astype(BF16)
        hid_g = _dot(xe, wg_ref[0])
        hid = (hid_g * _sigmoid(hid_g)) * _dot(xe, wu_ref[0])
        y = _dot(hid.astype(BF16), wd_ref[0])
        eye = (lax.broadcasted_iota(I32, (LANES, LANES), 0) == lax.broadcasted_iota(I32, (LANES, LANES), 1))
        for c in range(rows // LANES):
            g_row = gate_ref[0, :, c * LANES:(c + 1) * LANES]
            g_col = jnp.sum(jnp.where(eye, g_row, 0.0), axis=1, keepdims=True)
            sl = slice(c * LANES, (c + 1) * LANES)
            ye_ref[sl, :] = (y[sl, :] * g_col).astype(BF16)

        @pl.when(s == nsteps - 1)
        def _():
            buffer_wait(nxt, k_nxt)

    @pl.when(lax.rem(s, 2) == 0)
    def _():
        step(buf_a, 0, buf_b, 1)

    @pl.when(lax.rem(s, 2) == 1)
    def _():
        step(buf_b, 1, buf_a, 0)


def _moe_ffn(idx, gate, hm, wg, wu, wd, rows=512):
    e, nt, _ = idx.shape
    cap = nt * LANES
    n, d = hm.shape
    nblk = cap // rows
    nsteps = e * nblk
    idx3 = idx.reshape(nsteps, 1, rows)
    gate3 = gate.reshape(nsteps, 1, rows)
    ff = wg.shape[-1]
    return pl.pallas_call(
        functools.partial(_moe_kernel, rows=rows),
        grid=(e, nblk),
        in_specs=[pl.BlockSpec((1, 1, rows), lambda ei, j: (ei * nblk + j, 0, 0), memory_space=pltpu.SMEM),
                  pl.BlockSpec((1, 1, rows), lambda ei, j: (jnp.minimum(ei * nblk + j + 1, nsteps - 1), 0, 0),
                               memory_space=pltpu.SMEM),
                  pl.BlockSpec((1, 1, rows), lambda ei, j: (ei * nblk + j, 0, 0)),
                  pl.BlockSpec(memory_space=pl.ANY),
                  pl.BlockSpec((1, d, ff), lambda ei, j: (ei, 0, 0)),
                  pl.BlockSpec((1, d, ff), lambda ei, j: (ei, 0, 0)),
                  pl.BlockSpec((1, ff, d), lambda ei, j: (ei, 0, 0))],
        out_specs=pl.BlockSpec((rows, d), lambda ei, j: (ei * nblk + j, 0)),
        out_shape=jax.ShapeDtypeStruct((e * cap, d), BF16),
        scratch_shapes=[pltpu.VMEM((rows, d), F32), pltpu.VMEM((rows, d), F32), pltpu.SemaphoreType.DMA((2,))],
        compiler_params=_cparams(("arbitrary", "arbitrary")),
        name="moe_ffn",
    )(idx3, idx3, gate3, hm, wg, wu, wd)


COMBINE_WIN = 64


def _ple_final_kernel(lo_ref, x_ref, p_ref, pos_ref, ye_hbm, gple_ref, wpg_ref, wpp_ref, gfin_ref, o_ref,
                      win_a, win_b, acc_ref, sem, *, cap, rows_per_tile):
    i = pl.program_id(0)
    ntiles = pl.num_programs(0)
    win = COMBINE_WIN
    tm = x_ref.shape[0]

    def tile_floors(ti):
        return [jnp.bitwise_and(lo_ref[e, ti * rows_per_tile], -BF16_ROWS) for e in range(N_EXPERTS)]

    def clamp(floors):
        return [jnp.minimum(f, cap - win) for f in floors]

    def start_fetch(floors, buf, k):
        starts = clamp(floors)
        for e in range(N_EXPERTS):
            src = ye_hbm.at[pl.ds(pl.multiple_of(e * cap + starts[e], BF16_ROWS), win)]
            pltpu.make_async_copy(src, buf.at[pl.ds(e * win, win)], sem.at[k]).start()

    def wait_fetch(buf, k):
        pltpu.make_async_copy(ye_hbm.at[pl.ds(0, N_EXPERTS * win)], buf, sem.at[k]).wait()

    def expand(floors, buf):
        starts = clamp(floors)
        pos = pos_ref[...]
        w_io = lax.broadcasted_iota(I32, (win, tm), 0)
        parts = []
        for e in range(N_EXPERTS):
            pe = pos[e:e + 1, :]
            hit = ((pe - starts[e]) == w_io) & (pe >= floors[e])
            parts.append(jnp.where(hit, 1.0, 0.0).astype(BF16))
        onehot_t = jnp.concatenate(parts, axis=0)
        return _dot_tn(onehot_t, buf[...])

    @pl.when(i == 0)
    def _():
        start_fetch(tile_floors(0), win_a, 0)

    def step(cur, k_cur, nxt, k_nxt):
        floors0 = tile_floors(i)
        wait_fetch(cur, k_cur)

        @pl.when(i + 1 < ntiles)
        def _():
            start_fetch(tile_floors(i + 1), nxt, k_nxt)

        acc_ref[...] = x_ref[...] + expand(floors0, cur)

        nrounds = jnp.int32(1)
        for e in range(N_EXPERTS):
            hi = lo_ref[e, (i + 1) * rows_per_tile]
            nrounds = jnp.maximum(nrounds, lax.div(hi - floors0[e] + (win - 1), jnp.int32(win)))

        def extra_round(r, c):
            floors = [f + r * win for f in floors0]
            start_fetch(floors, cur, k_cur)
            wait_fetch(cur, k_cur)
            acc_ref[...] += expand(floors, cur)
            return c

        lax.fori_loop(1, nrounds, extra_round, 0)

        x = acc_ref[...]
        hp = _rms(x, gple_ref[...]).astype(BF16)
        gate = _sigmoid(_dot(hp, wpg_ref[...]))
        y = x + gate * _dot(p_ref[...].astype(BF16), wpp_ref[...])
        o_ref[...] = _rms(y, gfin_ref[...])

    @pl.when(lax.rem(i, 2) == 0)
    def _():
        step(win_a, 0, win_b, 1)

    @pl.when(lax.rem(i, 2) == 1)
    def _():
        step(win_b, 1, win_a, 0)


def _ple_final(x1, p2d, pos, lo_tab, ye, cap, w, tm=256):
    n, d = x1.shape
    pd = p2d.shape[-1]
    row = lambda wd: pl.BlockSpec((tm, wd), lambda i, lo: (i, 0))
    consts = [w["gple"], w["wpg"], w["wpp"], w["gfin"]]
    const_spec = lambda c: pl.BlockSpec(c.shape, lambda i, lo: (0,) * c.ndim)
    grid_spec = pltpu.PrefetchScalarGridSpec(
        num_scalar_prefetch=1,
        grid=(n // tm,),
        in_specs=[row(d), row(pd), pl.BlockSpec((N_EXPERTS, tm), lambda i, lo: (0, i)),
                  pl.BlockSpec(memory_space=pl.ANY)] + [const_spec(c) for c in consts],
        out_specs=row(d),
        scratch_shapes=[pltpu.VMEM((N_EXPERTS * COMBINE_WIN, d), BF16),
                        pltpu.VMEM((N_EXPERTS * COMBINE_WIN, d), BF16), pltpu.VMEM((tm, d), F32),
                        pltpu.SemaphoreType.DMA((2,))],
    )
    return pl.pallas_call(
        functools.partial(_ple_final_kernel, cap=cap, rows_per_tile=tm // LANES),
        grid_spec=grid_spec,
        out_shape=jax.ShapeDtypeStruct((n, d), F32),
        compiler_params=_cparams(("arbitrary",)),
        name="ple_final",
    )(lo_tab, x1, p2d, pos, ye, *consts)


def _rope_tables(T):
    half = HEAD_DIM // 2
    inv = ROPE_THETA ** (-jnp.arange(0, half, 2, dtype=F32) / half)
    t = jnp.arange(T)
    ang_r = (t // GRID_W).astype(F32)[:, None] * inv
    ang_c = (t % GRID_W).astype(F32)[:, None] * inv
    cos = jnp.concatenate([jnp.cos(ang_r)] * 2 + [jnp.cos(ang_c)] * 2, axis=1)
    sin = jnp.concatenate([-jnp.sin(ang_r), jnp.sin(ang_r), -jnp.sin(ang_c), jnp.sin(ang_c)], axis=1)
    return jnp.tile(cos, (1, N_Q_HEADS)), jnp.tile(sin, (1, N_Q_HEADS))


def _block_diag_ones(width):
    i = np.arange(width)
    return jnp.asarray((i[:, None] // HEAD_DIM) == (i[None, :] // HEAD_DIM), BF16)


def _prep_weights(g_mix, w_in, q_norm, k_norm, w_af, b_af, w_ab, b_ab, gla_norm, w_attn_up, w_gla_up,
                  w_out, g_moe, w_router, w_gate_e, w_up_e, w_down_e, g_ple, w_ple_gate, w_ple_proj,
                  g_final):
    d = w_in.shape[0]
    o_gla = ATTN_W + 2 * KV_W
    o_lr = o_gla + 2 * GLA_KW + 2 * GLA_VW
    o_ga = o_lr + 2 * GLA_LOWRANK
    wlr = jnp.zeros((d, LANES), F32).at[:, :2 * GLA_LOWRANK].set(w_in[:, o_lr:o_ga])
    wgate = jnp.zeros((LANES, 2 * GLA_KW), F32)
    wgate = wgate.at[:GLA_LOWRANK, :GLA_KW].set(w_af)
    wgate = wgate.at[GLA_LOWRANK:2 * GLA_LOWRANK, GLA_KW:].set(w_ab)
    wr_t = w_router.T
    wr_hi = wr_t.astype(BF16)
    wr_lo = (wr_t - wr_hi.astype(F32)).astype(BF16)
    return dict(
        gmix=g_mix.reshape(1, d),
        wqk=w_in[:, :ATTN_W + KV_W].astype(BF16),
        wvt=w_in[:, ATTN_W + KV_W:o_gla].T.astype(BF16),
        wgla=w_in[:, o_gla:o_lr].astype(BF16),
        wlr=wlr.astype(BF16),
        wga=w_in[:, o_ga:o_ga + d].astype(BF16),
        wgl=w_in[:, o_ga + d:].astype(BF16),
        wgate=wgate.astype(BF16),
        bgate=jnp.concatenate([b_af, b_ab]).reshape(1, 2 * GLA_KW),
        qn=jnp.tile(q_norm, N_Q_HEADS).reshape(1, ATTN_W),
        kn=jnp.tile(k_norm, N_KV_HEADS).reshape(1, KV_W),
        bdq=_block_diag_ones(ATTN_W),
        bdk=_block_diag_ones(KV_W),
        gnorm=gla_norm.reshape(1, GLA_DV),
        wau=w_attn_up.astype(BF16),
        wgu=w_gla_up.astype(BF16),
        wout=w_out.astype(BF16),
        gmoe=g_moe.reshape(1, d),
        wr_hi=wr_hi,
        wr_lo=wr_lo,
        wge=w_gate_e.astype(BF16),
        wue=w_up_e.astype(BF16),
        wde=w_down_e.astype(BF16),
        gple=g_ple.reshape(1, d),
        wpg=w_ple_gate.astype(BF16),
        wpp=w_ple_proj.astype(BF16),
        gfin=g_final.reshape(1, d),
    )


def _trunk(x, p, w):
    b, t, d = x.shape
    n = b * t
    x2d = x.reshape(n, d)
    q, k, vt, lq, lk, lv, lr, gf, gb = _in_proj(x2d, t, _rope_tables(t), w)
    r3 = lambda a: a.reshape(b, t, a.shape[-1])
    o_attn = _attention(r3(q), r3(k), vt).reshape(n, ATTN_W)
    o_gla = _gla(r3(lq), r3(lk), r3(lv), r3(lr), r3(gf), r3(gb), w["gnorm"]).reshape(n, GLA_VW)
    x1, hm, aff_t = _post_mix(x2d, o_attn, o_gla, w)
    cap = EC_CAPACITY * n // N_EXPERTS
    idx, gate, pos, lo_tab = _route(aff_t, cap)
    ye = _moe_ffn(idx, gate, hm, w["wge"], w["wue"], w["wde"])
    y = _ple_final(x1, p.reshape(n, p.shape[-1]), pos, lo_tab, ye, cap, w)
    return y.reshape(b, t, d)


def kernel(x_prompt, x_sample, p_prompt, p_sample, g_mix, w_in, q_norm, k_norm, w_af, b_af, w_ab, b_ab,
           gla_norm, w_attn_up, w_gla_up, w_out, g_moe, w_router, w_gate_e, w_up_e, w_down_e, g_ple,
           w_ple_gate, w_ple_proj, g_final):
    assert g_mix.shape[0] == 1, "single layer"
    w = _prep_weights(g_mix[0], w_in[0], q_norm[0], k_norm[0], w_af[0], b_af[0], w_ab[0], b_ab[0],
                      gla_norm[0], w_attn_up[0], w_gla_up[0], w_out[0], g_moe[0], w_router[0],
                      w_gate_e[0], w_up_e[0], w_down_e[0], g_ple[0], w_ple_gate[0], w_ple_proj[0], g_final)
    return (_trunk(x_prompt, p_prompt[0], w), _trunk(x_sample, p_sample[0], w))
```

```python
import functools

import numpy as np
import jax
import jax.numpy as jnp
from jax import lax
from jax.experimental import pallas as pl
from jax.experimental.pallas import tpu as pltpu

F32 = jnp.float32
BF16 = jnp.bfloat16
I32 = jnp.int32

HEAD_DIM = 64
N_Q_HEADS = 8
N_KV_HEADS = 2
GRID_W = 64
ROPE_THETA = 10000.0
GLA_HEADS = 4
GLA_DK = 64
GLA_DV = 128
GLA_LOWRANK = 16
GLA_GATE_NORM = 16.0
N_EXPERTS = 16
EC_CAPACITY = 2
NORM_EPS = 1e-6
LOG2E = 1.4426950408889634

ATTN_W = N_Q_HEADS * HEAD_DIM
KV_W = N_KV_HEADS * HEAD_DIM
GLA_KW = GLA_HEADS * GLA_DK
GLA_VW = GLA_HEADS * GLA_DV

LANES = 128
BF16_ROWS = 16
VMEM_LIMIT = 56 * 1024 * 1024

GLA_CHUNK = 128
GLA_LEVELS = 7
GLA_STEP_CHUNKS = 8


def _cparams(sem):
    return pltpu.CompilerParams(dimension_semantics=sem, vmem_limit_bytes=VMEM_LIMIT)


def _dot(a, b):
    return jnp.dot(a, b, preferred_element_type=F32)


def _dot_nt(a, b):
    return lax.dot_general(a, b, (((1,), (1,)), ((), ())), preferred_element_type=F32)


def _dot_tn(a, b):
    return lax.dot_general(a, b, (((0,), (0,)), ((), ())), preferred_element_type=F32)


def _split2(x):
    hi = x.astype(BF16)
    lo = (x - hi.astype(F32)).astype(BF16)
    return hi, lo


def _split3(x):
    hi = x.astype(BF16)
    r = x - hi.astype(F32)
    mid = r.astype(BF16)
    lo = (r - mid.astype(F32)).astype(BF16)
    return hi, mid, lo


def _rms(x, gain):
    ms = jnp.mean(x * x, axis=-1, keepdims=True)
    return x * lax.rsqrt(ms + NORM_EPS) * gain


def _sigmoid(x):
    return 1.0 / (1.0 + jnp.exp(-x))


def _full_spec(shape):
    nd = len(shape)
    return pl.BlockSpec(shape, lambda *_: (0,) * nd)


def _head_ms(x, bd):
    hi, lo = _split2(x * x)
    return (_dot(hi, bd) + _dot(lo, bd)) * (1.0 / HEAD_DIM)


def _rope(x, cos, sin):
    w = x.shape[-1]
    lane = lax.broadcasted_iota(I32, x.shape, 1)
    fwd = pltpu.roll(x, w - 16, axis=1)
    bwd = pltpu.roll(x, 16, axis=1)
    partner = jnp.where((lane % 32) < 16, fwd, bwd)
    return x * cos + partner * sin


def _in_proj_kernel(x_ref, cos_ref, sin_ref, gmix_ref, wqk_ref, wvt_ref, wgla_ref, wlr_ref,
                    wgate_ref, bgate_ref, qn_ref, kn_ref, bdq_ref, bdk_ref,
                    q_ref, k_ref, vt_ref, lq_ref, lk_ref, lv_ref, lr_ref, gf_ref, gb_ref):
    x = x_ref[...]
    h = _rms(x, gmix_ref[...]).astype(BF16)

    qk = _dot(h, wqk_ref[...])
    cos = cos_ref[...]
    sin = sin_ref[...]
    q = qk[:, :ATTN_W]
    q = q * lax.rsqrt(_head_ms(q, bdq_ref[...]) + NORM_EPS) * qn_ref[...]
    q = _rope(q, cos, sin) * (HEAD_DIM ** -0.5 * LOG2E)
    q_ref[...] = q.astype(BF16)
    k = qk[:, ATTN_W:]
    k = k * lax.rsqrt(_head_ms(k, bdk_ref[...]) + NORM_EPS) * kn_ref[...]
    k = _rope(k, cos[:, :KV_W], sin[:, :KV_W])
    k_ref[...] = k.astype(BF16)
    vt_ref[0] = _dot_nt(wvt_ref[...], h).astype(BF16)

    gla = _dot(h, wgla_ref[...])
    lq_ref[...] = gla[:, :GLA_KW] * (GLA_DK ** -0.5)
    lk_ref[...] = gla[:, GLA_KW:2 * GLA_KW]
    lv_ref[...] = gla[:, 2 * GLA_KW:2 * GLA_KW + GLA_VW].astype(BF16)
    lr_ref[...] = gla[:, 2 * GLA_KW + GLA_VW:]

    a = _dot(h, wlr_ref[...]).astype(BF16)
    z = _dot(a, wgate_ref[...]) + bgate_ref[...]
    g = (jnp.minimum(z, 0.0) - jnp.log(1.0 + jnp.exp(-jnp.abs(z)))) * (1.0 / GLA_GATE_NORM)
    gf_ref[...] = g[:, :GLA_KW]
    gb_ref[...] = g[:, GLA_KW:]


def _in_proj(x2d, T, tabs, w, tm=512):
    n, d = x2d.shape
    nt = T // tm
    row = lambda wd: pl.BlockSpec((tm, wd), lambda i: (i, 0))
    tab = pl.BlockSpec((tm, ATTN_W), lambda i: (i % nt, 0))
    consts = [w["gmix"], w["wqk"], w["wvt"], w["wgla"], w["wlr"], w["wgate"], w["bgate"],
              w["qn"], w["kn"], w["bdq"], w["bdk"]]
    outs = [(ATTN_W, BF16), (KV_W, BF16), None, (GLA_KW, F32), (GLA_KW, F32), (GLA_VW, BF16),
            (GLA_VW, F32), (GLA_KW, F32), (GLA_KW, F32)]
    vt_spec = pl.BlockSpec((1, KV_W, tm), lambda i: (i // nt, 0, i % nt))
    vt_shape = jax.ShapeDtypeStruct((n // T, KV_W, T), BF16)
    return pl.pallas_call(
        _in_proj_kernel,
        grid=(n // tm,),
        in_specs=[row(d), tab, tab] + [_full_spec(c.shape) for c in consts],
        out_specs=[vt_spec if o is None else row(o[0]) for o in outs],
        out_shape=[vt_shape if o is None else jax.ShapeDtypeStruct((n, o[0]), o[1]) for o in outs],
        compiler_params=_cparams(("parallel",)),
        name="in_proj",
    )(x2d, tabs[0], tabs[1], *consts)


def _attn_kernel(q_ref, k_ref, vt_ref, o_ref, qpad_ref, m_ref, acc_ref, s_even, s_odd, mx_even, mx_odd, *, nk):
    j = pl.program_id(2)
    group = N_Q_HEADS // N_KV_HEADS
    even = (s_even, mx_even)
    odd = (s_odd, mx_odd)

    def head_scores(dst, hq):
        s = _dot(k_ref[0], qpad_ref[hq])
        dst[0][hq] = s
        dst[1][hq] = jnp.max(s, axis=0, keepdims=True)

    def head_softmax(src, hq):
        m_old = m_ref[hq]
        m_new = jnp.maximum(m_old, src[1][hq])
        alpha = jnp.exp2(m_old - m_new)
        p = jnp.exp2(src[0][hq] - m_new)
        m_ref[hq] = m_new
        return p.astype(BF16), alpha

    def head_values(hq, p, alpha):
        hk = hq // group
        ones = jnp.ones((BF16_ROWS, vt_ref.shape[2]), BF16)
        vt = vt_ref[0, hk * HEAD_DIM:(hk + 1) * HEAD_DIM, :]
        vt1 = jnp.concatenate([vt, ones], axis=0)
        acc_ref[hq] = alpha * acc_ref[hq] + _dot(vt1, p)

    def scores_into(dst):
        for hq in range(N_Q_HEADS):
            head_scores(dst, hq)

    def consume(src, dst=None):
        pending = None
        for hq in range(N_Q_HEADS):
            if dst is not None:
                head_scores(dst, hq)
            p, alpha = head_softmax(src, hq)
            if pending is not None:
                head_values(*pending)
            pending = (hq, p, alpha)
        head_values(*pending)

    @pl.when(j == 0)
    def _():
        m_ref[...] = jnp.full(m_ref.shape, -jnp.inf, F32)
        acc_ref[...] = jnp.zeros(acc_ref.shape, F32)
        lane = lax.broadcasted_iota(I32, (q_ref.shape[1], LANES), 1)
        for hq in range(N_Q_HEADS):
            hk = hq // group
            chunk = q_ref[0, :, (hq // 2) * LANES:(hq // 2 + 1) * LANES].astype(F32)
            if hq % 2 != hk:
                chunk = pltpu.roll(chunk, HEAD_DIM, axis=1)
            keep = (lane // HEAD_DIM) == hk
            qpad_ref[hq] = jnp.where(keep, chunk, 0.0).T.astype(BF16)
        scores_into(even)

    @pl.when((j > 0) & (j < nk) & (lax.rem(j, 2) == 1))
    def _():
        consume(even, odd)

    @pl.when((j > 0) & (j < nk) & (lax.rem(j, 2) == 0))
    def _():
        consume(odd, even)

    @pl.when(j == nk)
    def _():
        consume(odd if nk % 2 == 0 else even)
        outs = [acc_ref[hq, :HEAD_DIM, :] / acc_ref[hq, HEAD_DIM:HEAD_DIM + 1, :] for hq in range(N_Q_HEADS)]
        o_ref[0] = jnp.concatenate(outs, axis=0).T.astype(BF16)


def _attention(q, k, vt, tq=1024, tk=512):
    b, t, _ = q.shape
    nk = t // tk
    s_buf = pltpu.VMEM((N_Q_HEADS, tk, tq), F32)
    mx_buf = pltpu.VMEM((N_Q_HEADS, 1, tq), F32)
    return pl.pallas_call(
        functools.partial(_attn_kernel, nk=nk),
        grid=(b, t // tq, nk + 1),
        in_specs=[pl.BlockSpec((1, tq, ATTN_W), lambda bi, i, j: (bi, i, 0)),
                  pl.BlockSpec((1, tk, KV_W), lambda bi, i, j: (bi, jnp.minimum(j, nk - 1), 0)),
                  pl.BlockSpec((1, KV_W, tk), lambda bi, i, j: (bi, 0, jnp.maximum(j - 1, 0)))],
        out_specs=pl.BlockSpec((1, tq, ATTN_W), lambda bi, i, j: (bi, i, 0)),
        out_shape=jax.ShapeDtypeStruct((b, t, ATTN_W), BF16),
        scratch_shapes=[pltpu.VMEM((N_Q_HEADS, LANES, tq), BF16),
                        pltpu.VMEM((N_Q_HEADS, 1, tq), F32),
                        pltpu.VMEM((N_Q_HEADS, HEAD_DIM + BF16_ROWS, tq), F32),
                        s_buf, s_buf, mx_buf, mx_buf],
        compiler_params=_cparams(("parallel", "parallel", "arbitrary")),
        name="attention",
    )(q, k, vt)


def _gla_chunk(q, k, v, g, st_ref, mask_ref, reverse):
    c = GLA_CHUNK
    pair_w = 2 * GLA_DK
    t_kw = lax.broadcasted_iota(I32, (c, GLA_KW), 0)
    lo_lanes = lax.broadcasted_iota(I32, (c, pair_w), 1) < GLA_DK

    a_pairs = [jnp.zeros((c, 2 * c), F32) for _ in range(2)]

    def add_level(a_pairs, qh, kh, mask2):
        qb = qh.astype(BF16)
        kb = kh.astype(BF16)
        out = []
        for p in range(2):
            qp = qb[:, p * pair_w:(p + 1) * pair_w]
            kp = kb[:, p * pair_w:(p + 1) * pair_w]
            k2 = jnp.concatenate([jnp.where(lo_lanes, kp, 0.0), jnp.where(lo_lanes, 0.0, kp)], axis=0)
            out.append(a_pairs[p] + mask2 * _dot_nt(qp, k2))
        return out

    pre = g
    tot = g
    for lvl in range(GLA_LEVELS):
        s = 1 << lvl
        odd = ((t_kw >> lvl) & 1) == 1
        if not reverse:
            qh = q * jnp.exp(pre)
            kh = k * jnp.exp(tot - pre)
        else:
            qh = q * jnp.exp(tot - pre + g)
            kh = k * jnp.exp(pre - g)
        a_pairs = add_level(a_pairs, qh, kh, mask_ref[lvl])
        sib = jnp.where(odd, pltpu.roll(tot, s, axis=0), pltpu.roll(tot, c - s, axis=0))
        pre = pre + jnp.where(odd, sib, 0.0)
        tot = tot + sib
    if not reverse:
        a_pairs = add_level(a_pairs, q, k, mask_ref[GLA_LEVELS])

    if not reverse:
        q_dec = q * jnp.exp(pre)
        k_dec = k * jnp.exp(tot - pre)
    else:
        q_dec = q * jnp.exp(tot - pre + g)
        k_dec = k * jnp.exp(pre - g)
    chunk_decay = jnp.exp(tot[0:1, :])
    sub_pair = lax.broadcasted_iota(I32, (2 * GLA_DV, pair_w), 0)
    lane_st = lax.broadcasted_iota(I32, (2 * GLA_DV, pair_w), 1)
    block_diag = (sub_pair // GLA_DV) == (lane_st // GLA_DK)
    first_head_v = lax.broadcasted_iota(I32, (c, 2 * GLA_DV), 1) < GLA_DV

    outs = []
    for p in range(2):
        st = st_ref[p]
        qd = q_dec[:, p * pair_w:(p + 1) * pair_w].astype(BF16)
        o_pair = _dot_nt(qd, st.astype(BF16))
        vp = v[:, p * 2 * GLA_DV:(p + 1) * 2 * GLA_DV].astype(BF16)
        kd = k_dec[:, p * pair_w:(p + 1) * pair_w].astype(BF16)
        upd = _dot_tn(vp, kd)
        st_ref[p] = st * chunk_decay[:, p * pair_w:(p + 1) * pair_w] + jnp.where(block_diag, upd, 0.0)
        v_diag = jnp.concatenate([jnp.where(first_head_v, vp, 0.0), jnp.where(first_head_v, 0.0, vp)], axis=0)
        outs.append(o_pair + _dot(a_pairs[p].astype(BF16), v_diag))
    return jnp.concatenate(outs, axis=1)


def _gla_level_masks(reverse):
    c = GLA_CHUNK
    t = np.arange(c)[:, None]
    s = np.arange(c)[None, :]
    masks = []
    for lvl in range(GLA_LEVELS):
        bt, bs = t >> lvl, s >> lvl
        masks.append((bs == bt + 1) & (bt % 2 == 0) if reverse else (bt == bs + 1) & (bs % 2 == 0))
    masks.append(t == s)
    m = np.stack(masks).astype(np.float32)
    return jnp.asarray(np.concatenate([m, m], axis=2))


def _gla_bwd_kernel(q_ref, k_ref, v_ref, g_ref, mask_ref, o_ref, st_ref):
    @pl.when(pl.program_id(1) == 0)
    def _():
        st_ref[...] = jnp.zeros(st_ref.shape, F32)

    for sub in reversed(range(GLA_STEP_CHUNKS)):
        rows = slice(sub * GLA_CHUNK, (sub + 1) * GLA_CHUNK)
        o_ref[0, rows, :] = _gla_chunk(q_ref[0, rows, :], k_ref[0, rows, :], v_ref[0, rows, :], g_ref[0, rows, :],
                                       st_ref, mask_ref, True)


def _gla_fwd_kernel(q_ref, k_ref, v_ref, g_ref, ob_ref, r_ref, gn_ref, mask_ref, o_ref, st_ref):
    @pl.when(pl.program_id(1) == 0)
    def _():
        st_ref[...] = jnp.zeros(st_ref.shape, F32)

    gn = gn_ref[...]
    for sub in range(GLA_STEP_CHUNKS):
        rows = slice(sub * GLA_CHUNK, (sub + 1) * GLA_CHUNK)
        o = _gla_chunk(q_ref[0, rows, :], k_ref[0, rows, :], v_ref[0, rows, :], g_ref[0, rows, :],
                       st_ref, mask_ref, False) + ob_ref[0, rows, :]
        r = r_ref[0, rows, :]
        gate = r * _sigmoid(r)
        outs = []
        for h in range(GLA_HEADS):
            oh = o[:, h * GLA_DV:(h + 1) * GLA_DV]
            outs.append(_rms(oh, gn))
        o_ref[0, rows, :] = (jnp.concatenate(outs, axis=1) * gate).astype(BF16)


def _gla(lq, lk, lv, lr, gf, gb, gla_norm):
    b, t, _ = lq.shape
    c = GLA_CHUNK * GLA_STEP_CHUNKS
    nc = t // c
    fwd = lambda wd: pl.BlockSpec((1, c, wd), lambda bi, ci: (bi, ci, 0))
    rev = lambda wd: pl.BlockSpec((1, c, wd), lambda bi, ci: (bi, nc - 1 - ci, 0))
    st = pltpu.VMEM((2, 2 * GLA_DV, 2 * GLA_DK), F32)
    mask_b = _gla_level_masks(True)
    mask_f = _gla_level_masks(False)
    o_b = pl.pallas_call(
        _gla_bwd_kernel,
        grid=(b, nc),
        in_specs=[rev(GLA_KW), rev(GLA_KW), rev(GLA_VW), rev(GLA_KW), _full_spec(mask_b.shape)],
        out_specs=rev(GLA_VW),
        out_shape=jax.ShapeDtypeStruct((b, t, GLA_VW), F32),
        scratch_shapes=[st],
        compiler_params=_cparams(("parallel", "arbitrary")),
        name="gla_bwd",
    )(lq, lk, lv, gb, mask_b)
    return pl.pallas_call(
        _gla_fwd_kernel,
        grid=(b, nc),
        in_specs=[fwd(GLA_KW), fwd(GLA_KW), fwd(GLA_VW), fwd(GLA_KW), fwd(GLA_VW), fwd(GLA_VW),
                  _full_spec(gla_norm.shape), _full_spec(mask_f.shape)],
        out_specs=fwd(GLA_VW),
        out_shape=jax.ShapeDtypeStruct((b, t, GLA_VW), BF16),
        scratch_shapes=[st],
        compiler_params=_cparams(("parallel", "arbitrary")),
        name="gla_fwd",
    )(lq, lk, lv, gf, o_b, lr, gla_norm, mask_f)


def _post_mix_kernel(x_ref, oa_ref, og_ref, gmix_ref, wga_ref, wgl_ref, wau_ref, wgu_ref, wout_ref, gmoe_ref,
                     wr_hi_ref, wr_lo_ref, x1_ref, aff_ref):
    x = x_ref[...]
    h = _rms(x, gmix_ref[...]).astype(BF16)
    merged = (_sigmoid(_dot(h, wga_ref[...])) * _dot(oa_ref[...], wau_ref[...])
              + _sigmoid(_dot(h, wgl_ref[...])) * _dot(og_ref[...], wgu_ref[...]))
    x1 = x + _dot(merged.astype(BF16), wout_ref[...])
    x1_ref[...] = x1
    hm = _rms(x1, gmoe_ref[...])
    h_hi, h_lo = _split2(hm)
    wr_hi = wr_hi_ref[...]
    both = _dot_nt(jnp.concatenate([wr_hi, wr_lo_ref[...]], axis=0), h_hi)
    logits = both[:N_EXPERTS] + both[N_EXPERTS:] + _dot_nt(wr_hi, h_lo)
    m = jnp.max(logits, axis=0, keepdims=True)
    e = jnp.exp(logits - m)
    aff_ref[...] = e / jnp.sum(e, axis=0, keepdims=True)


def _post_mix(x2d, oa, og, w, tm=512):
    n, d = x2d.shape
    row = lambda wd: pl.BlockSpec((tm, wd), lambda i: (i, 0))
    consts = [w["gmix"], w["wga"], w["wgl"], w["wau"], w["wgu"], w["wout"], w["gmoe"], w["wr_hi"], w["wr_lo"]]
    return pl.pallas_call(
        _post_mix_kernel,
        grid=(n // tm,),
        in_specs=[row(d), row(ATTN_W), row(GLA_VW)] + [_full_spec(c.shape) for c in consts],
        out_specs=[row(d), pl.BlockSpec((N_EXPERTS, tm), lambda i: (0, i))],
        out_shape=[jax.ShapeDtypeStruct((n, d), F32), jax.ShapeDtypeStruct((N_EXPERTS, n), F32)],
        compiler_params=_cparams(("parallel",)),
        name="post_mix",
    )(x2d, oa, og, *consts)


ROUTE_SLOTS = 256


def _route_kernel(a_ref, idx_ref, gate_ref, pos_ref, lo_ref, *, cap):
    a = a_ref[0]
    nrow = a.shape[0]
    b_io = lax.broadcasted_iota(I32, (nrow, LANES), 0)
    t_io = lax.broadcasted_iota(I32, (nrow, LANES), 1)
    tok = b_io * LANES + t_io

    def as_float(word):
        return lax.bitcast_convert_type(jnp.full((1, LANES), word, I32), F32)

    def thr_step(i, thr):
        cand = thr | lax.shift_left(jnp.int32(1), jnp.int32(30) - i)
        cnt = jnp.sum((a >= as_float(cand)).astype(I32))
        return jnp.where(cnt >= cap, cand, thr)

    thr = lax.fori_loop(0, 31, thr_step, jnp.int32(0))
    gt = a >= as_float(thr + 1)
    eq = (a >= as_float(thr)) & jnp.logical_not(gt)
    need = cap - jnp.sum(gt.astype(I32))

    def tie_step(i, lim):
        cand = lim + lax.shift_left(jnp.int32(1), jnp.int32(17) - i)
        cnt = jnp.sum((eq & (tok < cand)).astype(I32))
        return jnp.where(cnt < need, cand, lim)

    lim = lax.fori_loop(0, 18, tie_step, jnp.int32(0))
    selm = gt | (eq & (tok <= lim))
    selb = selm.astype(F32).astype(BF16)

    r_io = lax.broadcasted_iota(I32, (LANES, LANES), 0)
    c_io = lax.broadcasted_iota(I32, (LANES, LANES), 1)
    utri = (r_io <= c_io).astype(BF16)
    cum_row = _dot(selb, utri)
    row_cnt = cum_row[:, LANES - 1:LANES]
    br_io = lax.broadcasted_iota(I32, (nrow, nrow), 0)
    bc_io = lax.broadcasted_iota(I32, (nrow, nrow), 1)
    lrow = (bc_io <= br_io).astype(BF16)
    row_incl = _dot(lrow, jnp.broadcast_to(row_cnt, (nrow, LANES)).astype(BF16))
    row_excl = row_incl - row_cnt
    pos_ref[0] = jnp.where(selm, row_excl + cum_row - 1.0, -1.0).astype(I32)
    lo_ref[0] = row_excl.astype(I32)
    row_incl_c = row_incl[:, 0:1]
    row_excl_c = row_excl[:, 0:1]
    cum_t = cum_row.T.astype(BF16)
    a_hi, a_mid, a_lo = _split3(a.T)
    sw = ROUTE_SLOTS
    brow = lax.broadcasted_iota(I32, (nrow, sw), 0).astype(F32)
    tsub = lax.broadcasted_iota(I32, (LANES, sw), 0).astype(F32)
    lane = lax.broadcasted_iota(I32, (1, sw), 1).astype(F32)

    def slot_tile(jt, carry):
        j = lane + jnp.asarray(jt, F32) * float(sw)
        bj = jnp.sum((row_incl_c <= j).astype(F32), axis=0, keepdims=True)
        onehot = (brow == bj).astype(F32)
        base = jnp.sum(onehot * row_excl_c, axis=0, keepdims=True)
        jl = j - base
        ohb = onehot.astype(BF16)
        rt = _dot(cum_t, ohb)
        tj = jnp.sum((rt <= jl).astype(F32), axis=0, keepdims=True)
        acol = _dot(a_hi, ohb) + _dot(a_mid, ohb) + _dot(a_lo, ohb)
        gate = jnp.sum(jnp.where(tsub == tj, acol, 0.0), axis=0, keepdims=True)
        tok_id = (bj * float(LANES) + tj).astype(I32)
        for c in range(sw // LANES):
            row = jt * (sw // LANES) + c
            idx_ref[0, pl.ds(row, 1), :] = tok_id[:, c * LANES:(c + 1) * LANES]
            gate_ref[0, pl.ds(row, 1), :] = gate[:, c * LANES:(c + 1) * LANES]
        return carry

    lax.fori_loop(0, cap // sw, slot_tile, 0)


def _route(aff_t, cap):
    e, n = aff_t.shape
    nrow = n // LANES
    a3 = aff_t.reshape(e, nrow, LANES)
    nt = cap // LANES
    slot_spec = pl.BlockSpec((1, nt, LANES), lambda i: (i, 0, 0))
    tok_spec = pl.BlockSpec((1, nrow, LANES), lambda i: (i, 0, 0))
    idx, gate, pos, lo = pl.pallas_call(
        functools.partial(_route_kernel, cap=cap),
        grid=(e,),
        in_specs=[tok_spec],
        out_specs=[slot_spec, slot_spec, tok_spec, tok_spec],
        out_shape=[jax.ShapeDtypeStruct((e, nt, LANES), I32), jax.ShapeDtypeStruct((e, nt, LANES), F32),
                   jax.ShapeDtypeStruct((e, nrow, LANES), I32), jax.ShapeDtypeStruct((e, nrow, LANES), I32)],
        compiler_params=_cparams(("parallel",)),
        name="route",
    )(a3)
    lo_tab = jnp.concatenate([lo[:, :, 0], jnp.full((e, 1), cap, I32)], axis=1)
    return idx, gate, pos.reshape(e, n), lo_tab


MOE_ISSUE_UNROLL = 8


def _moe_kernel(idx_ref, idxn_ref, gate_ref, hm_hbm, gmoe_ref, wg_ref, wu_ref, wd_ref, ye_ref, buf_a, buf_b, sem,
                *, rows):
    nblk = pl.num_programs(1)
    s = pl.program_id(0) * nblk + pl.program_id(1)
    nsteps = pl.num_programs(0) * nblk

    def buffer_wait(buf, k):
        pltpu.make_async_copy(hm_hbm.at[pl.ds(0, rows)], buf, sem.at[k]).wait()

    @pl.when(s == 0)
    def _():
        def body(r0, c):
            for u in range(MOE_ISSUE_UNROLL):
                r = r0 * MOE_ISSUE_UNROLL + u
                n = idx_ref[0, 0, r]
                pltpu.make_async_copy(hm_hbm.at[pl.ds(n, 1)], buf_a.at[pl.ds(r, 1)], sem.at[0]).start()
            return c
        lax.fori_loop(0, rows // MOE_ISSUE_UNROLL, body, 0)

    def step(cur, k_cur, nxt, k_nxt):
        buffer_wait(cur, k_cur)

        @pl.when(s < nsteps)
        def _():
            for r in range(rows):
                n = idxn_ref[0, 0, r]
                pltpu.make_async_copy(hm_hbm.at[pl.ds(n, 1)], nxt.at[pl.ds(r, 1)], sem.at[k_nxt]).start()

        xe = _rms(cur[...], gmoe_ref[...]).astype(BF16)
        hid_g = _dot(xe, wg_ref[0])
        hid = (hid_g * _sigmoid(hid_g)) * _dot(xe, wu_ref[0])
        y = _dot(hid.astype(BF16), wd_ref[0])
        eye = (lax.broadcasted_iota(I32, (LANES, LANES), 0) == lax.broadcasted_iota(I32, (LANES, LANES), 1))
        for c in range(rows // LANES):
            g_row = gate_ref[0, :, c * LANES:(c + 1) * LANES]
            g_col = jnp.sum(jnp.where(eye, g_row, 0.0), axis=1, keepdims=True)
            sl = slice(c * LANES, (c + 1) * LANES)
            ye_ref[sl, :] = (y[sl, :] * g_col).astype(BF16)

        @pl.when(s == nsteps - 1)
        def _():
            buffer_wait(nxt, k_nxt)

    @pl.when(lax.rem(s, 2) == 0)
    def _():
        step(buf_a, 0, buf_b, 1)

    @pl.when(lax.rem(s, 2) == 1)
    def _():
        step(buf_b, 1, buf_a, 0)


def _moe_ffn(idx, gate, hm, gmoe, wg, wu, wd, rows=512):
    e, nt, _ = idx.shape
    cap = nt * LANES
    n, d = hm.shape
    nblk = cap // rows
    nsteps = e * nblk
    idx3 = idx.reshape(nsteps, 1, rows)
    gate3 = gate.reshape(nsteps, 1, rows)
    ff = wg.shape[-1]
    return pl.pallas_call(
        functools.partial(_moe_kernel, rows=rows),
        grid=(e, nblk),
        in_specs=[pl.BlockSpec((1, 1, rows), lambda ei, j: (ei * nblk + j, 0, 0), memory_space=pltpu.SMEM),
                  pl.BlockSpec((1, 1, rows), lambda ei, j: (jnp.minimum(ei * nblk + j + 1, nsteps - 1), 0, 0),
                               memory_space=pltpu.SMEM),
                  pl.BlockSpec((1, 1, rows), lambda ei, j: (ei * nblk + j, 0, 0)),
                  pl.BlockSpec(memory_space=pl.ANY),
                  pl.BlockSpec((1, d), lambda ei, j: (0, 0)),
                  pl.BlockSpec((1, d, ff), lambda ei, j: (ei, 0, 0)),
                  pl.BlockSpec((1, d, ff), lambda ei, j: (ei, 0, 0)),
                  pl.BlockSpec((1, ff, d), lambda ei, j: (ei, 0, 0))],
        out_specs=pl.BlockSpec((rows, d), lambda ei, j: (ei * nblk + j, 0)),
        out_shape=jax.ShapeDtypeStruct((e * cap, d), BF16),
        scratch_shapes=[pltpu.VMEM((rows, d), F32), pltpu.VMEM((rows, d), F32), pltpu.SemaphoreType.DMA((2,))],
        compiler_params=_cparams(("arbitrary", "arbitrary")),
        name="moe_ffn",
    )(idx3, idx3, gate3, hm, gmoe, wg, wu, wd)


COMBINE_WIN = 64


def _ple_final_kernel(lo_ref, x_ref, p_ref, pos_ref, ye_hbm, gple_ref, wpg_ref, wpp_ref, gfin_ref, o_ref,
                      win_a, win_b, acc_ref, sem, *, cap, rows_per_tile):
    i = pl.program_id(0)
    ntiles = pl.num_programs(0)
    win = COMBINE_WIN
    tm = x_ref.shape[0]

    def tile_floors(ti):
        return [jnp.bitwise_and(lo_ref[e, ti * rows_per_tile], -BF16_ROWS) for e in range(N_EXPERTS)]

    def clamp(floors):
        return [jnp.minimum(f, cap - win) for f in floors]

    def start_fetch(floors, buf, k):
        starts = clamp(floors)
        for e in range(N_EXPERTS):
            src = ye_hbm.at[pl.ds(pl.multiple_of(e * cap + starts[e], BF16_ROWS), win)]
            pltpu.make_async_copy(src, buf.at[pl.ds(e * win, win)], sem.at[k]).start()

    def wait_fetch(buf, k):
        pltpu.make_async_copy(ye_hbm.at[pl.ds(0, N_EXPERTS * win)], buf, sem.at[k]).wait()

    def expand(floors, buf):
        starts = clamp(floors)
        pos = pos_ref[...]
        w_io = lax.broadcasted_iota(I32, (win, tm), 0)
        parts = []
        for e in range(N_EXPERTS):
            pe = pos[e:e + 1, :]
            hit = ((pe - starts[e]) == w_io) & (pe >= floors[e])
            parts.append(jnp.where(hit, 1.0, 0.0).astype(BF16))
        onehot_t = jnp.concatenate(parts, axis=0)
        return _dot_tn(onehot_t, buf[...])

    @pl.when(i == 0)
    def _():
        start_fetch(tile_floors(0), win_a, 0)

    def step(cur, k_cur, nxt, k_nxt):
        floors0 = tile_floors(i)
        wait_fetch(cur, k_cur)

        @pl.when(i + 1 < ntiles)
        def _():
            start_fetch(tile_floors(i + 1), nxt, k_nxt)

        acc_ref[...] = x_ref[...] + expand(floors0, cur)

        nrounds = jnp.int32(1)
        for e in range(N_EXPERTS):
            hi = lo_ref[e, (i + 1) * rows_per_tile]
            nrounds = jnp.maximum(nrounds, lax.div(hi - floors0[e] + (win - 1), jnp.int32(win)))

        def extra_round(r, c):
            floors = [f + r * win for f in floors0]
            start_fetch(floors, cur, k_cur)
            wait_fetch(cur, k_cur)
            acc_ref[...] += expand(floors, cur)
            return c

        lax.fori_loop(1, nrounds, extra_round, 0)

        x = acc_ref[...]
        hp = _rms(x, gple_ref[...]).astype(BF16)
        gate = _sigmoid(_dot(hp, wpg_ref[...]))
        y = x + gate * _dot(p_ref[...].astype(BF16), wpp_ref[...])
        o_ref[...] = _rms(y, gfin_ref[...])

    @pl.when(lax.rem(i, 2) == 0)
    def _():
        step(win_a, 0, win_b, 1)

    @pl.when(lax.rem(i, 2) == 1)
    def _():
        step(win_b, 1, win_a, 0)


def _ple_final(x1, p2d, pos, lo_tab, ye, cap, w, tm=256):
    n, d = x1.shape
    pd = p2d.shape[-1]
    row = lambda wd: pl.BlockSpec((tm, wd), lambda i, lo: (i, 0))
    consts = [w["gple"], w["wpg"], w["wpp"], w["gfin"]]
    const_spec = lambda c: pl.BlockSpec(c.shape, lambda i, lo: (0,) * c.ndim)
    grid_spec = pltpu.PrefetchScalarGridSpec(
        num_scalar_prefetch=1,
        grid=(n // tm,),
        in_specs=[row(d), row(pd), pl.BlockSpec((N_EXPERTS, tm), lambda i, lo: (0, i)),
                  pl.BlockSpec(memory_space=pl.ANY)] + [const_spec(c) for c in consts],
        out_specs=row(d),
        scratch_shapes=[pltpu.VMEM((N_EXPERTS * COMBINE_WIN, d), BF16),
                        pltpu.VMEM((N_EXPERTS * COMBINE_WIN, d), BF16), pltpu.VMEM((tm, d), F32),
                        pltpu.SemaphoreType.DMA((2,))],
    )
    return pl.pallas_call(
        functools.partial(_ple_final_kernel, cap=cap, rows_per_tile=tm // LANES),
        grid_spec=grid_spec,
        out_shape=jax.ShapeDtypeStruct((n, d), F32),
        compiler_params=_cparams(("arbitrary",)),
        name="ple_final",
    )(lo_tab, x1, p2d, pos, ye, *consts)


def _rope_tables(T):
    half = HEAD_DIM // 2
    inv = ROPE_THETA ** (-jnp.arange(0, half, 2, dtype=F32) / half)
    t = jnp.arange(T)
    ang_r = (t // GRID_W).astype(F32)[:, None] * inv
    ang_c = (t % GRID_W).astype(F32)[:, None] * inv
    cos = jnp.concatenate([jnp.cos(ang_r)] * 2 + [jnp.cos(ang_c)] * 2, axis=1)
    sin = jnp.concatenate([-jnp.sin(ang_r), jnp.sin(ang_r), -jnp.sin(ang_c), jnp.sin(ang_c)], axis=1)
    return jnp.tile(cos, (1, N_Q_HEADS)), jnp.tile(sin, (1, N_Q_HEADS))


def _block_diag_ones(width):
    i = np.arange(width)
    return jnp.asarray((i[:, None] // HEAD_DIM) == (i[None, :] // HEAD_DIM), BF16)


def _prep_weights(g_mix, w_in, q_norm, k_norm, w_af, b_af, w_ab, b_ab, gla_norm, w_attn_up, w_gla_up,
                  w_out, g_moe, w_router, w_gate_e, w_up_e, w_down_e, g_ple, w_ple_gate, w_ple_proj,
                  g_final):
    d = w_in.shape[0]
    o_gla = ATTN_W + 2 * KV_W
    o_lr = o_gla + 2 * GLA_KW + 2 * GLA_VW
    o_ga = o_lr + 2 * GLA_LOWRANK
    wlr = jnp.zeros((d, LANES), F32).at[:, :2 * GLA_LOWRANK].set(w_in[:, o_lr:o_ga])
    wgate = jnp.zeros((LANES, 2 * GLA_KW), F32)
    wgate = wgate.at[:GLA_LOWRANK, :GLA_KW].set(w_af)
    wgate = wgate.at[GLA_LOWRANK:2 * GLA_LOWRANK, GLA_KW:].set(w_ab)
    wr_t = w_router.T
    wr_hi = wr_t.astype(BF16)
    wr_lo = (wr_t - wr_hi.astype(F32)).astype(BF16)
    return dict(
        gmix=g_mix.reshape(1, d),
        wqk=w_in[:, :ATTN_W + KV_W].astype(BF16),
        wvt=w_in[:, ATTN_W + KV_W:o_gla].T.astype(BF16),
        wgla=w_in[:, o_gla:o_lr].astype(BF16),
        wlr=wlr.astype(BF16),
        wga=w_in[:, o_ga:o_ga + d].astype(BF16),
        wgl=w_in[:, o_ga + d:].astype(BF16),
        wgate=wgate.astype(BF16),
        bgate=jnp.concatenate([b_af, b_ab]).reshape(1, 2 * GLA_KW),
        qn=jnp.tile(q_norm, N_Q_HEADS).reshape(1, ATTN_W),
        kn=jnp.tile(k_norm, N_KV_HEADS).reshape(1, KV_W),
        bdq=_block_diag_ones(ATTN_W),
        bdk=_block_diag_ones(KV_W),
        gnorm=gla_norm.reshape(1, GLA_DV),
        wau=w_attn_up.astype(BF16),
        wgu=w_gla_up.astype(BF16),
        wout=w_out.astype(BF16),
        gmoe=g_moe.reshape(1, d),
        wr_hi=wr_hi,
        wr_lo=wr_lo,
        wge=w_gate_e.astype(BF16),
        wue=w_up_e.astype(BF16),
        wde=w_down_e.astype(BF16),
        gple=g_ple.reshape(1, d),
        wpg=w_ple_gate.astype(BF16),
        wpp=w_ple_proj.astype(BF16),
        gfin=g_final.reshape(1, d),
    )


def _trunk(x, p, w):
    b, t, d = x.shape
    n = b * t
    x2d = x.reshape(n, d)
    q, k, vt, lq, lk, lv, lr, gf, gb = _in_proj(x2d, t, _rope_tables(t), w)
    r3 = lambda a: a.reshape(b, t, a.shape[-1])
    o_attn = _attention(r3(q), r3(k), vt).reshape(n, ATTN_W)
    o_gla = _gla(r3(lq), r3(lk), r3(lv), r3(lr), r3(gf), r3(gb), w["gnorm"]).reshape(n, GLA_VW)
    x1, aff_t = _post_mix(x2d, o_attn, o_gla, w)
    cap = EC_CAPACITY * n // N_EXPERTS
    idx, gate, pos, lo_tab = _route(aff_t, cap)
    ye = _moe_ffn(idx, gate, x1, w["gmoe"], w["wge"], w["wue"], w["wde"])
    y = _ple_final(x1, p.reshape(n, p.shape[-1]), pos, lo_tab, ye, cap, w)
    return y.reshape(b, t, d)


def kernel(x_prompt, x_sample, p_prompt, p_sample, g_mix, w_in, q_norm, k_norm, w_af, b_af, w_ab, b_ab,
           gla_norm, w_attn_up, w_gla_up, w_out, g_moe, w_router, w_gate_e, w_up_e, w_down_e, g_ple,
           w_ple_gate, w_ple_proj, g_final):
    assert g_mix.shape[0] == 1, "single layer"
    w = _prep_weights(g_mix[0], w_in[0], q_norm[0], k_norm[0], w_af[0], b_af[0], w_ab[0], b_ab[0],
                      gla_norm[0], w_attn_up[0], w_gla_up[0], w_out[0], g_moe[0], w_router[0],
                      w_gate_e[0], w_up_e[0], w_down_e[0], g_ple[0], w_ple_gate[0], w_ple_proj[0], g_final)
    return (_trunk(x_prompt, p_prompt[0], w), _trunk(x_sample, p_sample[0], w))
```

```python
import functools

import numpy as np
import jax
import jax.numpy as jnp
from jax import lax
from jax.experimental import pallas as pl
from jax.experimental.pallas import tpu as pltpu

F32 = jnp.float32
BF16 = jnp.bfloat16
I32 = jnp.int32

HEAD_DIM = 64
N_Q_HEADS = 8
N_KV_HEADS = 2
GRID_W = 64
ROPE_THETA = 10000.0
GLA_HEADS = 4
GLA_DK = 64
GLA_DV = 128
GLA_LOWRANK = 16
GLA_GATE_NORM = 16.0
N_EXPERTS = 16
EC_CAPACITY = 2
NORM_EPS = 1e-6
LOG2E = 1.4426950408889634

ATTN_W = N_Q_HEADS * HEAD_DIM
KV_W = N_KV_HEADS * HEAD_DIM
GLA_KW = GLA_HEADS * GLA_DK
GLA_VW = GLA_HEADS * GLA_DV

LANES = 128
BF16_ROWS = 16
VMEM_LIMIT = 56 * 1024 * 1024

GLA_CHUNK = 128
GLA_LEVELS = 7
GLA_STEP_CHUNKS = 8


def _cparams(sem):
    return pltpu.CompilerParams(dimension_semantics=sem, vmem_limit_bytes=VMEM_LIMIT)


def _dot(a, b):
    return jnp.dot(a, b, preferred_element_type=F32)


def _dot_nt(a, b):
    return lax.dot_general(a, b, (((1,), (1,)), ((), ())), preferred_element_type=F32)


def _dot_tn(a, b):
    return lax.dot_general(a, b, (((0,), (0,)), ((), ())), preferred_element_type=F32)


def _split2(x):
    hi = x.astype(BF16)
    lo = (x - hi.astype(F32)).astype(BF16)
    return hi, lo


def _split3(x):
    hi = x.astype(BF16)
    r = x - hi.astype(F32)
    mid = r.astype(BF16)
    lo = (r - mid.astype(F32)).astype(BF16)
    return hi, mid, lo


def _rms(x, gain):
    ms = jnp.mean(x * x, axis=-1, keepdims=True)
    return x * lax.rsqrt(ms + NORM_EPS) * gain


def _sigmoid(x):
    return 1.0 / (1.0 + jnp.exp(-x))


def _full_spec(shape):
    nd = len(shape)
    return pl.BlockSpec(shape, lambda *_: (0,) * nd)


def _head_ms(x, bd):
    hi, lo = _split2(x * x)
    return (_dot(hi, bd) + _dot(lo, bd)) * (1.0 / HEAD_DIM)


def _rope(x, cos, sin):
    w = x.shape[-1]
    lane = lax.broadcasted_iota(I32, x.shape, 1)
    fwd = pltpu.roll(x, w - 16, axis=1)
    bwd = pltpu.roll(x, 16, axis=1)
    partner = jnp.where((lane % 32) < 16, fwd, bwd)
    return x * cos + partner * sin


def _in_proj_kernel(x_ref, cos_ref, sin_ref, gmix_ref, wqk_ref, wvt_ref, wgla_ref, wlr_ref,
                    wgate_ref, bgate_ref, qn_ref, kn_ref, bdq_ref, bdk_ref,
                    q_ref, k_ref, vt_ref, lq_ref, lk_ref, lv_ref, lr_ref, gf_ref, gb_ref):
    x = x_ref[...]
    h = _rms(x, gmix_ref[...]).astype(BF16)

    qk = _dot(h, wqk_ref[...])
    cos = cos_ref[...]
    sin = sin_ref[...]
    q = qk[:, :ATTN_W]
    q = q * lax.rsqrt(_head_ms(q, bdq_ref[...]) + NORM_EPS) * qn_ref[...]
    q = _rope(q, cos, sin) * (HEAD_DIM ** -0.5 * LOG2E)
    q_ref[...] = q.astype(BF16)
    k = qk[:, ATTN_W:]
    k = k * lax.rsqrt(_head_ms(k, bdk_ref[...]) + NORM_EPS) * kn_ref[...]
    k = _rope(k, cos[:, :KV_W], sin[:, :KV_W])
    k_ref[...] = k.astype(BF16)
    vt_ref[0] = _dot_nt(wvt_ref[...], h).astype(BF16)

    gla = _dot(h, wgla_ref[...])
    lq_ref[...] = gla[:, :GLA_KW] * (GLA_DK ** -0.5)
    lk_ref[...] = gla[:, GLA_KW:2 * GLA_KW]
    lv_ref[...] = gla[:, 2 * GLA_KW:2 * GLA_KW + GLA_VW].astype(BF16)
    lr_ref[...] = gla[:, 2 * GLA_KW + GLA_VW:]

    a = _dot(h, wlr_ref[...]).astype(BF16)
    z = _dot(a, wgate_ref[...]) + bgate_ref[...]
    g = (jnp.minimum(z, 0.0) - jnp.log(1.0 + jnp.exp(-jnp.abs(z)))) * (1.0 / GLA_GATE_NORM)
    gf_ref[...] = g[:, :GLA_KW]
    gb_ref[...] = g[:, GLA_KW:]


def _in_proj(x2d, T, tabs, w, tm=512):
    n, d = x2d.shape
    nt = T // tm
    row = lambda wd: pl.BlockSpec((tm, wd), lambda i: (i, 0))
    tab = pl.BlockSpec((tm, ATTN_W), lambda i: (i % nt, 0))
    consts = [w["gmix"], w["wqk"], w["wvt"], w["wgla"], w["wlr"], w["wgate"], w["bgate"],
              w["qn"], w["kn"], w["bdq"], w["bdk"]]
    outs = [(ATTN_W, BF16), (KV_W, BF16), None, (GLA_KW, F32), (GLA_KW, F32), (GLA_VW, BF16),
            (GLA_VW, F32), (GLA_KW, F32), (GLA_KW, F32)]
    vt_spec = pl.BlockSpec((1, KV_W, tm), lambda i: (i // nt, 0, i % nt))
    vt_shape = jax.ShapeDtypeStruct((n // T, KV_W, T), BF16)
    return pl.pallas_call(
        _in_proj_kernel,
        grid=(n // tm,),
        in_specs=[row(d), tab, tab] + [_full_spec(c.shape) for c in consts],
        out_specs=[vt_spec if o is None else row(o[0]) for o in outs],
        out_shape=[vt_shape if o is None else jax.ShapeDtypeStruct((n, o[0]), o[1]) for o in outs],
        compiler_params=_cparams(("parallel",)),
        name="in_proj",
    )(x2d, tabs[0], tabs[1], *consts)


def _attn_kernel(q_ref, k_ref, vt_ref, o_ref, qpad_ref, m_ref, acc_ref, s_even, s_odd, mx_even, mx_odd, *, nk):
    j = pl.program_id(2)
    group = N_Q_HEADS // N_KV_HEADS
    even = (s_even, mx_even)
    odd = (s_odd, mx_odd)

    def head_scores(dst, hq):
        s = _dot(k_ref[0], qpad_ref[hq])
        dst[0][hq] = s
        dst[1][hq] = jnp.max(s, axis=0, keepdims=True)

    def head_softmax(src, hq):
        m_old = m_ref[hq]
        m_new = jnp.maximum(m_old, src[1][hq])
        alpha = jnp.exp2(m_old - m_new)
        p = jnp.exp2(src[0][hq] - m_new)
        m_ref[hq] = m_new
        return p.astype(BF16), alpha

    def head_values(hq, p, alpha):
        hk = hq // group
        ones = jnp.ones((BF16_ROWS, vt_ref.shape[2]), BF16)
        vt = vt_ref[0, hk * HEAD_DIM:(hk + 1) * HEAD_DIM, :]
        vt1 = jnp.concatenate([vt, ones], axis=0)
        acc_ref[hq] = alpha * acc_ref[hq] + _dot(vt1, p)

    def scores_into(dst):
        for hq in range(N_Q_HEADS):
            head_scores(dst, hq)

    def consume(src, dst=None):
        pending = None
        for hq in range(N_Q_HEADS):
            if dst is not None:
                head_scores(dst, hq)
            p, alpha = head_softmax(src, hq)
            if pending is not None:
                head_values(*pending)
            pending = (hq, p, alpha)
        head_values(*pending)

    @pl.when(j == 0)
    def _():
        m_ref[...] = jnp.full(m_ref.shape, -jnp.inf, F32)
        acc_ref[...] = jnp.zeros(acc_ref.shape, F32)
        lane = lax.broadcasted_iota(I32, (q_ref.shape[1], LANES), 1)
        for hq in range(N_Q_HEADS):
            hk = hq // group
            chunk = q_ref[0, :, (hq // 2) * LANES:(hq // 2 + 1) * LANES].astype(F32)
            if hq % 2 != hk:
                chunk = pltpu.roll(chunk, HEAD_DIM, axis=1)
            keep = (lane // HEAD_DIM) == hk
            qpad_ref[hq] = jnp.where(keep, chunk, 0.0).T.astype(BF16)
        scores_into(even)

    @pl.when((j > 0) & (j < nk) & (lax.rem(j, 2) == 1))
    def _():
        consume(even, odd)

    @pl.when((j > 0) & (j < nk) & (lax.rem(j, 2) == 0))
    def _():
        consume(odd, even)

    @pl.when(j == nk)
    def _():
        consume(odd if nk % 2 == 0 else even)
        outs = [acc_ref[hq, :HEAD_DIM, :] / acc_ref[hq, HEAD_DIM:HEAD_DIM + 1, :] for hq in range(N_Q_HEADS)]
        o_ref[0] = jnp.concatenate(outs, axis=0).T.astype(BF16)


def _attention(q, k, vt, tq=1024, tk=512):
    b, t, _ = q.shape
    nk = t // tk
    s_buf = pltpu.VMEM((N_Q_HEADS, tk, tq), F32)
    mx_buf = pltpu.VMEM((N_Q_HEADS, 1, tq), F32)
    return pl.pallas_call(
        functools.partial(_attn_kernel, nk=nk),
        grid=(b, t // tq, nk + 1),
        in_specs=[pl.BlockSpec((1, tq, ATTN_W), lambda bi, i, j: (bi, i, 0)),
                  pl.BlockSpec((1, tk, KV_W), lambda bi, i, j: (bi, jnp.minimum(j, nk - 1), 0)),
                  pl.BlockSpec((1, KV_W, tk), lambda bi, i, j: (bi, 0, jnp.maximum(j - 1, 0)))],
        out_specs=pl.BlockSpec((1, tq, ATTN_W), lambda bi, i, j: (bi, i, 0)),
        out_shape=jax.ShapeDtypeStruct((b, t, ATTN_W), BF16),
        scratch_shapes=[pltpu.VMEM((N_Q_HEADS, LANES, tq), BF16),
                        pltpu.VMEM((N_Q_HEADS, 1, tq), F32),
                        pltpu.VMEM((N_Q_HEADS, HEAD_DIM + BF16_ROWS, tq), F32),
                        s_buf, s_buf, mx_buf, mx_buf],
        compiler_params=_cparams(("parallel", "parallel", "arbitrary")),
        name="attention",
    )(q, k, vt)


def _gla_chunk(q, k, v, g, st_ref, mask_ref, reverse):
    c = GLA_CHUNK
    pair_w = 2 * GLA_DK
    t_kw = lax.broadcasted_iota(I32, (c, GLA_KW), 0)
    lo_lanes = lax.broadcasted_iota(I32, (c, pair_w), 1) < GLA_DK

    a_pairs = [jnp.zeros((c, 2 * c), F32) for _ in range(2)]

    def add_level(a_pairs, qh, kh, mask2):
        qb = qh.astype(BF16)
        kb = kh.astype(BF16)
        out = []
        for p in range(2):
            qp = qb[:, p * pair_w:(p + 1) * pair_w]
            kp = kb[:, p * pair_w:(p + 1) * pair_w]
            k2 = jnp.concatenate([jnp.where(lo_lanes, kp, 0.0), jnp.where(lo_lanes, 0.0, kp)], axis=0)
            out.append(a_pairs[p] + mask2 * _dot_nt(qp, k2))
        return out

    pre = g
    tot = g
    for lvl in range(GLA_LEVELS):
        s = 1 << lvl
        odd = ((t_kw >> lvl) & 1) == 1
        if not reverse:
            qh = q * jnp.exp(pre)
            kh = k * jnp.exp(tot - pre)
        else:
            qh = q * jnp.exp(tot - pre + g)
            kh = k * jnp.exp(pre - g)
        a_pairs = add_level(a_pairs, qh, kh, mask_ref[lvl])
        sib = jnp.where(odd, pltpu.roll(tot, s, axis=0), pltpu.roll(tot, c - s, axis=0))
        pre = pre + jnp.where(odd, sib, 0.0)
        tot = tot + sib
    if not reverse:
        a_pairs = add_level(a_pairs, q, k, mask_ref[GLA_LEVELS])

    if not reverse:
        q_dec = q * jnp.exp(pre)
        k_dec = k * jnp.exp(tot - pre)
    else:
        q_dec = q * jnp.exp(tot - pre + g)
        k_dec = k * jnp.exp(pre - g)
    chunk_decay = jnp.exp(tot[0:1, :])
    sub_pair = lax.broadcasted_iota(I32, (2 * GLA_DV, pair_w), 0)
    lane_st = lax.broadcasted_iota(I32, (2 * GLA_DV, pair_w), 1)
    block_diag = (sub_pair // GLA_DV) == (lane_st // GLA_DK)
    first_head_v = lax.broadcasted_iota(I32, (c, 2 * GLA_DV), 1) < GLA_DV

    outs = []
    for p in range(2):
        st = st_ref[p]
        qd = q_dec[:, p * pair_w:(p + 1) * pair_w].astype(BF16)
        o_pair = _dot_nt(qd, st.astype(BF16))
        vp = v[:, p * 2 * GLA_DV:(p + 1) * 2 * GLA_DV].astype(BF16)
        kd = k_dec[:, p * pair_w:(p + 1) * pair_w].astype(BF16)
        upd = _dot_tn(vp, kd)
        st_ref[p] = st * chunk_decay[:, p * pair_w:(p + 1) * pair_w] + jnp.where(block_diag, upd, 0.0)
        v_diag = jnp.concatenate([jnp.where(first_head_v, vp, 0.0), jnp.where(first_head_v, 0.0, vp)], axis=0)
        outs.append(o_pair + _dot(a_pairs[p].astype(BF16), v_diag))
    return jnp.concatenate(outs, axis=1)


def _gla_level_masks(reverse):
    c = GLA_CHUNK
    t = np.arange(c)[:, None]
    s = np.arange(c)[None, :]
    masks = []
    for lvl in range(GLA_LEVELS):
        bt, bs = t >> lvl, s >> lvl
        masks.append((bs == bt + 1) & (bt % 2 == 0) if reverse else (bt == bs + 1) & (bs % 2 == 0))
    masks.append(t == s)
    m = np.stack(masks).astype(np.float32)
    return jnp.asarray(np.concatenate([m, m], axis=2))


def _gla_bwd_kernel(q_ref, k_ref, v_ref, g_ref, mask_ref, o_ref, st_ref):
    @pl.when(pl.program_id(1) == 0)
    def _():
        st_ref[...] = jnp.zeros(st_ref.shape, F32)

    for sub in reversed(range(GLA_STEP_CHUNKS)):
        rows = slice(sub * GLA_CHUNK, (sub + 1) * GLA_CHUNK)
        o_ref[0, rows, :] = _gla_chunk(q_ref[0, rows, :], k_ref[0, rows, :], v_ref[0, rows, :], g_ref[0, rows, :],
                                       st_ref, mask_ref, True)


def _gla_fwd_kernel(q_ref, k_ref, v_ref, g_ref, ob_ref, r_ref, gn_ref, mask_ref, o_ref, st_ref):
    @pl.when(pl.program_id(1) == 0)
    def _():
        st_ref[...] = jnp.zeros(st_ref.shape, F32)

    gn = gn_ref[...]
    for sub in range(GLA_STEP_CHUNKS):
        rows = slice(sub * GLA_CHUNK, (sub + 1) * GLA_CHUNK)
        o = _gla_chunk(q_ref[0, rows, :], k_ref[0, rows, :], v_ref[0, rows, :], g_ref[0, rows, :],
                       st_ref, mask_ref, False) + ob_ref[0, rows, :]
        r = r_ref[0, rows, :]
        gate = r * _sigmoid(r)
        outs = []
        for h in range(GLA_HEADS):
            oh = o[:, h * GLA_DV:(h + 1) * GLA_DV]
            outs.append(_rms(oh, gn))
        o_ref[0, rows, :] = (jnp.concatenate(outs, axis=1) * gate).astype(BF16)


def _gla(lq, lk, lv, lr, gf, gb, gla_norm):
    b, t, _ = lq.shape
    c = GLA_CHUNK * GLA_STEP_CHUNKS
    nc = t // c
    fwd = lambda wd: pl.BlockSpec((1, c, wd), lambda bi, ci: (bi, ci, 0))
    rev = lambda wd: pl.BlockSpec((1, c, wd), lambda bi, ci: (bi, nc - 1 - ci, 0))
    st = pltpu.VMEM((2, 2 * GLA_DV, 2 * GLA_DK), F32)
    mask_b = _gla_level_masks(True)
    mask_f = _gla_level_masks(False)
    o_b = pl.pallas_call(
        _gla_bwd_kernel,
        grid=(b, nc),
        in_specs=[rev(GLA_KW), rev(GLA_KW), rev(GLA_VW), rev(GLA_KW), _full_spec(mask_b.shape)],
        out_specs=rev(GLA_VW),
        out_shape=jax.ShapeDtypeStruct((b, t, GLA_VW), F32),
        scratch_shapes=[st],
        compiler_params=_cparams(("parallel", "arbitrary")),
        name="gla_bwd",
    )(lq, lk, lv, gb, mask_b)
    return pl.pallas_call(
        _gla_fwd_kernel,
        grid=(b, nc),
        in_specs=[fwd(GLA_KW), fwd(GLA_KW), fwd(GLA_VW), fwd(GLA_KW), fwd(GLA_VW), fwd(GLA_VW),
                  _full_spec(gla_norm.shape), _full_spec(mask_f.shape)],
        out_specs=fwd(GLA_VW),
        out_shape=jax.ShapeDtypeStruct((b, t, GLA_VW), BF16),
        scratch_shapes=[st],
        compiler_params=_cparams(("parallel", "arbitrary")),
        name="gla_fwd",
    )(lq, lk, lv, gf, o_b, lr, gla_norm, mask_f)


def _post_mix_kernel(x_ref, oa_ref, og_ref, gmix_ref, wga_ref, wgl_ref, wau_ref, wgu_ref, wout_ref, gmoe_ref,
                     wr_hi_ref, wr_lo_ref, x1_ref, hm_ref, aff_ref):
    x = x_ref[...]
    h = _rms(x, gmix_ref[...]).astype(BF16)
    merged = (_sigmoid(_dot(h, wga_ref[...])) * _dot(oa_ref[...], wau_ref[...])
              + _sigmoid(_dot(h, wgl_ref[...])) * _dot(og_ref[...], wgu_ref[...]))
    x1 = x + _dot(merged.astype(BF16), wout_ref[...])
    x1_ref[...] = x1
    hm = _rms(x1, gmoe_ref[...])
    hm_ref[...] = hm
    h_hi, h_lo = _split2(hm)
    wr_hi = wr_hi_ref[...]
    both = _dot_nt(jnp.concatenate([wr_hi, wr_lo_ref[...]], axis=0), h_hi)
    logits = both[:N_EXPERTS] + both[N_EXPERTS:] + _dot_nt(wr_hi, h_lo)
    m = jnp.max(logits, axis=0, keepdims=True)
    e = jnp.exp(logits - m)
    aff_ref[...] = e / jnp.sum(e, axis=0, keepdims=True)


def _post_mix(x2d, oa, og, w, tm=512):
    n, d = x2d.shape
    row = lambda wd: pl.BlockSpec((tm, wd), lambda i: (i, 0))
    consts = [w["gmix"], w["wga"], w["wgl"], w["wau"], w["wgu"], w["wout"], w["gmoe"], w["wr_hi"], w["wr_lo"]]
    return pl.pallas_call(
        _post_mix_kernel,
        grid=(n // tm,),
        in_specs=[row(d), row(ATTN_W), row(GLA_VW)] + [_full_spec(c.shape) for c in consts],
        out_specs=[row(d), row(d), pl.BlockSpec((N_EXPERTS, tm), lambda i: (0, i))],
        out_shape=[jax.ShapeDtypeStruct((n, d), F32), jax.ShapeDtypeStruct((n, d), F32),
                   jax.ShapeDtypeStruct((N_EXPERTS, n), F32)],
        compiler_params=_cparams(("parallel",)),
        name="post_mix",
    )(x2d, oa, og, *consts)


ROUTE_SLOTS = 256


def _route_kernel(a_ref, idx_ref, gate_ref, pos_ref, lo_ref, *, cap):
    a = a_ref[0]
    nrow = a.shape[0]
    b_io = lax.broadcasted_iota(I32, (nrow, LANES), 0)
    t_io = lax.broadcasted_iota(I32, (nrow, LANES), 1)
    tok = b_io * LANES + t_io

    def as_float(word):
        return lax.bitcast_convert_type(jnp.full((1, LANES), word, I32), F32)

    def thr_step(i, thr):
        cand = thr | lax.shift_left(jnp.int32(1), jnp.int32(30) - i)
        cnt = jnp.sum((a >= as_float(cand)).astype(I32))
        return jnp.where(cnt >= cap, cand, thr)

    thr = lax.fori_loop(0, 31, thr_step, jnp.int32(0))
    gt = a >= as_float(thr + 1)
    eq = (a >= as_float(thr)) & jnp.logical_not(gt)
    need = cap - jnp.sum(gt.astype(I32))

    def tie_step(i, lim):
        cand = lim + lax.shift_left(jnp.int32(1), jnp.int32(17) - i)
        cnt = jnp.sum((eq & (tok < cand)).astype(I32))
        return jnp.where(cnt < need, cand, lim)

    lim = lax.fori_loop(0, 18, tie_step, jnp.int32(0))
    selm = gt | (eq & (tok <= lim))
    selb = selm.astype(F32).astype(BF16)

    r_io = lax.broadcasted_iota(I32, (LANES, LANES), 0)
    c_io = lax.broadcasted_iota(I32, (LANES, LANES), 1)
    utri = (r_io <= c_io).astype(BF16)
    cum_row = _dot(selb, utri)
    row_cnt = cum_row[:, LANES - 1:LANES]
    br_io = lax.broadcasted_iota(I32, (nrow, nrow), 0)
    bc_io = lax.broadcasted_iota(I32, (nrow, nrow), 1)
    lrow = (bc_io <= br_io).astype(BF16)
    row_incl = _dot(lrow, jnp.broadcast_to(row_cnt, (nrow, LANES)).astype(BF16))
    row_excl = row_incl - row_cnt
    pos_ref[0] = jnp.where(selm, row_excl + cum_row - 1.0, -1.0).astype(I32)
    lo_ref[0] = row_excl.astype(I32)
    row_incl_c = row_incl[:, 0:1]
    row_excl_c = row_excl[:, 0:1]
    cum_t = cum_row.T.astype(BF16)
    a_hi, a_mid, a_lo = _split3(a.T)
    sw = ROUTE_SLOTS
    brow = lax.broadcasted_iota(I32, (nrow, sw), 0).astype(F32)
    tsub = lax.broadcasted_iota(I32, (LANES, sw), 0).astype(F32)
    lane = lax.broadcasted_iota(I32, (1, sw), 1).astype(F32)

    def slot_tile(jt, carry):
        j = lane + jnp.asarray(jt, F32) * float(sw)
        bj = jnp.sum((row_incl_c <= j).astype(F32), axis=0, keepdims=True)
        onehot = (brow == bj).astype(F32)
        base = jnp.sum(onehot * row_excl_c, axis=0, keepdims=True)
        jl = j - base
        ohb = onehot.astype(BF16)
        rt = _dot(cum_t, ohb)
        tj = jnp.sum((rt <= jl).astype(F32), axis=0, keepdims=True)
        acol = _dot(a_hi, ohb) + _dot(a_mid, ohb) + _dot(a_lo, ohb)
        gate = jnp.sum(jnp.where(tsub == tj, acol, 0.0), axis=0, keepdims=True)
        tok_id = (bj * float(LANES) + tj).astype(I32)
        for c in range(sw // LANES):
            row = jt * (sw // LANES) + c
            idx_ref[0, pl.ds(row, 1), :] = tok_id[:, c * LANES:(c + 1) * LANES]
            gate_ref[0, pl.ds(row, 1), :] = gate[:, c * LANES:(c + 1) * LANES]
        return carry

    lax.fori_loop(0, cap // sw, slot_tile, 0)


def _route(aff_t, cap):
    e, n = aff_t.shape
    nrow = n // LANES
    a3 = aff_t.reshape(e, nrow, LANES)
    nt = cap // LANES
    slot_spec = pl.BlockSpec((1, nt, LANES), lambda i: (i, 0, 0))
    tok_spec = pl.BlockSpec((1, nrow, LANES), lambda i: (i, 0, 0))
    idx, gate, pos, lo = pl.pallas_call(
        functools.partial(_route_kernel, cap=cap),
        grid=(e,),
        in_specs=[tok_spec],
        out_specs=[slot_spec, slot_spec, tok_spec, tok_spec],
        out_shape=[jax.ShapeDtypeStruct((e, nt, LANES), I32), jax.ShapeDtypeStruct((e, nt, LANES), F32),
                   jax.ShapeDtypeStruct((e, nrow, LANES), I32), jax.ShapeDtypeStruct((e, nrow, LANES), I32)],
        compiler_params=_cparams(("parallel",)),
        name="route",
    )(a3)
    lo_tab = jnp.concatenate([lo[:, :, 0], jnp.full((e, 1), cap, I32)], axis=1)
    return idx, gate, pos.reshape(e, n), lo_tab


MOE_ISSUE_UNROLL = 8


def _moe_kernel(idx_ref, idxn_ref, gate_ref, hm_hbm, wg_ref, wu_ref, wd_ref, ye_ref, buf_a, buf_b, sem, *, rows):
    nblk = pl.num_programs(1)
    s = pl.program_id(0) * nblk + pl.program_id(1)
    nsteps = pl.num_programs(0) * nblk

    def buffer_wait(buf, k):
        pltpu.make_async_copy(hm_hbm.at[pl.ds(0, rows)], buf, sem.at[k]).wait()

    @pl.when(s == 0)
    def _():
        def body(r0, c):
            for u in range(MOE_ISSUE_UNROLL):
                r = r0 * MOE_ISSUE_UNROLL + u
                n = idx_ref[0, 0, r]
                pltpu.make_async_copy(hm_hbm.at[pl.ds(n, 1)], buf_a.at[pl.ds(r, 1)], sem.at[0]).start()
            return c
        lax.fori_loop(0, rows // MOE_ISSUE_UNROLL, body, 0)

    def step(cur, k_cur, nxt, k_nxt):
        buffer_wait(cur, k_cur)

        @pl.when(s < nsteps)
        def _():
            for r in range(rows):
                n = idxn_ref[0, 0, r]
                pltpu.make_async_copy(hm_hbm.at[pl.ds(n, 1)], nxt.at[pl.ds(r, 1)],
                                      sem.at[k_nxt]).start(priority=r % 2)

        xe = cur[...].astype(BF16)
        hid_g = _dot(xe, wg_ref[0])
        hid = (hid_g * _sigmoid(hid_g)) * _dot(xe, wu_ref[0])
        y = _dot(hid.astype(BF16), wd_ref[0])
        eye = (lax.broadcasted_iota(I32, (LANES, LANES), 0) == lax.broadcasted_iota(I32, (LANES, LANES), 1))
        for c in range(rows // LANES):
            g_row = gate_ref[0, :, c * LANES:(c + 1) * LANES]
            g_col = jnp.sum(jnp.where(eye, g_row, 0.0), axis=1, keepdims=True)
            sl = slice(c * LANES, (c + 1) * LANES)
            ye_ref[sl, :] = (y[sl, :] * g_col).astype(BF16)

        @pl.when(s == nsteps - 1)
        def _():
            buffer_wait(nxt, k_nxt)

    @pl.when(lax.rem(s, 2) == 0)
    def _():
        step(buf_a, 0, buf_b, 1)

    @pl.when(lax.rem(s, 2) == 1)
    def _():
        step(buf_b, 1, buf_a, 0)


def _moe_ffn(idx, gate, hm, wg, wu, wd, rows=512):
    e, nt, _ = idx.shape
    cap = nt * LANES
    n, d = hm.shape
    nblk = cap // rows
    nsteps = e * nblk
    idx3 = idx.reshape(nsteps, 1, rows)
    gate3 = gate.reshape(nsteps, 1, rows)
    ff = wg.shape[-1]
    return pl.pallas_call(
        functools.partial(_moe_kernel, rows=rows),
        grid=(e, nblk),
        in_specs=[pl.BlockSpec((1, 1, rows), lambda ei, j: (ei * nblk + j, 0, 0), memory_space=pltpu.SMEM),
                  pl.BlockSpec((1, 1, rows), lambda ei, j: (jnp.minimum(ei * nblk + j + 1, nsteps - 1), 0, 0),
                               memory_space=pltpu.SMEM),
                  pl.BlockSpec((1, 1, rows), lambda ei, j: (ei * nblk + j, 0, 0)),
                  pl.BlockSpec(memory_space=pl.ANY),
                  pl.BlockSpec((1, d, ff), lambda ei, j: (ei, 0, 0)),
                  pl.BlockSpec((1, d, ff), lambda ei, j: (ei, 0, 0)),
                  pl.BlockSpec((1, ff, d), lambda ei, j: (ei, 0, 0))],
        out_specs=pl.BlockSpec((rows, d), lambda ei, j: (ei * nblk + j, 0)),
        out_shape=jax.ShapeDtypeStruct((e * cap, d), BF16),
        scratch_shapes=[pltpu.VMEM((rows, d), F32), pltpu.VMEM((rows, d), F32), pltpu.SemaphoreType.DMA((2,))],
        compiler_params=_cparams(("arbitrary", "arbitrary")),
        name="moe_ffn",
    )(idx3, idx3, gate3, hm, wg, wu, wd)


COMBINE_WIN = 64


def _ple_final_kernel(lo_ref, x_ref, p_ref, pos_ref, ye_hbm, gple_ref, wpg_ref, wpp_ref, gfin_ref, o_ref,
                      win_a, win_b, acc_ref, sem, *, cap, rows_per_tile):
    i = pl.program_id(0)
    ntiles = pl.num_programs(0)
    win = COMBINE_WIN
    tm = x_ref.shape[0]

    def tile_floors(ti):
        return [jnp.bitwise_and(lo_ref[e, ti * rows_per_tile], -BF16_ROWS) for e in range(N_EXPERTS)]

    def clamp(floors):
        return [jnp.minimum(f, cap - win) for f in floors]

    def start_fetch(floors, buf, k):
        starts = clamp(floors)
        for e in range(N_EXPERTS):
            src = ye_hbm.at[pl.ds(pl.multiple_of(e * cap + starts[e], BF16_ROWS), win)]
            pltpu.make_async_copy(src, buf.at[pl.ds(e * win, win)], sem.at[k]).start()

    def wait_fetch(buf, k):
        pltpu.make_async_copy(ye_hbm.at[pl.ds(0, N_EXPERTS * win)], buf, sem.at[k]).wait()

    def expand(floors, buf):
        starts = clamp(floors)
        pos = pos_ref[...]
        w_io = lax.broadcasted_iota(I32, (win, tm), 0)
        parts = []
        for e in range(N_EXPERTS):
            pe = pos[e:e + 1, :]
            hit = ((pe - starts[e]) == w_io) & (pe >= floors[e])
            parts.append(jnp.where(hit, 1.0, 0.0).astype(BF16))
        onehot_t = jnp.concatenate(parts, axis=0)
        return _dot_tn(onehot_t, buf[...])

    @pl.when(i == 0)
    def _():
        start_fetch(tile_floors(0), win_a, 0)

    def step(cur, k_cur, nxt, k_nxt):
        floors0 = tile_floors(i)
        wait_fetch(cur, k_cur)

        @pl.when(i + 1 < ntiles)
        def _():
            start_fetch(tile_floors(i + 1), nxt, k_nxt)

        acc_ref[...] = x_ref[...] + expand(floors0, cur)

        nrounds = jnp.int32(1)
        for e in range(N_EXPERTS):
            hi = lo_ref[e, (i + 1) * rows_per_tile]
            nrounds = jnp.maximum(nrounds, lax.div(hi - floors0[e] + (win - 1), jnp.int32(win)))

        def extra_round(r, c):
            floors = [f + r * win for f in floors0]
            start_fetch(floors, cur, k_cur)
            wait_fetch(cur, k_cur)
            acc_ref[...] += expand(floors, cur)
            return c

        lax.fori_loop(1, nrounds, extra_round, 0)

        x = acc_ref[...]
        hp = _rms(x, gple_ref[...]).astype(BF16)
        gate = _sigmoid(_dot(hp, wpg_ref[...]))
        y = x + gate * _dot(p_ref[...].astype(BF16), wpp_ref[...])
        o_ref[...] = _rms(y, gfin_ref[...])

    @pl.when(lax.rem(i, 2) == 0)
    def _():
        step(win_a, 0, win_b, 1)

    @pl.when(lax.rem(i, 2) == 1)
    def _():
        step(win_b, 1, win_a, 0)


def _ple_final(x1, p2d, pos, lo_tab, ye, cap, w, tm=256):
    n, d = x1.shape
    pd = p2d.shape[-1]
    row = lambda wd: pl.BlockSpec((tm, wd), lambda i, lo: (i, 0))
    consts = [w["gple"], w["wpg"], w["wpp"], w["gfin"]]
    const_spec = lambda c: pl.BlockSpec(c.shape, lambda i, lo: (0,) * c.ndim)
    grid_spec = pltpu.PrefetchScalarGridSpec(
        num_scalar_prefetch=1,
        grid=(n // tm,),
        in_specs=[row(d), row(pd), pl.BlockSpec((N_EXPERTS, tm), lambda i, lo: (0, i)),
                  pl.BlockSpec(memory_space=pl.ANY)] + [const_spec(c) for c in consts],
        out_specs=row(d),
        scratch_shapes=[pltpu.VMEM((N_EXPERTS * COMBINE_WIN, d), BF16),
                        pltpu.VMEM((N_EXPERTS * COMBINE_WIN, d), BF16), pltpu.VMEM((tm, d), F32),
                        pltpu.SemaphoreType.DMA((2,))],
    )
    return pl.pallas_call(
        functools.partial(_ple_final_kernel, cap=cap, rows_per_tile=tm // LANES),
        grid_spec=grid_spec,
        out_shape=jax.ShapeDtypeStruct((n, d), F32),
        compiler_params=_cparams(("arbitrary",)),
        name="ple_final",
    )(lo_tab, x1, p2d, pos, ye, *consts)


def _rope_tables(T):
    half = HEAD_DIM // 2
    inv = ROPE_THETA ** (-jnp.arange(0, half, 2, dtype=F32) / half)
    t = jnp.arange(T)
    ang_r = (t // GRID_W).astype(F32)[:, None] * inv
    ang_c = (t % GRID_W).astype(F32)[:, None] * inv
    cos = jnp.concatenate([jnp.cos(ang_r)] * 2 + [jnp.cos(ang_c)] * 2, axis=1)
    sin = jnp.concatenate([-jnp.sin(ang_r), jnp.sin(ang_r), -jnp.sin(ang_c), jnp.sin(ang_c)], axis=1)
    return jnp.tile(cos, (1, N_Q_HEADS)), jnp.tile(sin, (1, N_Q_HEADS))


def _block_diag_ones(width):
    i = np.arange(width)
    return jnp.asarray((i[:, None] // HEAD_DIM) == (i[None, :] // HEAD_DIM), BF16)


def _prep_weights(g_mix, w_in, q_norm, k_norm, w_af, b_af, w_ab, b_ab, gla_norm, w_attn_up, w_gla_up,
                  w_out, g_moe, w_router, w_gate_e, w_up_e, w_down_e, g_ple, w_ple_gate, w_ple_proj,
                  g_final):
    d = w_in.shape[0]
    o_gla = ATTN_W + 2 * KV_W
    o_lr = o_gla + 2 * GLA_KW + 2 * GLA_VW
    o_ga = o_lr + 2 * GLA_LOWRANK
    wlr = jnp.zeros((d, LANES), F32).at[:, :2 * GLA_LOWRANK].set(w_in[:, o_lr:o_ga])
    wgate = jnp.zeros((LANES, 2 * GLA_KW), F32)
    wgate = wgate.at[:GLA_LOWRANK, :GLA_KW].set(w_af)
    wgate = wgate.at[GLA_LOWRANK:2 * GLA_LOWRANK, GLA_KW:].set(w_ab)
    wr_t = w_router.T
    wr_hi = wr_t.astype(BF16)
    wr_lo = (wr_t - wr_hi.astype(F32)).astype(BF16)
    return dict(
        gmix=g_mix.reshape(1, d),
        wqk=w_in[:, :ATTN_W + KV_W].astype(BF16),
        wvt=w_in[:, ATTN_W + KV_W:o_gla].T.astype(BF16),
        wgla=w_in[:, o_gla:o_lr].astype(BF16),
        wlr=wlr.astype(BF16),
        wga=w_in[:, o_ga:o_ga + d].astype(BF16),
        wgl=w_in[:, o_ga + d:].astype(BF16),
        wgate=wgate.astype(BF16),
        bgate=jnp.concatenate([b_af, b_ab]).reshape(1, 2 * GLA_KW),
        qn=jnp.tile(q_norm, N_Q_HEADS).reshape(1, ATTN_W),
        kn=jnp.tile(k_norm, N_KV_HEADS).reshape(1, KV_W),
        bdq=_block_diag_ones(ATTN_W),
        bdk=_block_diag_ones(KV_W),
        gnorm=gla_norm.reshape(1, GLA_DV),
        wau=w_attn_up.astype(BF16),
        wgu=w_gla_up.astype(BF16),
        wout=w_out.astype(BF16),
        gmoe=g_moe.reshape(1, d),
        wr_hi=wr_hi,
        wr_lo=wr_lo,
        wge=w_gate_e.astype(BF16),
        wue=w_up_e.astype(BF16),
        wde=w_down_e.astype(BF16),
        gple=g_ple.reshape(1, d),
        wpg=w_ple_gate.astype(BF16),
        wpp=w_ple_proj.astype(BF16),
        gfin=g_final.reshape(1, d),
    )


def _trunk(x, p, w):
    b, t, d = x.shape
    n = b * t
    x2d = x.reshape(n, d)
    q, k, vt, lq, lk, lv, lr, gf, gb = _in_proj(x2d, t, _rope_tables(t), w)
    r3 = lambda a: a.reshape(b, t, a.shape[-1])
    o_attn = _attention(r3(q), r3(k), vt).reshape(n, ATTN_W)
    o_gla = _gla(r3(lq), r3(lk), r3(lv), r3(lr), r3(gf), r3(gb), w["gnorm"]).reshape(n, GLA_VW)
    x1, hm, aff_t = _post_mix(x2d, o_attn, o_gla, w)
    cap = EC_CAPACITY * n // N_EXPERTS
    idx, gate, pos, lo_tab = _route(aff_t, cap)
    ye = _moe_ffn(idx, gate, hm, w["wge"], w["wue"], w["wde"])
    y = _ple_final(x1, p.reshape(n, p.shape[-1]), pos, lo_tab, ye, cap, w)
    return y.reshape(b, t, d)


def kernel(x_prompt, x_sample, p_prompt, p_sample, g_mix, w_in, q_norm, k_norm, w_af, b_af, w_ab, b_ab,
           gla_norm, w_attn_up, w_gla_up, w_out, g_moe, w_router, w_gate_e, w_up_e, w_down_e, g_ple,
           w_ple_gate, w_ple_proj, g_final):
    assert g_mix.shape[0] == 1, "single layer"
    w = _prep_weights(g_mix[0], w_in[0], q_norm[0], k_norm[0], w_af[0], b_af[0], w_ab[0], b_ab[0],
                      gla_norm[0], w_attn_up[0], w_gla_up[0], w_out[0], g_moe[0], w_router[0],
                      w_gate_e[0], w_up_e[0], w_down_e[0], g_ple[0], w_ple_gate[0], w_ple_proj[0], g_final)
    return (_trunk(x_prompt, p_prompt[0], w), _trunk(x_sample, p_sample[0], w))
```
